```python
import jax, jax.numpy as jnp
from jax import lax
import numpy as np

D_MODEL = 1024
BATCH = 32
SEQ = 2048
DEPTH = 1
DEC_BATCH = 128
DEC_SEQ = 4
PAST_LEN = 16384
PAGE_SIZE = 128

D_PLE = 256
D_MIX = D_MODEL
H_A = 8
NOPE = 64
ROPE = 32
V_DIM = 64
Q_LORA = 384
KV_LORA = 256
ROPE_BASE = 10000.0
Q_BLOCK = 128
SCALE = (NOPE + ROPE) ** -0.5
H_B = 8
N_B = 64
DECAY_LORA = 64
ICLR_LORA = 64
LNX_EPS = 64e-5
W_A = H_A * V_DIM
W_B = H_B * N_B
EPS = 1e-6
NEG = -1e30

C_Q = 0
C_KV = C_Q + Q_LORA
C_KPE = C_KV + KV_LORA
C_GA = C_KPE + ROPE
C_GB = C_GA + W_A
C_SHIFT = C_GB + W_B
C_TOTAL = C_SHIFT + 3 * W_B + DECAY_LORA + ICLR_LORA
N_SHIFT = C_TOTAL - C_SHIFT

kernel_name = "hymba_mla_rwkv7_step"


def _rmsnorm(x, g):
    xf = x.astype(jnp.float32)
    y = xf * lax.rsqrt(jnp.mean(xf * xf, axis=-1, keepdims=True) + EPS)
    return (y * g).astype(x.dtype)


def _rope_tables(pos):
    inv = ROPE_BASE ** (-jnp.arange(0, ROPE, 2, dtype=jnp.float32) / ROPE)
    ang = pos.astype(jnp.float32)[:, None] * inv[None, :]
    return jnp.cos(ang), jnp.sin(ang)


def _apply_rope(x, cos, sin):
    x1, x2 = x[..., :ROPE // 2], x[..., ROPE // 2:]
    return jnp.concatenate([x1 * cos - x2 * sin, x2 * cos + x1 * sin], axis=-1).astype(x.dtype)


def _front(x, prev_proj, pos, norm_in, w_in, mu_shift, q_a_norm, w_uq, kv_a_norm, g_qn, g_qp, g_kp):
    xn = _rmsnorm(x, norm_in)
    p_all = jnp.einsum('btd,dc->btc', xn, w_in)
    cur = p_all[..., C_SHIFT:]
    prev = jnp.concatenate([prev_proj[:, None, :].astype(cur.dtype), cur[:, :-1]], axis=1)
    shifted = cur + (prev - cur) * mu_shift
    cos, sin = _rope_tables(pos)
    c_q = _rmsnorm(p_all[..., C_Q:C_KV], q_a_norm)
    c_kv = _rmsnorm(p_all[..., C_KV:C_KPE], kv_a_norm)
    q = jnp.einsum('btc,chd->bthd', c_q, w_uq)
    q_nope = _rmsnorm(q[..., :NOPE], g_qn)
    q_pe = _apply_rope(_rmsnorm(q[..., NOPE:], g_qp), cos[None, :, None], sin[None, :, None])
    k_pe = _apply_rope(_rmsnorm(p_all[..., C_KPE:C_GA], g_kp), cos[None], sin[None])
    gate_a = p_all[..., C_GA:C_GB]
    gate_b = p_all[..., C_GB:C_SHIFT]
    return xn, q_nope, q_pe, c_kv, k_pe, gate_a, gate_b, shifted


def _decompress(c_kv, w_ukv, g_kn):
    kv = jnp.einsum('btc,chd->bthd', c_kv, w_ukv)
    return _rmsnorm(kv[..., :NOPE], g_kn), kv[..., NOPE:]


def _attn_partial(qn, qp, kn, kp, v, mask):
    s = (jnp.einsum('bqhd,bkhd->bhqk', qn, kn) + jnp.einsum('bqhr,bkr->bhqk', qp, kp)).astype(jnp.float32) * SCALE
    if mask is not None:
        s = jnp.where(mask, s, NEG)
    m = jnp.max(s, axis=-1)
    e = jnp.exp(s - m[..., None])
    return m, jnp.sum(e, axis=-1), jnp.einsum('bhqk,bkhd->bhqd', e, v.astype(jnp.float32))


def _merge(p1, p2):
    m1, l1, a1 = p1
    m2, l2, a2 = p2
    m = jnp.maximum(m1, m2)
    s1, s2 = jnp.exp(m1 - m), jnp.exp(m2 - m)
    return m, l1 * s1 + l2 * s2, a1 * s1[..., None] + a2 * s2[..., None]


def _prompt_attention(qn, qp, kn, kp, v):
    B, T = qn.shape[0], qn.shape[1]
    kpos = jnp.arange(T)

    def block(i):
        start = i * Q_BLOCK
        qn_b = lax.dynamic_slice_in_dim(qn, start, Q_BLOCK, axis=1)
        qp_b = lax.dynamic_slice_in_dim(qp, start, Q_BLOCK, axis=1)
        qpos = start + jnp.arange(Q_BLOCK)
        mask = (kpos[None, :] <= qpos[:, None])[None, None]
        _, l, acc = _attn_partial(qn_b, qp_b, kn, kp, v, mask)
        return acc / l[..., None]

    o = lax.map(block, jnp.arange(T // Q_BLOCK))
    return jnp.transpose(o, (1, 0, 3, 2, 4)).reshape(B, T, W_A).astype(qn.dtype)


def _sample_attention(qn, qp, kn_new, kp_new, v_new, cache_ckv, cache_kpe, layer, page_table, w_ukv, g_kn):
    Bd, Tq = qn.shape[0], qn.shape[1]
    init = (jnp.full((Bd, H_A, Tq), NEG, jnp.float32), jnp.zeros((Bd, H_A, Tq), jnp.float32),
            jnp.zeros((Bd, H_A, Tq, V_DIM), jnp.float32))

    def step(carry, phys):
        c_blk = cache_ckv[layer, phys]
        kp_blk = cache_kpe[layer, phys]
        kn_blk, v_blk = _decompress(c_blk, w_ukv, g_kn)
        return _merge(carry, _attn_partial(qn, qp, kn_blk, kp_blk, v_blk, None)), None

    carry, _ = lax.scan(step, init, page_table.T)
    causal = (jnp.arange(Tq)[None, :] <= jnp.arange(Tq)[:, None])[None, None]
    _, l, acc = _merge(carry, _attn_partial(qn, qp, kn_new, kp_new, v_new, causal))
    o = acc / l[..., None]
    return jnp.transpose(o, (0, 2, 1, 3)).reshape(Bd, Tq, W_A).astype(qn.dtype)


def _wkv7_scan(r, w, k, v, kk, a, S0):
    xs = tuple(jnp.swapaxes(t.astype(jnp.float32), 0, 1) for t in (r, w, k, v, kk, a))

    def step(S, inp):
        r_t, w_t, k_t, v_t, kk_t, a_t = inp
        sa = jnp.einsum('bhvk,bhk->bhv', S, kk_t)
        S = S * w_t[:, :, None, :] - sa[..., None] * (kk_t * a_t)[:, :, None, :] + v_t[..., None] * k_t[:, :, None, :]
        return S, jnp.einsum('bhvk,bhk->bhv', S, r_t)

    S, y = lax.scan(step, S0.astype(jnp.float32), xs)
    return jnp.swapaxes(y, 0, 1), S


def _rwkv(shifted, gate_b, S0, w_decay0, w_decay_up, a0, w_iclr_up, k_k, k_a, r_k, lnx_w, lnx_b):
    B, T = shifted.shape[0], shifted.shape[1]
    hs = lambda t: t.reshape(B, T, H_B, N_B)
    r = shifted[..., 0:W_B]
    k = shifted[..., W_B:2 * W_B]
    v = shifted[..., 2 * W_B:3 * W_B]
    xw = shifted[..., 3 * W_B:3 * W_B + DECAY_LORA]
    xa = shifted[..., 3 * W_B + DECAY_LORA:]
    w_log = -jax.nn.softplus(-(w_decay0 + jnp.tanh(xw) @ w_decay_up).astype(jnp.float32)) - 0.5
    decay = jnp.exp(-jnp.exp(w_log))
    a = jax.nn.sigmoid((a0 + xa @ w_iclr_up).astype(jnp.float32))
    kk = hs(k * k_k).astype(jnp.float32)
    kk = kk / jnp.maximum(jnp.sqrt(jnp.sum(kk * kk, axis=-1, keepdims=True)), 1e-12)
    k_mod = hs(k.astype(jnp.float32) * (1.0 + (a - 1.0) * k_a))
    r_h, v_h = hs(r).astype(jnp.float32), hs(v).astype(jnp.float32)
    y, S = _wkv7_scan(r_h, hs(decay), k_mod, v_h, kk, hs(a), S0)
    mean = jnp.mean(y, axis=-1, keepdims=True)
    var = jnp.mean(jnp.square(y - mean), axis=-1, keepdims=True)
    y = ((y - mean) * lax.rsqrt(var + LNX_EPS)).reshape(B, T, W_B) * lnx_w + lnx_b
    bonus = jnp.sum(r_h * k_mod * r_k, axis=-1, keepdims=True) * v_h
    y = (y + bonus.reshape(B, T, W_B)) * jax.nn.silu(gate_b.astype(jnp.float32))
    return y.astype(shifted.dtype), S


def _back(x, o_a, gate_a, o_b, w_out, p_l, w_ple, ple_norm, w_ple_gate):
    mixed = jnp.concatenate([o_a * jax.nn.silu(gate_a), o_b], axis=-1)
    x1 = x + jnp.einsum('btc,cd->btd', mixed, w_out)
    g = jax.nn.sigmoid(jnp.einsum('btd,de->bte', _rmsnorm(x1, ple_norm), w_ple_gate))
    return x1 + jnp.einsum('btp,pd->btd', p_l, w_ple) * g


def setup_inputs(seed: int = 0) -> dict:
    key = jax.random.key(seed)
    ks = iter(jax.random.split(key, 48))
    f32 = jnp.float32
    nrm = lambda shape, scale: jax.random.normal(next(ks), shape, f32) * scale
    gain = lambda shape: 1.0 + nrm(shape, 0.05)
    n_pages = PAST_LEN // PAGE_SIZE
    n_used = DEC_BATCH * n_pages
    n_pool = n_used + max(n_used // 4, 1)
    perm = jax.random.permutation(next(ks), n_pool)
    page_table = perm[:n_used].reshape(DEC_BATCH, n_pages).astype(jnp.int32)
    return {
        "x_prompt": nrm((BATCH, SEQ, D_MODEL), 1.0),
        "x_sample": nrm((DEC_BATCH, DEC_SEQ, D_MODEL), 1.0),
        "p_prompt": nrm((DEPTH, BATCH, SEQ, D_PLE), 1.0),
        "p_sample": nrm((DEPTH, DEC_BATCH, DEC_SEQ, D_PLE), 1.0),
        "cache_ckv": nrm((DEPTH, n_pool, PAGE_SIZE, KV_LORA), 1.0),
        "cache_kpe": nrm((DEPTH, n_pool, PAGE_SIZE, ROPE), 1.0),
        "state_wkv": nrm((DEPTH, DEC_BATCH, H_B, N_B, N_B), 0.3),
        "state_shift": nrm((DEPTH, DEC_BATCH, D_MODEL), 1.0),
        "page_table": page_table,
        "norm_in": gain((DEPTH, D_MODEL)),
        "w_in": nrm((DEPTH, D_MODEL, C_TOTAL), D_MODEL ** -0.5),
        "mu_shift": jax.random.uniform(next(ks), (DEPTH, N_SHIFT), f32, 0.0, 1.0),
        "q_a_norm": gain((DEPTH, Q_LORA)),
        "w_uq": nrm((DEPTH, Q_LORA, H_A, NOPE + ROPE), Q_LORA ** -0.5),
        "kv_a_norm": gain((DEPTH, KV_LORA)),
        "w_ukv": nrm((DEPTH, KV_LORA, H_A, NOPE + V_DIM), KV_LORA ** -0.5),
        "g_q_nope": gain((DEPTH, NOPE)),
        "g_q_pe": gain((DEPTH, ROPE)),
        "g_k_nope": gain((DEPTH, NOPE)),
        "g_k_pe": gain((DEPTH, ROPE)),
        "w_decay0": jax.random.uniform(next(ks), (DEPTH, W_B), f32, -6.0, -1.0),
        "w_decay_up": nrm((DEPTH, DECAY_LORA, W_B), 0.1 * DECAY_LORA ** -0.5),
        "a0": nrm((DEPTH, W_B), 0.1),
        "w_iclr_up": nrm((DEPTH, ICLR_LORA, W_B), 0.5 * ICLR_LORA ** -0.5),
        "k_k": 0.85 + nrm((DEPTH, W_B), 0.05),
        "k_a": gain((DEPTH, W_B)),
        "r_k": nrm((DEPTH, H_B, N_B), 0.1),
        "lnx_w": gain((DEPTH, W_B)),
        "lnx_b": nrm((DEPTH, W_B), 0.01),
        "w_out": nrm((DEPTH, D_MIX, D_MODEL), D_MIX ** -0.5),
        "w_ple": nrm((DEPTH, D_PLE, D_MODEL), D_PLE ** -0.5),
        "ple_norm": gain((DEPTH, D_MODEL)),
        "w_ple_gate": nrm((DEPTH, D_MODEL, D_MODEL), D_MODEL ** -0.5),
    }


def reference(x_prompt, x_sample, p_prompt, p_sample, cache_ckv, cache_kpe, state_wkv, state_shift, page_table,
              norm_in, w_in, mu_shift, q_a_norm, w_uq, kv_a_norm, w_ukv, g_q_nope, g_q_pe, g_k_nope, g_k_pe,
              w_decay0, w_decay_up, a0, w_iclr_up, k_k, k_a, r_k, lnx_w, lnx_b, w_out, w_ple, ple_norm, w_ple_gate):
    B, T = x_prompt.shape[0], x_prompt.shape[1]
    Bd, Tq = x_sample.shape[0], x_sample.shape[1]
    pos_p = jnp.arange(T)
    pos_s = PAST_LEN + jnp.arange(Tq)
    y_p, y_s = x_prompt, x_sample
    ckv_pl, ckv_sl, kpe_pl, kpe_sl, wkv_pl, wkv_sl, sh_pl, sh_sl = [], [], [], [], [], [], [], []
    for i in range(DEPTH):
        front_w = (norm_in[i], w_in[i], mu_shift[i], q_a_norm[i], w_uq[i], kv_a_norm[i], g_q_nope[i], g_q_pe[i], g_k_pe[i])
        rwkv_w = (w_decay0[i], w_decay_up[i], a0[i], w_iclr_up[i], k_k[i], k_a[i], r_k[i], lnx_w[i], lnx_b[i])
        xn, qn, qp, ckv, kpe, ga, gb, sh = _front(y_p, jnp.zeros((B, N_SHIFT), y_p.dtype), pos_p, *front_w)
        kn, v = _decompress(ckv, w_ukv[i], g_k_nope[i])
        o_a = _prompt_attention(qn, qp, kn, kpe, v)
        o_b, S_p = _rwkv(sh, gb, jnp.zeros((B, H_B, N_B, N_B), jnp.float32), *rwkv_w)
        y_p = _back(y_p, o_a, ga, o_b, w_out[i], p_prompt[i], w_ple[i], ple_norm[i], w_ple_gate[i])
        ckv_pl.append(ckv); kpe_pl.append(kpe); wkv_pl.append(S_p); sh_pl.append(xn[:, -1])
        prev_proj = jnp.einsum('bd,dc->bc', state_shift[i], w_in[i][:, C_SHIFT:])
        xn_s, qn_s, qp_s, ckv_s, kpe_s, ga_s, gb_s, sh_s = _front(y_s, prev_proj, pos_s, *front_w)
        kn_s, v_s = _decompress(ckv_s, w_ukv[i], g_k_nope[i])
        o_a_s = _sample_attention(qn_s, qp_s, kn_s, kpe_s, v_s, cache_ckv, cache_kpe, i, page_table, w_ukv[i], g_k_nope[i])
        o_b_s, S_s = _rwkv(sh_s, gb_s, state_wkv[i], *rwkv_w)
        y_s = _back(y_s, o_a_s, ga_s, o_b_s, w_out[i], p_sample[i], w_ple[i], ple_norm[i], w_ple_gate[i])
        ckv_sl.append(ckv_s); kpe_sl.append(kpe_s); wkv_sl.append(S_s); sh_sl.append(xn_s[:, -1])
    ckv_prompt, ckv_sample = jnp.stack(ckv_pl), jnp.stack(ckv_sl)
    kpe_prompt, kpe_sample = jnp.stack(kpe_pl), jnp.stack(kpe_sl)
    wkv_prompt, wkv_sample = jnp.stack(wkv_pl), jnp.stack(wkv_sl)
    shift_prompt, shift_sample = jnp.stack(sh_pl), jnp.stack(sh_sl)
    return (y_p, y_s, ckv_prompt, ckv_sample, kpe_prompt, kpe_sample, wkv_prompt, wkv_sample, shift_prompt, shift_sample)
```

```python
import functools
import math

import jax
import jax.numpy as jnp
from jax import lax
from jax.experimental import pallas as pl
from jax.experimental.pallas import tpu as pltpu

F32 = jnp.float32
BF16 = jnp.bfloat16

LANES = 128
H_A = 8
NOPE = 64
ROPE = 32
V_DIM = 64
Q_LORA = 384
KV_LORA = 256
ROPE_BASE = 10000.0
SCALE = (NOPE + ROPE) ** -0.5
H_B = 8
N_B = 64
W_A = H_A * V_DIM
W_B = H_B * N_B
DECAY_LORA = 64
ICLR_LORA = 64
LNX_EPS = 64e-5
EPS = 1e-6
NEG = -1e30

P_Q = 0
P_KV = P_Q + Q_LORA
P_KPE = P_KV + KV_LORA
P_GA = P_KPE + LANES
P_GB = P_GA + W_A
P_SH = P_GB + W_B
N_SH = 3 * W_B + 2 * LANES
P_TOT = P_SH + N_SH

HCAT = H_A * LANES
RW_CHUNK = 64
RW_GROUP = 4 * N_B
VMEM_LIMIT = 56 * 1024 * 1024


def _cparams(sem):
    return pltpu.CompilerParams(dimension_semantics=sem, vmem_limit_bytes=VMEM_LIMIT)


def _lane_iota(n=LANES):
    return lax.broadcasted_iota(jnp.int32, (1, n), 1)


def _rms(x, g, n):
    ms = jnp.sum(x * x, axis=-1, keepdims=True) * (1.0 / n)
    return x * lax.rsqrt(ms + EPS) * g


def _rope128(n, c, s1, s2):
    return n * c + pltpu.roll(n, LANES - ROPE // 2, axis=1) * s1 + pltpu.roll(n, ROPE // 2, axis=1) * s2


def _front_body(x_ref, xlast_ref, prev0_ref, tab_ref, norm_in_ref, w_in_ref, mu_ref, qan_ref, wuq_ref, kvan_ref,
                wuk_ref, wuv_ref, gq_ref, gk_ref, gkp_ref, wd0_ref, wdu_ref, a0_ref, wiu_ref, kk_ref_, ka_ref,
                qcat_ref, kcat_ref, v_ref, ckv_ref, kpe_ref, ga_ref, gb_ref,
                r_ref, lw_ref, km_ref, vb_ref, kkn_ref, kka_ref, xl_ref,
                cur_scr, *, tm, tiles_per_seq, seq_len, sample):
    i = pl.program_id(0)
    x = x_ref[...]
    xn = _rms(x, norm_in_ref[...], x.shape[-1])
    xl_ref[...] = _rms(xlast_ref[...], norm_in_ref[...], x.shape[-1])
    p = jnp.dot(xn.astype(BF16), w_in_ref[...], preferred_element_type=F32)

    c_tab, s1_tab, s2_tab = tab_ref[0], tab_ref[1], tab_ref[2]
    lane = _lane_iota()
    is_rope = lane < ROPE

    c_q = _rms(p[:, P_Q:P_KV], qan_ref[...], Q_LORA)
    q = jnp.dot(c_q.astype(BF16), wuq_ref[...], preferred_element_type=F32)
    gq = gq_ref[...]
    for h in range(H_A):
        qb = q[:, h * LANES:(h + 1) * LANES]
        sq = qb * qb
        s_r = jnp.sum(jnp.where(is_rope, sq, 0.0), axis=-1, keepdims=True)
        s_n = jnp.sum(jnp.where(is_rope, 0.0, sq), axis=-1, keepdims=True)
        inv = jnp.where(is_rope, lax.rsqrt(s_r * (1.0 / ROPE) + EPS), lax.rsqrt(s_n * (1.0 / NOPE) + EPS))
        qn = qb * inv * gq
        qcat_ref[:, h * LANES:(h + 1) * LANES] = _rope128(qn, c_tab, s1_tab, s2_tab).astype(BF16)

    c_kv = _rms(p[:, P_KV:P_KPE], kvan_ref[...], KV_LORA)
    ckv_ref[...] = c_kv
    c_kv16 = c_kv.astype(BF16)
    v_ref[...] = jnp.dot(c_kv16, wuv_ref[...], preferred_element_type=F32).astype(BF16)
    kraw = jnp.dot(c_kv16, wuk_ref[...], preferred_element_type=F32)
    kp = p[:, P_KPE:P_GA]
    kp_n = kp * lax.rsqrt(jnp.sum(kp * kp, axis=-1, keepdims=True) * (1.0 / ROPE) + EPS) * gkp_ref[...]
    kp_r = _rope128(kp_n, c_tab, s1_tab, s2_tab)
    kpe_ref[...] = kp_r[:, :ROPE]
    gk = gk_ref[...]
    for h in range(H_A):
        kb = kraw[:, h * LANES:(h + 1) * LANES]
        s_n = jnp.sum(kb * kb, axis=-1, keepdims=True)
        kn = kb * lax.rsqrt(s_n * (1.0 / NOPE) + EPS) * gk
        kcat_ref[:, h * LANES:(h + 1) * LANES] = (kn + kp_r).astype(BF16)

    ga_ref[...] = p[:, P_GA:P_GB]
    gb_ref[...] = p[:, P_GB:P_SH]

    cur = p[:, P_SH:P_TOT]
    if sample:
        cur_scr[8:8 + tm, :] = cur
        cur_scr[7:8, :] = jnp.zeros((1, N_SH), F32)
        row = lax.broadcasted_iota(jnp.int32, (tm, 1), 0)
        prev = jnp.where(row % seq_len == 0, prev0_ref[...], cur_scr[7:7 + tm, :])
    else:
        @pl.when(i % tiles_per_seq == 0)
        def _():
            cur_scr[7:8, :] = prev0_ref[...]
        cur_scr[8:8 + tm, :] = cur
        prev = cur_scr[7:7 + tm, :]
        cur_scr[7:8, :] = cur[tm - 1:tm, :]
    sh = cur + (prev - cur) * mu_ref[...]
    r = sh[:, 0:W_B]
    k = sh[:, W_B:2 * W_B]
    v = sh[:, 2 * W_B:3 * W_B]
    xw = sh[:, 3 * W_B:3 * W_B + LANES]
    xa = sh[:, 3 * W_B + LANES:3 * W_B + 2 * LANES]
    z = wd0_ref[...] + jnp.dot(jnp.tanh(xw).astype(BF16), wdu_ref[...], preferred_element_type=F32)
    nz = -z
    softplus = jnp.maximum(nz, 0.0) + jnp.log1p(jnp.exp(-jnp.abs(nz)))
    w_log = -softplus - 0.5
    lw_ref[...] = -jnp.exp(w_log)
    a = jax.nn.sigmoid(a0_ref[...] + jnp.dot(xa.astype(BF16), wiu_ref[...], preferred_element_type=F32))
    kk = k * kk_ref_[...]
    lo_half = lane < N_B
    for c in range(W_B // LANES):
        blk = kk[:, c * LANES:(c + 1) * LANES]
        sq = blk * blk
        s0 = jnp.sum(jnp.where(lo_half, sq, 0.0), axis=-1, keepdims=True)
        s1 = jnp.sum(jnp.where(lo_half, 0.0, sq), axis=-1, keepdims=True)
        den = jnp.where(lo_half, jnp.maximum(jnp.sqrt(s0), 1e-12), jnp.maximum(jnp.sqrt(s1), 1e-12))
        kkn = blk / den
        kkn_ref[:, c * LANES:(c + 1) * LANES] = kkn
        kka_ref[:, c * LANES:(c + 1) * LANES] = kkn * a[:, c * LANES:(c + 1) * LANES]
    r_ref[...] = r
    vb_ref[...] = v
    km_ref[...] = k * (1.0 + (a - 1.0) * ka_ref[...])


def _front_call(x3d, prev0, tabs, wts, *, sample):
    nseq_, seq_len, d = x3d.shape
    x2d = x3d.reshape(nseq_ * seq_len, d)
    xlast = x3d[:, seq_len - 1, :]
    n = x2d.shape[0]
    if sample:
        tm = n
        tiles_per_seq = 1
        nseq = n // seq_len
    else:
        tm = 256
        assert seq_len % tm == 0
        tiles_per_seq = seq_len // tm
        nseq = n // seq_len
    grid = (n // tm,)
    ttab = tabs.shape[1]
    tab_blocks = ttab // tm
    row = lambda i: (i, 0)
    const2 = lambda i: (0, 0)
    prev_spec = pl.BlockSpec((tm, N_SH), row) if sample else pl.BlockSpec((1, N_SH), const2)
    xl_shape = jax.ShapeDtypeStruct((nseq, d), F32)
    xl_spec = pl.BlockSpec((nseq, d), const2)
    scratch = [pltpu.VMEM((tm + 8, N_SH), F32)]
    in_specs = [pl.BlockSpec((tm, d), row), pl.BlockSpec((nseq, d), const2), prev_spec,
                pl.BlockSpec((3, tm, LANES), lambda i: (0, i % tab_blocks, 0))]
    in_specs += [pl.BlockSpec(w.shape, const2) for w in wts]
    out_shapes = [
        jax.ShapeDtypeStruct((n, HCAT), BF16), jax.ShapeDtypeStruct((n, HCAT), BF16),
        jax.ShapeDtypeStruct((n, W_A), BF16), jax.ShapeDtypeStruct((n, KV_LORA), F32),
        jax.ShapeDtypeStruct((n, ROPE), F32), jax.ShapeDtypeStruct((n, W_A), F32),
        jax.ShapeDtypeStruct((n, W_B), F32),
    ] + [jax.ShapeDtypeStruct((n, W_B), F32)] * 6 + [xl_shape]
    out_specs = [pl.BlockSpec((tm, s.shape[1]), row) for s in out_shapes[:-1]] + [xl_spec]
    body = functools.partial(_front_body, tm=tm, tiles_per_seq=tiles_per_seq, seq_len=seq_len, sample=sample)
    return pl.pallas_call(
        body, grid=grid, in_specs=in_specs, out_specs=out_specs, out_shape=out_shapes,
        scratch_shapes=scratch, compiler_params=_cparams(("arbitrary",)),
        name="front_sample" if sample else "front_prompt",
    )(x2d, xlast, prev0, tabs, *wts)


def _matmul_body(a_ref, b_ref, o_ref):
    o_ref[...] = jnp.dot(a_ref[...].astype(BF16), b_ref[...], preferred_element_type=F32)


def _matmul_call(a, b16):
    m, k = a.shape
    n = b16.shape[1]
    return pl.pallas_call(
        _matmul_body, grid=(1,),
        in_specs=[pl.BlockSpec((m, k), lambda i: (0, 0)), pl.BlockSpec((k, n), lambda i: (0, 0))],
        out_specs=pl.BlockSpec((m, n), lambda i: (0, 0)),
        out_shape=jax.ShapeDtypeStruct((m, n), F32),
        compiler_params=_cparams(("arbitrary",)), name="prev_proj",
    )(a, b16)


def _p_attn_body(q_ref, k_ref, v_ref, o_ref, *, tq):
    qi = pl.program_id(2)
    lane = _lane_iota()
    outs = []
    for hh in range(2):
        q = q_ref[:, hh * LANES:(hh + 1) * LANES]

        def scores(j):
            kb = k_ref[pl.ds(pl.multiple_of(j * tq, tq), tq), hh * LANES:(hh + 1) * LANES]
            return lax.dot_general(q, kb, (((1,), (1,)), ((), ())), preferred_element_type=F32)

        def update(carry, s, j):
            m, l, acc = carry
            m_new = jnp.maximum(m, jnp.max(s, axis=-1, keepdims=True))
            alpha = jnp.exp(m - m_new)
            e = jnp.exp(s - m_new)
            vb = v_ref[pl.ds(pl.multiple_of(j * tq, tq), tq), :]
            acc = acc * alpha + jnp.dot(e.astype(BF16), vb, preferred_element_type=F32)
            return m_new, l * alpha + jnp.sum(e, axis=-1, keepdims=True), acc

        def step(j, carry):
            return update(carry, scores(j), j)

        init = (jnp.full((tq, 1), NEG, F32), jnp.zeros((tq, 1), F32), jnp.zeros((tq, LANES), F32))
        carry = lax.fori_loop(0, qi, step, init)
        s = scores(qi)
        rows = lax.broadcasted_iota(jnp.int32, (tq, tq), 0)
        cols = lax.broadcasted_iota(jnp.int32, (tq, tq), 1)
        s = jnp.where(cols <= rows, s, NEG)
        m, l, acc = update(carry, s, qi)
        outs.append(acc / l)
    o_ref[...] = jnp.where(lane < V_DIM, outs[0], outs[1])


def _p_attn_call(qcat, kcat, v16, *, nseq, seq_len):
    n = qcat.shape[0]
    tq = 256
    nq = seq_len // tq
    grid = (nseq, H_A // 2, nq)
    return pl.pallas_call(
        functools.partial(_p_attn_body, tq=tq), grid=grid,
        in_specs=[pl.BlockSpec((tq, 2 * LANES), lambda b, h, i: (b * nq + i, h)),
                  pl.BlockSpec((seq_len, 2 * LANES), lambda b, h, i: (b, h)),
                  pl.BlockSpec((seq_len, LANES), lambda b, h, i: (b, h))],
        out_specs=pl.BlockSpec((tq, LANES), lambda b, h, i: (b * nq + i, h)),
        out_shape=jax.ShapeDtypeStruct((n, W_A), F32),
        compiler_params=_cparams(("arbitrary", "arbitrary", "arbitrary")), name="p_attn",
    )(qcat, kcat, v16)


def _split2(x):
    hi = x.astype(BF16)
    lo = (x - hi.astype(F32)).astype(BF16)
    return hi, lo


def _split3(x):
    hi = x.astype(BF16)
    r1 = x - hi.astype(F32)
    mid = r1.astype(BF16)
    lo = (r1 - mid.astype(F32)).astype(BF16)
    return hi, mid, lo


def _s_attn_body(pt_ref, qn_ref, qp_ref, cnew_ref, pnew_ref, wkt_ref, wuv_ref, *refs, pg, page):
    c_refs = refs[:pg]
    p_refs = refs[pg:2 * pg]
    o_ref = refs[2 * pg]
    c_scr, p_scr, m_scr, l_scr, acc_scr = refs[2 * pg + 1:]
    j = pl.program_id(1)
    nj = pl.num_programs(1)
    nrow = qn_ref.shape[0]

    @pl.when(j == 0)
    def _():
        m_scr[...] = jnp.full(m_scr.shape, NEG, F32)
        l_scr[...] = jnp.zeros(l_scr.shape, F32)
        acc_scr[...] = jnp.zeros(acc_scr.shape, F32)

    def attend(c16, p16, mask):
        nk = c16.shape[0]
        krt = lax.dot_general(wkt_ref[...], c16, (((1,), (1,)), ((), ())), preferred_element_type=F32)
        ssq = jnp.concatenate(
            [jnp.sum(jnp.square(krt[h * NOPE:(h + 1) * NOPE, :]), axis=0, keepdims=True) for h in range(H_A)], axis=0)
        rinv = lax.rsqrt(ssq * (1.0 / NOPE) + EPS)
        sraw = jnp.dot(qn_ref[...], krt.astype(BF16), preferred_element_type=F32)
        spe = lax.dot_general(qp_ref[...], p16, (((1,), (1,)), ((), ())), preferred_element_type=F32)
        s = sraw * jnp.concatenate([rinv] * (nrow // H_A), axis=0) + spe
        if mask is not None:
            s = jnp.where(mask, s, NEG)
        m_old = m_scr[...]
        m_new = jnp.maximum(m_old, jnp.max(s, axis=-1, keepdims=True))
        alpha = jnp.exp(m_old - m_new)
        e = jnp.exp(s - m_new)
        l_scr[...] = l_scr[...] * alpha + jnp.sum(e, axis=-1, keepdims=True)
        acc_scr[...] = acc_scr[...] * alpha + jnp.dot(e.astype(BF16), c16, preferred_element_type=F32)
        m_scr[...] = m_new

    for t in range(pg):
        c_scr[t * page:(t + 1) * page, :] = c_refs[t][...].astype(BF16)
        p_scr[t * page:(t + 1) * page, :] = p_refs[t][...].astype(BF16)
    attend(c_scr[...], p_scr[...], None)

    @pl.when(j == nj - 1)
    def _():
        nnew = cnew_ref.shape[0]
        rows = lax.broadcasted_iota(jnp.int32, (nrow, nnew), 0) // H_A
        cols = lax.broadcasted_iota(jnp.int32, (nrow, nnew), 1)
        attend(cnew_ref[...].astype(BF16), pnew_ref[...].astype(BF16), cols <= rows)
        o_lat = acc_scr[...] / l_scr[...]
        hi, lo = _split2(o_lat)
        o_ref[...] = (jnp.dot(hi, wuv_ref[...], preferred_element_type=F32)
                      + jnp.dot(lo, wuv_ref[...], preferred_element_type=F32))


def _s_attn_call(page_table, qn_blk, qp_blk, cnew, pnew, wkt16, wuv16, cache_ckv, cache_kpe, layer):
    bd, nrow, _ = qn_blk.shape
    n_pages = page_table.shape[1]
    page = cache_ckv.shape[2]
    pg = 8
    assert n_pages % pg == 0
    nnew = cnew.shape[1]
    grid = (bd, n_pages // pg)
    seq3 = lambda b, j, pt: (b, 0, 0)
    const2 = lambda b, j, pt: (0, 0)

    def page_map(t):
        return lambda b, j, pt: (layer, pt[b, j * pg + t], 0, 0)

    in_specs = [pl.BlockSpec((None, nrow, W_A), seq3), pl.BlockSpec((None, nrow, ROPE), seq3),
                pl.BlockSpec((None, nnew, KV_LORA), seq3), pl.BlockSpec((None, nnew, ROPE), seq3),
                pl.BlockSpec(wkt16.shape, const2), pl.BlockSpec(wuv16.shape, const2)]
    in_specs += [pl.BlockSpec((None, None, page, KV_LORA), page_map(t)) for t in range(pg)]
    in_specs += [pl.BlockSpec((None, None, page, ROPE), page_map(t)) for t in range(pg)]
    gs = pltpu.PrefetchScalarGridSpec(
        num_scalar_prefetch=1, grid=grid, in_specs=in_specs,
        out_specs=pl.BlockSpec((None, nrow, W_A), seq3),
        scratch_shapes=[pltpu.VMEM((pg * page, KV_LORA), BF16), pltpu.VMEM((pg * page, ROPE), BF16),
                        pltpu.VMEM((nrow, 1), F32), pltpu.VMEM((nrow, 1), F32), pltpu.VMEM((nrow, KV_LORA), F32)])
    return pl.pallas_call(
        functools.partial(_s_attn_body, pg=pg, page=page), grid_spec=gs,
        out_shape=jax.ShapeDtypeStruct((bd, nrow, W_A), F32),
        compiler_params=_cparams(("arbitrary", "arbitrary")), name="s_attn",
    )(page_table, qn_blk, qp_blk, cnew, pnew, wkt16, wuv16,
      *([cache_ckv] * pg), *([cache_kpe] * pg))


def _rwkv_body(r_ref, lw_ref, km_ref, v_ref, kk_ref, kka_ref, gb_ref, s0_ref, rk_ref, lnw_ref, lnb_ref,
               o_ref, sout_ref, s_scr, *, c):
    ci = pl.program_id(1)
    nc = pl.num_programs(1)
    g4 = RW_GROUP
    ngroups = W_B // g4

    rr = lax.broadcasted_iota(jnp.int32, (g4, g4), 0) // N_B
    cc = lax.broadcasted_iota(jnp.int32, (g4, g4), 1) // N_B
    blockmask = rr == cc
    t_idx = lax.broadcasted_iota(jnp.int32, (c, g4), 0)
    i_idx = lax.broadcasted_iota(jnp.int32, (c, g4), 1) % N_B
    low_strict = i_idx < t_idx
    low_incl = i_idx <= t_idx
    eye = (i_idx == t_idx).astype(F32)
    def mmh(x, y):
        return jnp.dot(x.astype(BF16), bd_rows(y), preferred_element_type=F32)

    def bd_rows(y):
        y16 = y.astype(BF16)
        if c == N_B:
            t = jnp.concatenate([y16] * 4, axis=0)
        else:
            pad = jnp.zeros((N_B - c, g4), BF16)
            t = jnp.concatenate([y16, pad] * 4, axis=0)
        return jnp.where(blockmask, t, jnp.zeros_like(t))

    def nt(x, ybd16):
        return lax.dot_general(x.astype(BF16), ybd16, (((1,), (1,)), ((), ())), preferred_element_type=F32)

    @pl.when(ci == 0)
    def _():
        s_scr[...] = jnp.zeros(s_scr.shape, F32)
        for h in range(H_B):
            g, jh = divmod(h, 4)
            s_scr[g, jh * N_B:(jh + 1) * N_B, jh * N_B:(jh + 1) * N_B] = s0_ref[h]

    tri = (lax.broadcasted_iota(jnp.int32, (c, c), 1) <= lax.broadcasted_iota(jnp.int32, (c, c), 0)).astype(BF16)
    lw_all = lw_ref[...]
    g_all = sum(jnp.dot(tri, part, preferred_element_type=F32) for part in _split3(lw_all))

    ones_blk = blockmask.astype(BF16)

    for g in range(ngroups):
        sl = slice(g * g4, (g + 1) * g4)
        lw = lw_all[:, sl]
        gc = g_all[:, sl]
        big = jnp.exp(gc)
        at = -kk_ref[:, sl] * jnp.exp(gc - lw)
        ginv = jnp.exp(-gc)
        bt = kka_ref[:, sl] * ginv
        kt = km_ref[:, sl] * ginv
        r = r_ref[:, sl]
        rt = r * big
        v = v_ref[:, sl]
        s_old = s_scr[g]
        s16 = s_old.astype(BF16)

        bt_bd = bd_rows(bt)
        kt_bd = bd_rows(kt)
        a_ab = jnp.where(low_strict, nt(at, bt_bd), 0.0)
        a_ak = jnp.where(low_strict, nt(at, kt_bd), 0.0)
        a_rk = jnp.where(low_incl, nt(rt, kt_bd), 0.0)
        a_rb = jnp.where(low_incl, nt(rt, bt_bd), 0.0)

        if c <= 16:
            tinv = eye + a_ab
            pw = a_ab
            n = 1
            while 2 * n < c:
                pw = mmh(pw, pw)
                tinv = mmh(tinv, eye + pw)
                n *= 2
        else:
            same16 = (i_idx // 16) == (t_idx // 16)
            dg = jnp.where(same16, a_ab, 0.0)
            lo = a_ab - dg
            td = eye + dg
            pw = dg
            for _ in range(3):
                pw = mmh(pw, pw)
                td = mmh(td, eye + pw)
            nn = mmh(td, lo)
            n2 = mmh(nn, nn)
            tinv = mmh(mmh(eye + nn, eye + n2), td)

        b = mmh(a_ak, v) + nt(at, s16)
        u = mmh(tinv, b)
        y = mmh(a_rk, v) + mmh(a_rb, u) + nt(rt, s16)

        g_last = big[c - 1:c, :]
        upd = (lax.dot_general(v.astype(BF16), (kt * g_last).astype(BF16), (((0,), (0,)), ((), ())),
                               preferred_element_type=F32)
               + lax.dot_general(u.astype(BF16), (bt * g_last).astype(BF16), (((0,), (0,)), ((), ())),
                                 preferred_element_type=F32))
        s_scr[g] = s_old * g_last + jnp.where(blockmask, upd, 0.0)

        def head_sum(x):
            return sum(jnp.dot(part, ones_blk, preferred_element_type=F32) for part in _split3(x))

        mean = head_sum(y) * (1.0 / N_B)
        d = y - mean
        var = head_sum(d * d) * (1.0 / N_B)
        yn = d * lax.rsqrt(var + LNX_EPS) * lnw_ref[:, sl] + lnb_ref[:, sl]
        bonus = head_sum(r * km_ref[:, sl] * rk_ref[:, sl]) * v
        gb = gb_ref[:, sl]
        o_ref[:, sl] = (yn + bonus) * (gb * jax.nn.sigmoid(gb))

    @pl.when(ci == nc - 1)
    def _():
        for h in range(H_B):
            g, jh = divmod(h, 4)
            sout_ref[h] = s_scr[g, jh * N_B:(jh + 1) * N_B, jh * N_B:(jh + 1) * N_B]


def _rwkv_call(r, lw, km, v, kk, kka, gb, s0, rk, lnw, lnb, *, nseq, c):
    n = r.shape[0]
    nc = n // nseq // c
    tok = pl.BlockSpec((c, W_B), lambda b, i: (b * nc + i, 0))
    vec = pl.BlockSpec((1, W_B), lambda b, i: (0, 0))
    st = pl.BlockSpec((None, H_B, N_B, N_B), lambda b, i: (b, 0, 0, 0))
    return pl.pallas_call(
        functools.partial(_rwkv_body, c=c), grid=(nseq, nc),
        in_specs=[tok] * 7 + [st, vec, vec, vec],
        out_specs=[tok, st],
        out_shape=[jax.ShapeDtypeStruct((n, W_B), F32), jax.ShapeDtypeStruct((nseq, H_B, N_B, N_B), F32)],
        scratch_shapes=[pltpu.VMEM((W_B // RW_GROUP, RW_GROUP, RW_GROUP), F32)],
        compiler_params=_cparams(("arbitrary", "arbitrary")), name=f"rwkv_c{c}",
    )(r, lw, km, v, kk, kka, gb, s0, rk, lnw, lnb)


def _back_body(x_ref, oa_ref, ga_ref, ob_ref, p_ref, woa_ref, wob_ref, wple_ref, pn_ref, wg_ref, y_ref):
    ga = ga_ref[...]
    mixed_a = oa_ref[...] * (ga * jax.nn.sigmoid(ga))
    x1 = (x_ref[...] + jnp.dot(mixed_a.astype(BF16), woa_ref[...], preferred_element_type=F32)
          + jnp.dot(ob_ref[...].astype(BF16), wob_ref[...], preferred_element_type=F32))
    xg = _rms(x1, pn_ref[...], x1.shape[-1])
    gate = jax.nn.sigmoid(jnp.dot(xg.astype(BF16), wg_ref[...], preferred_element_type=F32))
    y_ref[...] = x1 + jnp.dot(p_ref[...].astype(BF16), wple_ref[...], preferred_element_type=F32) * gate


def _back_call(x2d, oa, ga, ob, p2d, wts):
    n, d = x2d.shape
    tm = min(512, n)
    row = lambda i: (i, 0)
    const2 = lambda i: (0, 0)
    ins = [x2d, oa, ga, ob, p2d]
    return pl.pallas_call(
        _back_body, grid=(n // tm,),
        in_specs=[pl.BlockSpec((tm, a.shape[1]), row) for a in ins] + [pl.BlockSpec(w.shape, const2) for w in wts],
        out_specs=pl.BlockSpec((tm, d), row),
        out_shape=jax.ShapeDtypeStruct((n, d), F32),
        compiler_params=_cparams(("arbitrary",)), name="back",
    )(*ins, *wts)


def _pad_cols(a, width):
    return jnp.pad(a, ((0, 0), (0, width - a.shape[1])))


def _row(vec):
    return vec.reshape(1, -1).astype(F32)


def _rope_tabs(pos):
    inv = ROPE_BASE ** (-jnp.arange(0, ROPE, 2, dtype=F32) / ROPE)
    ang = pos.astype(F32)[:, None] * inv[None, :]
    cos, sin = jnp.cos(ang), jnp.sin(ang)
    t = pos.shape[0]
    z16 = jnp.zeros((t, ROPE // 2), F32)
    c = jnp.concatenate([cos, cos, jnp.ones((t, NOPE), F32), jnp.zeros((t, LANES - ROPE - NOPE), F32)], axis=1)
    s1 = _pad_cols(jnp.concatenate([-sin, z16], axis=1), LANES)
    s2 = _pad_cols(jnp.concatenate([z16, sin], axis=1), LANES)
    return jnp.stack([c, s1, s2])


def _layer_weights(i, norm_in, w_in, mu_shift, q_a_norm, w_uq, kv_a_norm, w_ukv, g_q_nope, g_q_pe, g_k_nope,
                   g_k_pe, w_decay0, w_decay_up, a0, w_iclr_up, k_k, k_a):
    w = w_in[i]
    d = w.shape[0]
    c_q, c_kv, c_kpe = 0, Q_LORA, Q_LORA + KV_LORA
    c_ga = c_kpe + ROPE
    c_gb = c_ga + W_A
    c_sh = c_gb + W_B
    c_xw = c_sh + 3 * W_B
    c_xa = c_xw + DECAY_LORA
    w_perm = jnp.concatenate([
        w[:, c_q:c_kpe], _pad_cols(w[:, c_kpe:c_ga], LANES), w[:, c_ga:c_sh], w[:, c_sh:c_xw],
        _pad_cols(w[:, c_xw:c_xa], LANES), _pad_cols(w[:, c_xa:], LANES)], axis=1).astype(BF16)
    mu = mu_shift[i][None, :]
    mu_perm = jnp.concatenate([mu[:, :3 * W_B], _pad_cols(mu[:, 3 * W_B:3 * W_B + DECAY_LORA], LANES),
                               _pad_cols(mu[:, 3 * W_B + DECAY_LORA:], LANES)], axis=1)
    uq = w_uq[i]
    uq_cat = jnp.concatenate([uq[..., NOPE:], uq[..., :NOPE],
                              jnp.zeros((Q_LORA, H_A, LANES - NOPE - ROPE), F32)], axis=-1)
    uq_cat = uq_cat.reshape(Q_LORA, HCAT).astype(BF16)
    ukv = w_ukv[i]
    uk = ukv[..., :NOPE]
    uk_cat = jnp.concatenate([jnp.zeros((KV_LORA, H_A, ROPE), F32), uk,
                              jnp.zeros((KV_LORA, H_A, LANES - NOPE - ROPE), F32)], axis=-1)
    uk_cat = uk_cat.reshape(KV_LORA, HCAT).astype(BF16)
    uv = ukv[..., NOPE:].reshape(KV_LORA, W_A).astype(BF16)
    ukt = jnp.transpose(uk.reshape(KV_LORA, H_A * NOPE)).astype(BF16)
    zpad = jnp.zeros((LANES - NOPE - ROPE,), F32)
    gq_prompt = jnp.concatenate([g_q_pe[i], g_q_nope[i], zpad]) * SCALE
    gq_sample = jnp.concatenate([g_q_pe[i], g_q_nope[i] * g_k_nope[i], zpad]) * SCALE
    gk = jnp.concatenate([jnp.zeros((ROPE,), F32), g_k_nope[i], zpad])
    gkp = jnp.concatenate([g_k_pe[i], jnp.zeros((LANES - ROPE,), F32)])
    wdu = jnp.pad(w_decay_up[i], ((0, LANES - DECAY_LORA), (0, 0))).astype(BF16)
    wiu = jnp.pad(w_iclr_up[i], ((0, LANES - ICLR_LORA), (0, 0))).astype(BF16)

    def front_wts(gq):
        return [_row(norm_in[i]), w_perm, mu_perm, _row(q_a_norm[i]), uq_cat, _row(kv_a_norm[i]), uk_cat, uv,
                _row(gq), _row(gk), _row(gkp), _row(w_decay0[i]), wdu, _row(a0[i]), wiu, _row(k_k[i]), _row(k_a[i])]

    return front_wts(gq_prompt), front_wts(gq_sample), w_perm[:, P_SH:], ukt, uv


def kernel(x_prompt, x_sample, p_prompt, p_sample, cache_ckv, cache_kpe, state_wkv, state_shift, page_table,
           norm_in, w_in, mu_shift, q_a_norm, w_uq, kv_a_norm, w_ukv, g_q_nope, g_q_pe, g_k_nope, g_k_pe,
           w_decay0, w_decay_up, a0, w_iclr_up, k_k, k_a, r_k, lnx_w, lnx_b, w_out, w_ple, ple_norm, w_ple_gate):
    b, t, d = x_prompt.shape
    bd, tq, _ = x_sample.shape
    depth = w_in.shape[0]
    page = cache_ckv.shape[2]
    past_len = page_table.shape[1] * page
    tabs_p = _rope_tabs(jnp.arange(t))
    tabs_s = jnp.tile(_rope_tabs(past_len + jnp.arange(tq)), (1, bd, 1))
    y_p = x_prompt.reshape(b * t, d)
    y_s = x_sample.reshape(bd * tq, d)
    outs = [[] for _ in range(8)]
    for i in range(depth):
        wts_p, wts_s, w_shift16, ukt16, uv16 = _layer_weights(
            i, norm_in, w_in, mu_shift, q_a_norm, w_uq, kv_a_norm, w_ukv, g_q_nope, g_q_pe, g_k_nope, g_k_pe,
            w_decay0, w_decay_up, a0, w_iclr_up, k_k, k_a)
        rw_vecs = [_row(r_k[i]), _row(lnx_w[i]), _row(lnx_b[i])]
        wo = w_out[i].astype(BF16)
        back_wts = [wo[:W_A], wo[W_A:], w_ple[i].astype(BF16), _row(ple_norm[i]), w_ple_gate[i].astype(BF16)]

        (qcat, kcat, v16, ckv, kpe, ga, gb, r, lw, km, vb, kkn, kka, xl) = _front_call(
            y_p.reshape(b, t, d), jnp.zeros((1, N_SH), F32), tabs_p, wts_p, sample=False)
        o_a = _p_attn_call(qcat, kcat, v16, nseq=b, seq_len=t)
        o_b, s_p = _rwkv_call(r, lw, km, vb, kkn, kka, gb, jnp.zeros((b, H_B, N_B, N_B), F32), *rw_vecs,
                              nseq=b, c=RW_CHUNK)
        y_p = _back_call(y_p, o_a, ga, o_b, p_prompt[i].reshape(b * t, -1), back_wts)
        outs[0].append(ckv.reshape(b, t, KV_LORA)); outs[2].append(kpe.reshape(b, t, ROPE))
        outs[4].append(s_p); outs[6].append(xl)

        prev_proj = _matmul_call(state_shift[i], w_shift16)
        prev0 = jnp.pad(prev_proj[:, None, :], ((0, 0), (0, tq - 1), (0, 0))).reshape(bd * tq, N_SH)
        (qcat, _, _, ckv, kpe, ga, gb, r, lw, km, vb, kkn, kka, xl) = _front_call(
            y_s.reshape(bd, tq, d), prev0, tabs_s, wts_s, sample=True)
        q4 = qcat.reshape(bd, tq, H_A, LANES)
        qp_blk = q4[..., :ROPE].reshape(bd, tq * H_A, ROPE)
        eye_h = jnp.eye(H_A, dtype=BF16)
        qn_blk = (q4[..., ROPE:ROPE + NOPE][:, :, :, None, :] * eye_h[None, None, :, :, None]
                  ).reshape(bd, tq * H_A, W_A)
        npad = LANES
        cnew = jnp.pad(ckv.reshape(bd, tq, KV_LORA), ((0, 0), (0, npad - tq), (0, 0)))
        pnew = jnp.pad(kpe.reshape(bd, tq, ROPE), ((0, 0), (0, npad - tq), (0, 0)))
        o_full = _s_attn_call(page_table, qn_blk, qp_blk, cnew, pnew, ukt16, uv16, cache_ckv, cache_kpe, i)
        o5 = o_full.reshape(bd, tq, H_A, H_A, V_DIM)
        o_a = jnp.einsum('bqhhd->bqhd', o5).reshape(bd * tq, W_A)
        cpad = 16

        def pad_tok(a):
            return jnp.pad(a.reshape(bd, tq, -1), ((0, 0), (0, cpad - tq), (0, 0))).reshape(bd * cpad, -1)

        o_b8, s_s = _rwkv_call(pad_tok(r), pad_tok(lw), pad_tok(km), pad_tok(vb), pad_tok(kkn), pad_tok(kka),
                               pad_tok(gb), state_wkv[i], *rw_vecs, nseq=bd, c=cpad)
        o_b = o_b8.reshape(bd, cpad, W_B)[:, :tq].reshape(bd * tq, W_B)
        y_s = _back_call(y_s, o_a, ga, o_b, p_sample[i].reshape(bd * tq, -1), back_wts)
        outs[1].append(ckv.reshape(bd, tq, KV_LORA)); outs[3].append(kpe.reshape(bd, tq, ROPE))
        outs[5].append(s_s); outs[7].append(xl)
    st = [jnp.stack(o) for o in outs]
    return (y_p.reshape(b, t, d), y_s.reshape(bd, tq, d), st[0], st[1], st[2], st[3], st[4], st[5], st[6], st[7])
```

```python
import functools

import jax
import jax.numpy as jnp
from jax import lax
from jax.experimental import pallas as pl
from jax.experimental.pallas import tpu as pltpu

F32 = jnp.float32
BF16 = jnp.bfloat16

LANES = 128
H_A = 8
NOPE = 64
ROPE = 32
V_DIM = 64
Q_LORA = 384
KV_LORA = 256
ROPE_BASE = 10000.0
SCALE = (NOPE + ROPE) ** -0.5
H_B = 8
N_B = 64
W_A = H_A * V_DIM
W_B = H_B * N_B
DECAY_LORA = 64
ICLR_LORA = 64
LNX_EPS = 64e-5
EPS = 1e-6
NEG = -1e30

P_Q = 0
P_KV = P_Q + Q_LORA
P_KPE = P_KV + KV_LORA
P_GA = P_KPE + LANES
P_GB = P_GA + W_A
P_SH = P_GB + W_B
N_SH = 3 * W_B + 2 * LANES
P_TOT = P_SH + N_SH

HCAT = H_A * LANES
RW_CHUNK = 64
RW_GROUP = 4 * N_B
RW_NGROUPS = W_B // RW_GROUP
FRONT_TM = 256
BACK_TM = 512
ATTN_T = 512
PAGES_PER_STEP = 16
PAGES_PER_SUB = 4
VMEM_LIMIT = 56 * 1024 * 1024


def _cparams(sem):
    return pltpu.CompilerParams(dimension_semantics=sem, vmem_limit_bytes=VMEM_LIMIT)


def _lane_iota(n=LANES):
    return lax.broadcasted_iota(jnp.int32, (1, n), 1)


def _rms(x, g, n):
    ms = jnp.sum(x * x, axis=-1, keepdims=True) * (1.0 / n)
    return x * lax.rsqrt(ms + EPS) * g


def _rope128(n, c, s1, s2):
    return n * c + pltpu.roll(n, LANES - ROPE // 2, axis=1) * s1 + pltpu.roll(n, ROPE // 2, axis=1) * s2


def _split2(x):
    hi = x.astype(BF16)
    lo = (x - hi.astype(F32)).astype(BF16)
    return hi, lo


def _split3(x):
    hi = x.astype(BF16)
    r1 = x - hi.astype(F32)
    mid = r1.astype(BF16)
    lo = (r1 - mid.astype(F32)).astype(BF16)
    return hi, mid, lo


def _nt(x16, y16):
    return lax.dot_general(x16, y16, (((1,), (1,)), ((), ())), preferred_element_type=F32)


def _front_body(x_ref, xlast_ref, prev0_ref, tab_ref, norm_in_ref, w_in_ref, mu_ref, qan_ref, wuq_ref, kvan_ref,
                wuk_ref, wuv_ref, gq_ref, gk_ref, gkp_ref, wd0_ref, wdu_ref, a0_ref, wiu_ref, kk_ref_, ka_ref,
                qcat_ref, kcat_ref, v_ref, ckv_ref, kpe_ref, ga_ref, gb_ref,
                r_ref, lw_ref, km_ref, vb_ref, kkn_ref, kka_ref, xl_ref,
                cur_scr, *, tm, tiles_per_seq, seq_len, sample):
    i = pl.program_id(0)
    x = x_ref[...]
    xn = _rms(x, norm_in_ref[...], x.shape[-1])
    xl_ref[...] = _rms(xlast_ref[...], norm_in_ref[...], x.shape[-1])
    p = jnp.dot(xn.astype(BF16), w_in_ref[...], preferred_element_type=F32)

    c_tab, s1_tab, s2_tab = tab_ref[0], tab_ref[1], tab_ref[2]
    lane = _lane_iota()
    is_rope = lane < ROPE

    c_q = _rms(p[:, P_Q:P_KV], qan_ref[...], Q_LORA)
    q = jnp.dot(c_q.astype(BF16), wuq_ref[...], preferred_element_type=F32)
    gq = gq_ref[...]
    for h in range(H_A):
        qb = q[:, h * LANES:(h + 1) * LANES]
        sq = qb * qb
        s_r = jnp.sum(jnp.where(is_rope, sq, 0.0), axis=-1, keepdims=True)
        s_n = jnp.sum(jnp.where(is_rope, 0.0, sq), axis=-1, keepdims=True)
        inv = jnp.where(is_rope, lax.rsqrt(s_r * (1.0 / ROPE) + EPS), lax.rsqrt(s_n * (1.0 / NOPE) + EPS))
        qn = qb * inv * gq
        qcat_ref[:, h * LANES:(h + 1) * LANES] = _rope128(qn, c_tab, s1_tab, s2_tab).astype(BF16)

    c_kv = _rms(p[:, P_KV:P_KPE], kvan_ref[...], KV_LORA)
    ckv_ref[...] = c_kv
    c_kv16 = c_kv.astype(BF16)
    v_ref[...] = jnp.dot(c_kv16, wuv_ref[...], preferred_element_type=F32).astype(BF16)
    kraw = jnp.dot(c_kv16, wuk_ref[...], preferred_element_type=F32)
    kp = p[:, P_KPE:P_GA]
    kp_n = kp * lax.rsqrt(jnp.sum(kp * kp, axis=-1, keepdims=True) * (1.0 / ROPE) + EPS) * gkp_ref[...]
    kp_r = _rope128(kp_n, c_tab, s1_tab, s2_tab)
    kpe_ref[...] = kp_r[:, :ROPE]
    gk = gk_ref[...]
    for h in range(H_A):
        kb = kraw[:, h * LANES:(h + 1) * LANES]
        s_n = jnp.sum(kb * kb, axis=-1, keepdims=True)
        kn = kb * lax.rsqrt(s_n * (1.0 / NOPE) + EPS) * gk
        kcat_ref[:, h * LANES:(h + 1) * LANES] = (kn + kp_r).astype(BF16)

    ga_ref[...] = p[:, P_GA:P_GB]
    gb_ref[...] = p[:, P_GB:P_SH]

    cur = p[:, P_SH:P_TOT]
    if sample:
        cur_scr[8:8 + tm, :] = cur
        cur_scr[7:8, :] = jnp.zeros((1, N_SH), F32)
        row = lax.broadcasted_iota(jnp.int32, (tm, 1), 0)
        prev = jnp.where(row % seq_len == 0, prev0_ref[...], cur_scr[7:7 + tm, :])
    else:
        @pl.when(i % tiles_per_seq == 0)
        def _():
            cur_scr[7:8, :] = prev0_ref[...]
        cur_scr[8:8 + tm, :] = cur
        prev = cur_scr[7:7 + tm, :]
        cur_scr[7:8, :] = cur[tm - 1:tm, :]
    sh = cur + (prev - cur) * mu_ref[...]
    r = sh[:, 0:W_B]
    k = sh[:, W_B:2 * W_B]
    v = sh[:, 2 * W_B:3 * W_B]
    xw = sh[:, 3 * W_B:3 * W_B + LANES]
    xa = sh[:, 3 * W_B + LANES:3 * W_B + 2 * LANES]
    z = wd0_ref[...] + jnp.dot(jnp.tanh(xw).astype(BF16), wdu_ref[...], preferred_element_type=F32)
    nz = -z
    softplus = jnp.maximum(nz, 0.0) + jnp.log1p(jnp.exp(-jnp.abs(nz)))
    w_log = -softplus - 0.5
    lw_ref[...] = -jnp.exp(w_log)
    a = jax.nn.sigmoid(a0_ref[...] + jnp.dot(xa.astype(BF16), wiu_ref[...], preferred_element_type=F32))
    kk = k * kk_ref_[...]
    lo_half = lane < N_B
    for c in range(W_B // LANES):
        blk = kk[:, c * LANES:(c + 1) * LANES]
        sq = blk * blk
        s0 = jnp.sum(jnp.where(lo_half, sq, 0.0), axis=-1, keepdims=True)
        s1 = jnp.sum(jnp.where(lo_half, 0.0, sq), axis=-1, keepdims=True)
        den = jnp.where(lo_half, jnp.maximum(jnp.sqrt(s0), 1e-12), jnp.maximum(jnp.sqrt(s1), 1e-12))
        kkn = blk / den
        kkn_ref[:, c * LANES:(c + 1) * LANES] = kkn
        kka_ref[:, c * LANES:(c + 1) * LANES] = kkn * a[:, c * LANES:(c + 1) * LANES]
    r_ref[...] = r
    vb_ref[...] = v
    km_ref[...] = k * (1.0 + (a - 1.0) * ka_ref[...])


def _front_call(x3d, prev0, tabs, wts, *, sample):
    nseq, seq_len, d = x3d.shape
    x2d = x3d.reshape(nseq * seq_len, d)
    xlast = x3d[:, seq_len - 1, :]
    n = x2d.shape[0]
    if sample:
        tm = n
        tiles_per_seq = 1
    else:
        tm = FRONT_TM
        assert seq_len % tm == 0
        tiles_per_seq = seq_len // tm
    grid = (n // tm,)
    ttab = tabs.shape[1]
    tab_blocks = ttab // tm
    row = lambda i: (i, 0)
    const2 = lambda i: (0, 0)
    prev_spec = pl.BlockSpec((tm, N_SH), row) if sample else pl.BlockSpec((1, N_SH), const2)
    xl_shape = jax.ShapeDtypeStruct((nseq, d), F32)
    xl_spec = pl.BlockSpec((nseq, d), const2)
    scratch = [pltpu.VMEM((tm + 8, N_SH), F32)]
    in_specs = [pl.BlockSpec((tm, d), row), pl.BlockSpec((nseq, d), const2), prev_spec,
                pl.BlockSpec((3, tm, LANES), lambda i: (0, i % tab_blocks, 0))]
    in_specs += [pl.BlockSpec(w.shape, const2) for w in wts]
    out_shapes = [
        jax.ShapeDtypeStruct((n, HCAT), BF16), jax.ShapeDtypeStruct((n, HCAT), BF16),
        jax.ShapeDtypeStruct((n, W_A), BF16), jax.ShapeDtypeStruct((n, KV_LORA), F32),
        jax.ShapeDtypeStruct((n, ROPE), F32), jax.ShapeDtypeStruct((n, W_A), F32),
        jax.ShapeDtypeStruct((n, W_B), F32),
    ] + [jax.ShapeDtypeStruct((n, W_B), F32)] * 6 + [xl_shape]
    out_specs = [pl.BlockSpec((tm, s.shape[1]), row) for s in out_shapes[:-1]] + [xl_spec]
    body = functools.partial(_front_body, tm=tm, tiles_per_seq=tiles_per_seq, seq_len=seq_len, sample=sample)
    return pl.pallas_call(
        body, grid=grid, in_specs=in_specs, out_specs=out_specs, out_shape=out_shapes,
        scratch_shapes=scratch, compiler_params=_cparams(("arbitrary",)),
        name="front_sample" if sample else "front_prompt",
    )(x2d, xlast, prev0, tabs, *wts)


def _matmul_body(a_ref, b_ref, o_ref):
    o_ref[...] = jnp.dot(a_ref[...].astype(BF16), b_ref[...], preferred_element_type=F32)


def _matmul_call(a, b16):
    m, k = a.shape
    n = b16.shape[1]
    return pl.pallas_call(
        _matmul_body, grid=(1,),
        in_specs=[pl.BlockSpec((m, k), lambda i: (0, 0)), pl.BlockSpec((k, n), lambda i: (0, 0))],
        out_specs=pl.BlockSpec((m, n), lambda i: (0, 0)),
        out_shape=jax.ShapeDtypeStruct((m, n), F32),
        compiler_params=_cparams(("arbitrary",)), name="prev_proj",
    )(a, b16)


def _p_attn_body(q_ref, k_ref, v_ref, o_ref, *, tq):
    qi = pl.program_id(2)
    lane = _lane_iota()
    qs = [q_ref[:, hh * LANES:(hh + 1) * LANES] for hh in range(2)]

    def scores(j, hh):
        kb = k_ref[pl.ds(pl.multiple_of(j * tq, tq), tq), hh * LANES:(hh + 1) * LANES]
        return _nt(qs[hh], kb)

    def update(carry, s, vb):
        m, l, acc = carry
        m_new = jnp.maximum(m, jnp.max(s, axis=-1, keepdims=True))
        alpha = jnp.exp(m - m_new)
        e = jnp.exp(s - m_new)
        acc = acc * alpha + jnp.dot(e.astype(BF16), vb, preferred_element_type=F32)
        return m_new, l * alpha + jnp.sum(e, axis=-1, keepdims=True), acc

    def step(j, carry):
        vb = v_ref[pl.ds(pl.multiple_of(j * tq, tq), tq), :]
        ss = [scores(j, hh) for hh in range(2)]
        return tuple(update(carry[hh], ss[hh], vb) for hh in range(2))

    init = (jnp.full((tq, 1), NEG, F32), jnp.zeros((tq, 1), F32), jnp.zeros((tq, LANES), F32))
    carry = lax.fori_loop(0, qi, step, (init, init))
    rows = lax.broadcasted_iota(jnp.int32, (tq, tq), 0)
    cols = lax.broadcasted_iota(jnp.int32, (tq, tq), 1)
    causal = cols <= rows
    vb = v_ref[pl.ds(pl.multiple_of(qi * tq, tq), tq), :]
    ss = [jnp.where(causal, scores(qi, hh), NEG) for hh in range(2)]
    outs = []
    for hh in range(2):
        _, l, acc = update(carry[hh], ss[hh], vb)
        outs.append(acc / l)
    o_ref[...] = jnp.where(lane < V_DIM, outs[0], outs[1])


def _p_attn_call(qcat, kcat, v16, *, nseq, seq_len):
    n = qcat.shape[0]
    tq = min(ATTN_T, seq_len)
    nq = seq_len // tq
    grid = (nseq, H_A // 2, nq)
    return pl.pallas_call(
        functools.partial(_p_attn_body, tq=tq), grid=grid,
        in_specs=[pl.BlockSpec((tq, 2 * LANES), lambda b, h, i: (b * nq + i, h)),
                  pl.BlockSpec((seq_len, 2 * LANES), lambda b, h, i: (b, h)),
                  pl.BlockSpec((seq_len, LANES), lambda b, h, i: (b, h))],
        out_specs=pl.BlockSpec((tq, LANES), lambda b, h, i: (b * nq + i, h)),
        out_shape=jax.ShapeDtypeStruct((n, W_A), F32),
        compiler_params=_cparams(("arbitrary", "arbitrary", "arbitrary")), name="p_attn",
    )(qcat, kcat, v16)


def _s_attn_body(pt_ref, qn_ref, qp_ref, cnew_ref, pnew_ref, wkt_ref, wuv_ref, *refs, pg, sub):
    c_refs = refs[:pg]
    p_refs = refs[pg:2 * pg]
    o_ref = refs[2 * pg]
    m_scr, l_scr, acc_scr = refs[2 * pg + 1:]
    j = pl.program_id(1)
    nj = pl.num_programs(1)
    nrow = qn_ref.shape[0]

    @pl.when(j == 0)
    def _():
        m_scr[...] = jnp.full(m_scr.shape, NEG, F32)
        l_scr[...] = jnp.zeros(l_scr.shape, F32)
        acc_scr[...] = jnp.zeros(acc_scr.shape, F32)

    def attend(c16s, pt16s, mask):
        krts = [_nt(wkt_ref[...], c16) for c16 in c16s]
        s_list = []
        for krt, pt16 in zip(krts, pt16s):
            ssq = jnp.concatenate(
                [jnp.sum(jnp.square(krt[h * NOPE:(h + 1) * NOPE, :]), axis=0, keepdims=True) for h in range(H_A)],
                axis=0)
            rinv = lax.rsqrt(ssq * (1.0 / NOPE) + EPS)
            sraw = jnp.dot(qn_ref[...], krt.astype(BF16), preferred_element_type=F32)
            spe = jnp.dot(qp_ref[...], pt16, preferred_element_type=F32)
            s = sraw * jnp.concatenate([rinv] * (nrow // H_A), axis=0) + spe
            if mask is not None:
                s = jnp.where(mask, s, NEG)
            s_list.append(s)
        m_old = m_scr[...]
        m_new = m_old
        for s in s_list:
            m_new = jnp.maximum(m_new, jnp.max(s, axis=-1, keepdims=True))
        alpha = jnp.exp(m_old - m_new)
        l_new = l_scr[...] * alpha
        acc = acc_scr[...] * alpha
        for s, c16 in zip(s_list, c16s):
            e = jnp.exp(s - m_new)
            l_new = l_new + jnp.sum(e, axis=-1, keepdims=True)
            acc = acc + jnp.dot(e.astype(BF16), c16, preferred_element_type=F32)
        l_scr[...] = l_new
        acc_scr[...] = acc
        m_scr[...] = m_new

    c16s, pt16s = [], []
    for g in range(pg // sub):
        c16s.append(jnp.concatenate([c_refs[t][...].astype(BF16) for t in range(g * sub, (g + 1) * sub)], axis=0))
        pt16s.append(jnp.concatenate([p_refs[t][...].astype(BF16) for t in range(g * sub, (g + 1) * sub)], axis=1))
    attend(c16s, pt16s, None)

    @pl.when(j == nj - 1)
    def _():
        nnew = cnew_ref.shape[0]
        rows = lax.broadcasted_iota(jnp.int32, (nrow, nnew), 0) // H_A
        cols = lax.broadcasted_iota(jnp.int32, (nrow, nnew), 1)
        attend([cnew_ref[...].astype(BF16)], [pnew_ref[...].astype(BF16)], cols <= rows)
        o_lat = acc_scr[...] / l_scr[...]
        hi, lo = _split2(o_lat)
        o_ref[...] = (jnp.dot(hi, wuv_ref[...], preferred_element_type=F32)
                      + jnp.dot(lo, wuv_ref[...], preferred_element_type=F32))


def _s_attn_call(page_table, qn_blk, qp_blk, cnew, pnew_t, wkt16, wuv16, cache_ckv, cache_kpe_t, layer):
    bd, nrow, _ = qn_blk.shape
    n_pages = page_table.shape[1]
    page = cache_ckv.shape[2]
    pg = min(PAGES_PER_STEP, n_pages)
    sub = min(PAGES_PER_SUB, pg)
    assert n_pages % pg == 0 and pg % sub == 0
    nnew = cnew.shape[1]
    grid = (bd, n_pages // pg)
    seq3 = lambda b, j, pt: (b, 0, 0)
    const2 = lambda b, j, pt: (0, 0)

    def page_map(t):
        return lambda b, j, pt: (layer, pt[b, j * pg + t], 0, 0)

    in_specs = [pl.BlockSpec((None, nrow, W_A), seq3), pl.BlockSpec((None, nrow, ROPE), seq3),
                pl.BlockSpec((None, nnew, KV_LORA), seq3), pl.BlockSpec((None, ROPE, nnew), seq3),
                pl.BlockSpec(wkt16.shape, const2), pl.BlockSpec(wuv16.shape, const2)]
    in_specs += [pl.BlockSpec((None, None, page, KV_LORA), page_map(t)) for t in range(pg)]
    in_specs += [pl.BlockSpec((None, None, ROPE, page), page_map(t)) for t in range(pg)]
    gs = pltpu.PrefetchScalarGridSpec(
        num_scalar_prefetch=1, grid=grid, in_specs=in_specs,
        out_specs=pl.BlockSpec((None, nrow, W_A), seq3),
        scratch_shapes=[pltpu.VMEM((nrow, 1), F32), pltpu.VMEM((nrow, 1), F32), pltpu.VMEM((nrow, KV_LORA), F32)])
    return pl.pallas_call(
        functools.partial(_s_attn_body, pg=pg, sub=sub), grid_spec=gs,
        out_shape=jax.ShapeDtypeStruct((bd, nrow, W_A), F32),
        compiler_params=_cparams(("arbitrary", "arbitrary")), name="s_attn",
    )(page_table, qn_blk, qp_blk, cnew, pnew_t, wkt16, wuv16,
      *([cache_ckv] * pg), *([cache_kpe_t] * pg))


def _rwkv_body(r_ref, lw_ref, km_ref, v_ref, kk_ref, kka_ref, gb_ref, s0_ref, rk_ref, lnw_ref, lnb_ref,
               o_ref, sout_ref, s_scr, *, c, nb):
    ci = pl.program_id(1)
    nc = pl.num_programs(1)
    g4 = RW_GROUP
    items = [(s, g) for s in range(nb) for g in range(RW_NGROUPS)]

    rr = lax.broadcasted_iota(jnp.int32, (g4, g4), 0) // N_B
    cc = lax.broadcasted_iota(jnp.int32, (g4, g4), 1) // N_B
    blockmask = rr == cc
    ones_blk = blockmask.astype(BF16)
    t_idx = lax.broadcasted_iota(jnp.int32, (c, g4), 0)
    i_idx = lax.broadcasted_iota(jnp.int32, (c, g4), 1) % N_B
    low_strict = i_idx < t_idx
    low_incl = i_idx <= t_idx
    eye = (i_idx == t_idx).astype(F32)

    def bd_rows(y):
        y16 = y.astype(BF16)
        if c == N_B:
            t = jnp.concatenate([y16] * 4, axis=0)
        else:
            pad = jnp.zeros((N_B - c, g4), BF16)
            t = jnp.concatenate([y16, pad] * 4, axis=0)
        return jnp.where(blockmask, t, jnp.zeros_like(t))

    def mmh(x, y):
        return jnp.dot(x.astype(BF16), bd_rows(y), preferred_element_type=F32)

    @pl.when(ci == 0)
    def _():
        s_scr[...] = jnp.zeros(s_scr.shape, F32)
        for s in range(nb):
            for h in range(H_B):
                g, jh = divmod(h, 4)
                s_scr[s, g, jh * N_B:(jh + 1) * N_B, jh * N_B:(jh + 1) * N_B] = s0_ref[s, h]

    tri = (lax.broadcasted_iota(jnp.int32, (c, c), 1) <= lax.broadcasted_iota(jnp.int32, (c, c), 0)).astype(BF16)
    g_seq = []
    for s in range(nb):
        g_seq.append(sum(jnp.dot(tri, part, preferred_element_type=F32) for part in _split3(lw_ref[s])))

    st = []
    for (s, g) in items:
        sl = slice(g * g4, (g + 1) * g4)
        lw = lw_ref[s, :, sl]
        gc = g_seq[s][:, sl]
        big = jnp.exp(gc)
        ginv = jnp.exp(-gc)
        at = -kk_ref[s, :, sl] * jnp.exp(gc - lw)
        bt = kka_ref[s, :, sl] * ginv
        kt = km_ref[s, :, sl] * ginv
        rt = r_ref[s, :, sl] * big
        s_old = s_scr[s, g]
        st.append(dict(sl=sl, s=s, g=g, bt=bt, kt=kt, g_last=big[c - 1:c, :], s_old=s_old,
                       lhs2=jnp.concatenate([at, rt], axis=0).astype(BF16), v=v_ref[s, :, sl]))

    for d in st:
        x1 = _nt(d["lhs2"], bd_rows(d["bt"]))
        x2 = _nt(d["lhs2"], bd_rows(d["kt"]))
        d["x3"] = _nt(d["lhs2"], d["s_old"].astype(BF16))
        d["a_ab"] = jnp.where(low_strict, x1[:c], 0.0)
        d["a_rb"] = jnp.where(low_incl, x1[c:], 0.0)
        d["a_ak"] = jnp.where(low_strict, x2[:c], 0.0)
        d["a_rk"] = jnp.where(low_incl, x2[c:], 0.0)
    for d in st:
        x4 = jnp.dot(jnp.concatenate([d["a_ak"], d["a_rk"]], axis=0).astype(BF16), bd_rows(d["v"]),
                     preferred_element_type=F32)
        d["b"] = x4[:c] + d["x3"][:c]
        d["ypart"] = x4[c:] + d["x3"][c:]

    if c <= 16:
        for d in st:
            d["tinv"] = eye + d["a_ab"]
            d["pw"] = d["a_ab"]
        n = 1
        while 2 * n < c:
            for d in st:
                d["pw"] = mmh(d["pw"], d["pw"])
            for d in st:
                d["tinv"] = mmh(d["tinv"], eye + d["pw"])
            n *= 2
        for d in st:
            d["u"] = mmh(d["tinv"], d["b"])
    else:
        same16 = (i_idx // 16) == (t_idx // 16)
        for d in st:
            dg = jnp.where(same16, d["a_ab"], 0.0)
            d["lo"] = d["a_ab"] - dg
            d["td"] = eye + dg
            d["pw"] = dg
        for _ in range(3):
            for d in st:
                d["pw"] = mmh(d["pw"], d["pw"])
            for d in st:
                d["td"] = mmh(d["td"], eye + d["pw"])
        for d in st:
            d["nn"] = mmh(d["td"], d["lo"])
            d["w"] = mmh(d["td"], d["b"])
        for d in st:
            d["n2"] = mmh(d["nn"], d["nn"])
        for d in st:
            d["w2"] = d["w"] + mmh(d["n2"], d["w"])
        for d in st:
            d["u"] = d["w2"] + mmh(d["nn"], d["w2"])

    for d in st:
        d["y"] = d["ypart"] + mmh(d["a_rb"], d["u"])
        lhs_t = jnp.concatenate([d["v"], d["u"]], axis=0).astype(BF16)
        rhs_t = jnp.concatenate([d["kt"] * d["g_last"], d["bt"] * d["g_last"]], axis=0).astype(BF16)
        upd = lax.dot_general(lhs_t, rhs_t, (((0,), (0,)), ((), ())), preferred_element_type=F32)
        s_scr[d["s"], d["g"]] = d["s_old"] * d["g_last"] + jnp.where(blockmask, upd, 0.0)

    def head_sums(xs):
        parts = []
        for x in xs:
            parts.extend(_split2(x))
        res = jnp.dot(jnp.concatenate(parts, axis=0), ones_blk, preferred_element_type=F32)
        return [res[(2 * k) * c:(2 * k + 1) * c] + res[(2 * k + 1) * c:(2 * k + 2) * c] for k in range(len(xs))]

    rkr = [r_ref[d["s"], :, d["sl"]] * km_ref[d["s"], :, d["sl"]] * rk_ref[:, d["sl"]] for d in st]
    sums = head_sums([d["y"] for d in st] + rkr)
    devs = [d["y"] - sums[k] * (1.0 / N_B) for k, d in enumerate(st)]
    var_sums = head_sums([dv * dv for dv in devs])
    for k, d in enumerate(st):
        sl = d["sl"]
        yn = devs[k] * lax.rsqrt(var_sums[k] * (1.0 / N_B) + LNX_EPS) * lnw_ref[:, sl] + lnb_ref[:, sl]
        bonus = sums[len(st) + k] * d["v"]
        gb = gb_ref[d["s"], :, sl]
        o_ref[d["s"], :, sl] = (yn + bonus) * (gb * jax.nn.sigmoid(gb))

    @pl.when(ci == nc - 1)
    def _():
        for s in range(nb):
            for h in range(H_B):
                g, jh = divmod(h, 4)
                sout_ref[s, h] = s_scr[s, g, jh * N_B:(jh + 1) * N_B, jh * N_B:(jh + 1) * N_B]


def _rwkv_call(r, lw, km, v, kk, kka, gb, s0, rk, lnw, lnb, *, c, nb):
    nseq, t, _ = r.shape
    nc = t // c
    assert nseq % nb == 0 and t % c == 0
    tok = pl.BlockSpec((nb, c, W_B), lambda b, i: (b, i, 0))
    vec = pl.BlockSpec((1, W_B), lambda b, i: (0, 0))
    st = pl.BlockSpec((nb, H_B, N_B, N_B), lambda b, i: (b, 0, 0, 0))
    return pl.pallas_call(
        functools.partial(_rwkv_body, c=c, nb=nb), grid=(nseq // nb, nc),
        in_specs=[tok] * 7 + [st, vec, vec, vec],
        out_specs=[tok, st],
        out_shape=[jax.ShapeDtypeStruct((nseq, t, W_B), F32), jax.ShapeDtypeStruct((nseq, H_B, N_B, N_B), F32)],
        scratch_shapes=[pltpu.VMEM((nb, RW_NGROUPS, RW_GROUP, RW_GROUP), F32)],
        compiler_params=_cparams(("arbitrary", "arbitrary")), name=f"rwkv_c{c}",
    )(r, lw, km, v, kk, kka, gb, s0, rk, lnw, lnb)


def _back_body(x_ref, oa_ref, ga_ref, ob_ref, p_ref, woa_ref, wob_ref, wple_ref, pn_ref, wg_ref, y_ref):
    ga = ga_ref[...]
    mixed_a = oa_ref[...] * (ga * jax.nn.sigmoid(ga))
    x1 = (x_ref[...] + jnp.dot(mixed_a.astype(BF16), woa_ref[...], preferred_element_type=F32)
          + jnp.dot(ob_ref[...].astype(BF16), wob_ref[...], preferred_element_type=F32))
    xg = _rms(x1, pn_ref[...], x1.shape[-1])
    gate = jax.nn.sigmoid(jnp.dot(xg.astype(BF16), wg_ref[...], preferred_element_type=F32))
    y_ref[...] = x1 + jnp.dot(p_ref[...].astype(BF16), wple_ref[...], preferred_element_type=F32) * gate


def _back_call(x2d, oa, ga, ob, p2d, wts):
    n, d = x2d.shape
    tm = min(BACK_TM, n)
    row = lambda i: (i, 0)
    const2 = lambda i: (0, 0)
    ins = [x2d, oa, ga, ob, p2d]
    return pl.pallas_call(
        _back_body, grid=(n // tm,),
        in_specs=[pl.BlockSpec((tm, a.shape[1]), row) for a in ins] + [pl.BlockSpec(w.shape, const2) for w in wts],
        out_specs=pl.BlockSpec((tm, d), row),
        out_shape=jax.ShapeDtypeStruct((n, d), F32),
        compiler_params=_cparams(("arbitrary",)), name="back",
    )(*ins, *wts)


def _pad_cols(a, width):
    return jnp.pad(a, ((0, 0), (0, width - a.shape[1])))


def _row(vec):
    return vec.reshape(1, -1).astype(F32)


def _rope_tabs(pos):
    inv = ROPE_BASE ** (-jnp.arange(0, ROPE, 2, dtype=F32) / ROPE)
    ang = pos.astype(F32)[:, None] * inv[None, :]
    cos, sin = jnp.cos(ang), jnp.sin(ang)
    t = pos.shape[0]
    z16 = jnp.zeros((t, ROPE // 2), F32)
    c = jnp.concatenate([cos, cos, jnp.ones((t, NOPE), F32), jnp.zeros((t, LANES - ROPE - NOPE), F32)], axis=1)
    s1 = _pad_cols(jnp.concatenate([-sin, z16], axis=1), LANES)
    s2 = _pad_cols(jnp.concatenate([z16, sin], axis=1), LANES)
    return jnp.stack([c, s1, s2])


def _layer_weights(i, norm_in, w_in, mu_shift, q_a_norm, w_uq, kv_a_norm, w_ukv, g_q_nope, g_q_pe, g_k_nope,
                   g_k_pe, w_decay0, w_decay_up, a0, w_iclr_up, k_k, k_a):
    w = w_in[i]
    c_q, c_kv, c_kpe = 0, Q_LORA, Q_LORA + KV_LORA
    c_ga = c_kpe + ROPE
    c_gb = c_ga + W_A
    c_sh = c_gb + W_B
    c_xw = c_sh + 3 * W_B
    c_xa = c_xw + DECAY_LORA
    w_perm = jnp.concatenate([
        w[:, c_q:c_kpe], _pad_cols(w[:, c_kpe:c_ga], LANES), w[:, c_ga:c_sh], w[:, c_sh:c_xw],
        _pad_cols(w[:, c_xw:c_xa], LANES), _pad_cols(w[:, c_xa:], LANES)], axis=1).astype(BF16)
    mu = mu_shift[i][None, :]
    mu_perm = jnp.concatenate([mu[:, :3 * W_B], _pad_cols(mu[:, 3 * W_B:3 * W_B + DECAY_LORA], LANES),
                               _pad_cols(mu[:, 3 * W_B + DECAY_LORA:], LANES)], axis=1)
    uq = w_uq[i]
    uq_cat = jnp.concatenate([uq[..., NOPE:], uq[..., :NOPE],
                              jnp.zeros((Q_LORA, H_A, LANES - NOPE - ROPE), F32)], axis=-1)
    uq_cat = uq_cat.reshape(Q_LORA, HCAT).astype(BF16)
    ukv = w_ukv[i]
    uk = ukv[..., :NOPE]
    uk_cat = jnp.concatenate([jnp.zeros((KV_LORA, H_A, ROPE), F32), uk,
                              jnp.zeros((KV_LORA, H_A, LANES - NOPE - ROPE), F32)], axis=-1)
    uk_cat = uk_cat.reshape(KV_LORA, HCAT).astype(BF16)
    uv = ukv[..., NOPE:].reshape(KV_LORA, W_A).astype(BF16)
    ukt = jnp.transpose(uk.reshape(KV_LORA, H_A * NOPE)).astype(BF16)
    zpad = jnp.zeros((LANES - NOPE - ROPE,), F32)
    gq_prompt = jnp.concatenate([g_q_pe[i], g_q_nope[i], zpad]) * SCALE
    gq_sample = jnp.concatenate([g_q_pe[i], g_q_nope[i] * g_k_nope[i], zpad]) * SCALE
    gk = jnp.concatenate([jnp.zeros((ROPE,), F32), g_k_nope[i], zpad])
    gkp = jnp.concatenate([g_k_pe[i], jnp.zeros((LANES - ROPE,), F32)])
    wdu = jnp.pad(w_decay_up[i], ((0, LANES - DECAY_LORA), (0, 0))).astype(BF16)
    wiu = jnp.pad(w_iclr_up[i], ((0, LANES - ICLR_LORA), (0, 0))).astype(BF16)

    def front_wts(gq):
        return [_row(norm_in[i]), w_perm, mu_perm, _row(q_a_norm[i]), uq_cat, _row(kv_a_norm[i]), uk_cat, uv,
                _row(gq), _row(gk), _row(gkp), _row(w_decay0[i]), wdu, _row(a0[i]), wiu, _row(k_k[i]), _row(k_a[i])]

    return front_wts(gq_prompt), front_wts(gq_sample), w_perm[:, P_SH:], ukt, uv


def kernel(x_prompt, x_sample, p_prompt, p_sample, cache_ckv, cache_kpe, state_wkv, state_shift, page_table,
           norm_in, w_in, mu_shift, q_a_norm, w_uq, kv_a_norm, w_ukv, g_q_nope, g_q_pe, g_k_nope, g_k_pe,
           w_decay0, w_decay_up, a0, w_iclr_up, k_k, k_a, r_k, lnx_w, lnx_b, w_out, w_ple, ple_norm, w_ple_gate):
    b, t, d = x_prompt.shape
    bd, tq, _ = x_sample.shape
    depth = w_in.shape[0]
    page = cache_ckv.shape[2]
    past_len = page_table.shape[1] * page
    tabs_p = _rope_tabs(jnp.arange(t))
    tabs_s = jnp.tile(_rope_tabs(past_len + jnp.arange(tq)), (1, bd, 1))
    cache_kpe_t = jnp.swapaxes(cache_kpe, 2, 3)
    y_p = x_prompt.reshape(b * t, d)
    y_s = x_sample.reshape(bd * tq, d)
    outs = [[] for _ in range(8)]
    for i in range(depth):
        wts_p, wts_s, w_shift16, ukt16, uv16 = _layer_weights(
            i, norm_in, w_in, mu_shift, q_a_norm, w_uq, kv_a_norm, w_ukv, g_q_nope, g_q_pe, g_k_nope, g_k_pe,
            w_decay0, w_decay_up, a0, w_iclr_up, k_k, k_a)
        rw_vecs = [_row(r_k[i]), _row(lnx_w[i]), _row(lnx_b[i])]
        wo = w_out[i].astype(BF16)
        back_wts = [wo[:W_A], wo[W_A:], w_ple[i].astype(BF16), _row(ple_norm[i]), w_ple_gate[i].astype(BF16)]

        (qcat, kcat, v16, ckv, kpe, ga, gb, r, lw, km, vb, kkn, kka, xl) = _front_call(
            y_p.reshape(b, t, d), jnp.zeros((1, N_SH), F32), tabs_p, wts_p, sample=False)
        o_a = _p_attn_call(qcat, kcat, v16, nseq=b, seq_len=t)
        seq3 = lambda a: a.reshape(b, t, W_B)
        o_b, s_p = _rwkv_call(seq3(r), seq3(lw), seq3(km), seq3(vb), seq3(kkn), seq3(kka), seq3(gb),
                              jnp.zeros((b, H_B, N_B, N_B), F32), *rw_vecs, c=min(RW_CHUNK, t), nb=2)
        y_p = _back_call(y_p, o_a, ga, o_b.reshape(b * t, W_B), p_prompt[i].reshape(b * t, -1), back_wts)
        outs[0].append(ckv.reshape(b, t, KV_LORA)); outs[2].append(kpe.reshape(b, t, ROPE))
        outs[4].append(s_p); outs[6].append(xl)

        prev_proj = _matmul_call(state_shift[i], w_shift16)
        prev0 = jnp.pad(prev_proj[:, None, :], ((0, 0), (0, tq - 1), (0, 0))).reshape(bd * tq, N_SH)
        (qcat, _, _, ckv, kpe, ga, gb, r, lw, km, vb, kkn, kka, xl) = _front_call(
            y_s.reshape(bd, tq, d), prev0, tabs_s, wts_s, sample=True)
        q4 = qcat.reshape(bd, tq, H_A, LANES)
        qp_blk = q4[..., :ROPE].reshape(bd, tq * H_A, ROPE)
        eye_h = jnp.eye(H_A, dtype=BF16)
        qn_blk = (q4[..., ROPE:ROPE + NOPE][:, :, :, None, :] * eye_h[None, None, :, :, None]
                  ).reshape(bd, tq * H_A, W_A)
        npad = LANES
        cnew = jnp.pad(ckv.reshape(bd, tq, KV_LORA), ((0, 0), (0, npad - tq), (0, 0)))
        pnew_t = jnp.swapaxes(jnp.pad(kpe.reshape(bd, tq, ROPE), ((0, 0), (0, npad - tq), (0, 0))), 1, 2)
        o_full = _s_attn_call(page_table, qn_blk, qp_blk, cnew, pnew_t, ukt16, uv16, cache_ckv, cache_kpe_t, i)
        o5 = o_full.reshape(bd, tq, H_A, H_A, V_DIM)
        o_a = jnp.einsum('bqhhd->bqhd', o5).reshape(bd * tq, W_A)
        cpad = 16

        def pad_tok(a):
            return jnp.pad(a.reshape(bd, tq, -1), ((0, 0), (0, cpad - tq), (0, 0)))

        o_b16, s_s = _rwkv_call(pad_tok(r), pad_tok(lw), pad_tok(km), pad_tok(vb), pad_tok(kkn), pad_tok(kka),
                                pad_tok(gb), state_wkv[i], *rw_vecs, c=cpad, nb=4)
        o_b = o_b16[:, :tq].reshape(bd * tq, W_B)
        y_s = _back_call(y_s, o_a, ga, o_b, p_sample[i].reshape(bd * tq, -1), back_wts)
        outs[1].append(ckv.reshape(bd, tq, KV_LORA)); outs[3].append(kpe.reshape(bd, tq, ROPE))
        outs[5].append(s_s); outs[7].append(xl)
    st = [jnp.stack(o) for o in outs]
    return (y_p.reshape(b, t, d), y_s.reshape(bd, tq, d), st[0], st[1], st[2], st[3], st[4], st[5], st[6], st[7])
```

```python
import functools

import jax
import jax.numpy as jnp
from jax import lax
from jax.experimental import pallas as pl
from jax.experimental.pallas import tpu as pltpu

F32 = jnp.float32
BF16 = jnp.bfloat16

LANES = 128
H_A = 8
NOPE = 64
ROPE = 32
V_DIM = 64
Q_LORA = 384
KV_LORA = 256
ROPE_BASE = 10000.0
SCALE = (NOPE + ROPE) ** -0.5
H_B = 8
N_B = 64
W_A = H_A * V_DIM
W_B = H_B * N_B
DECAY_LORA = 64
ICLR_LORA = 64
LNX_EPS = 64e-5
EPS = 1e-6
NEG = -1e30

P_Q = 0
P_KV = P_Q + Q_LORA
P_KPE = P_KV + KV_LORA
P_GA = P_KPE + LANES
P_GB = P_GA + W_A
P_SH = P_GB + W_B
N_SH = 3 * W_B + 2 * LANES
P_TOT = P_SH + N_SH

HCAT = H_A * LANES
RW_CHUNK = 64
RW_GROUP = 4 * N_B
RW_NGROUPS = W_B // RW_GROUP
FRONT_TM = 256
BACK_TM = 512
ATTN_T = 512
PAGES_PER_STEP = 16
PAGES_PER_SUB = 4
VMEM_LIMIT = 56 * 1024 * 1024


def _cparams(sem):
    return pltpu.CompilerParams(dimension_semantics=sem, vmem_limit_bytes=VMEM_LIMIT)


def _lane_iota(n=LANES):
    return lax.broadcasted_iota(jnp.int32, (1, n), 1)


def _rms(x, g, n):
    ms = jnp.sum(x * x, axis=-1, keepdims=True) * (1.0 / n)
    return x * lax.rsqrt(ms + EPS) * g


def _rope128(n, c, s1, s2):
    return n * c + pltpu.roll(n, LANES - ROPE // 2, axis=1) * s1 + pltpu.roll(n, ROPE // 2, axis=1) * s2


def _split2(x):
    hi = x.astype(BF16)
    lo = (x - hi.astype(F32)).astype(BF16)
    return hi, lo


def _split3(x):
    hi = x.astype(BF16)
    r1 = x - hi.astype(F32)
    mid = r1.astype(BF16)
    lo = (r1 - mid.astype(F32)).astype(BF16)
    return hi, mid, lo


def _nt(x16, y16):
    return lax.dot_general(x16, y16, (((1,), (1,)), ((), ())), preferred_element_type=F32)


def _front_body(x_ref, xlast_ref, prev0_ref, tab_ref, norm_in_ref, w_in_ref, mu_ref, qan_ref, wuq_ref, kvan_ref,
                wuk_ref, wuv_ref, gq_ref, gk_ref, gkp_ref, wd0_ref, wdu_ref, a0_ref, wiu_ref, kk_ref_, ka_ref,
                qcat_ref, kcat_ref, v_ref, ckv_ref, kpe_ref, ga_ref, gb_ref,
                r_ref, lw_ref, km_ref, vb_ref, kkn_ref, kka_ref, xl_ref,
                cur_scr, *, tm, tiles_per_seq, seq_len, sample):
    i = pl.program_id(0)
    x = x_ref[...]
    xn = _rms(x, norm_in_ref[...], x.shape[-1])
    xl_ref[...] = _rms(xlast_ref[...], norm_in_ref[...], x.shape[-1])
    xn16 = xn.astype(BF16)

    def proj(lo, hi):
        return jnp.dot(xn16, w_in_ref[:, lo:hi], preferred_element_type=F32)

    c_tab, s1_tab, s2_tab = tab_ref[0], tab_ref[1], tab_ref[2]
    lane = _lane_iota()
    is_rope = lane < ROPE

    c_q = _rms(proj(P_Q, P_KV), qan_ref[...], Q_LORA)
    q = jnp.dot(c_q.astype(BF16), wuq_ref[...], preferred_element_type=F32)
    gq = gq_ref[...]
    for h in range(H_A):
        qb = q[:, h * LANES:(h + 1) * LANES]
        sq = qb * qb
        s_r = jnp.sum(jnp.where(is_rope, sq, 0.0), axis=-1, keepdims=True)
        s_n = jnp.sum(jnp.where(is_rope, 0.0, sq), axis=-1, keepdims=True)
        inv = jnp.where(is_rope, lax.rsqrt(s_r * (1.0 / ROPE) + EPS), lax.rsqrt(s_n * (1.0 / NOPE) + EPS))
        qn = qb * inv * gq
        qcat_ref[:, h * LANES:(h + 1) * LANES] = _rope128(qn, c_tab, s1_tab, s2_tab).astype(BF16)

    p_kv = proj(P_KV, P_GA)
    c_kv = _rms(p_kv[:, :KV_LORA], kvan_ref[...], KV_LORA)
    ckv_ref[...] = c_kv
    c_kv16 = c_kv.astype(BF16)
    v_ref[...] = jnp.dot(c_kv16, wuv_ref[...], preferred_element_type=F32).astype(BF16)
    kraw = jnp.dot(c_kv16, wuk_ref[...], preferred_element_type=F32)
    kp = p_kv[:, KV_LORA:]
    kp_n = kp * lax.rsqrt(jnp.sum(kp * kp, axis=-1, keepdims=True) * (1.0 / ROPE) + EPS) * gkp_ref[...]
    kp_r = _rope128(kp_n, c_tab, s1_tab, s2_tab)
    kpe_ref[...] = kp_r[:, :ROPE]
    gk = gk_ref[...]
    for h in range(H_A):
        kb = kraw[:, h * LANES:(h + 1) * LANES]
        s_n = jnp.sum(kb * kb, axis=-1, keepdims=True)
        kn = kb * lax.rsqrt(s_n * (1.0 / NOPE) + EPS) * gk
        kcat_ref[:, h * LANES:(h + 1) * LANES] = (kn + kp_r).astype(BF16)

    ga_ref[...] = proj(P_GA, P_GB)
    gb_ref[...] = proj(P_GB, P_SH)

    cur = proj(P_SH, P_TOT)
    if sample:
        cur_scr[8:8 + tm, :] = cur
        cur_scr[7:8, :] = jnp.zeros((1, N_SH), F32)
        row = lax.broadcasted_iota(jnp.int32, (tm, 1), 0)
        prev = jnp.where(row % seq_len == 0, prev0_ref[...], cur_scr[7:7 + tm, :])
    else:
        @pl.when(i % tiles_per_seq == 0)
        def _():
            cur_scr[7:8, :] = prev0_ref[...]
        cur_scr[8:8 + tm, :] = cur
        prev = cur_scr[7:7 + tm, :]
        cur_scr[7:8, :] = cur[tm - 1:tm, :]
    sh = cur + (prev - cur) * mu_ref[...]
    r = sh[:, 0:W_B]
    k = sh[:, W_B:2 * W_B]
    v = sh[:, 2 * W_B:3 * W_B]
    xw = sh[:, 3 * W_B:3 * W_B + LANES]
    xa = sh[:, 3 * W_B + LANES:3 * W_B + 2 * LANES]
    z = wd0_ref[...] + jnp.dot(jnp.tanh(xw).astype(BF16), wdu_ref[...], preferred_element_type=F32)
    nz = -z
    softplus = jnp.maximum(nz, 0.0) + jnp.log1p(jnp.exp(-jnp.abs(nz)))
    w_log = -softplus - 0.5
    lw_ref[...] = -jnp.exp(w_log)
    a = jax.nn.sigmoid(a0_ref[...] + jnp.dot(xa.astype(BF16), wiu_ref[...], preferred_element_type=F32))
    kk = k * kk_ref_[...]
    lo_half = lane < N_B
    for c in range(W_B // LANES):
        blk = kk[:, c * LANES:(c + 1) * LANES]
        sq = blk * blk
        s0 = jnp.sum(jnp.where(lo_half, sq, 0.0), axis=-1, keepdims=True)
        s1 = jnp.sum(jnp.where(lo_half, 0.0, sq), axis=-1, keepdims=True)
        den = jnp.where(lo_half, jnp.maximum(jnp.sqrt(s0), 1e-12), jnp.maximum(jnp.sqrt(s1), 1e-12))
        kkn = blk / den
        kkn_ref[:, c * LANES:(c + 1) * LANES] = kkn
        kka_ref[:, c * LANES:(c + 1) * LANES] = kkn * a[:, c * LANES:(c + 1) * LANES]
    r_ref[...] = r
    vb_ref[...] = v
    km_ref[...] = k * (1.0 + (a - 1.0) * ka_ref[...])


def _front_call(x3d, prev0, tabs, wts, *, sample):
    nseq, seq_len, d = x3d.shape
    x2d = x3d.reshape(nseq * seq_len, d)
    xlast = x3d[:, seq_len - 1, :]
    n = x2d.shape[0]
    if sample:
        tm = n
        tiles_per_seq = 1
    else:
        tm = FRONT_TM
        assert seq_len % tm == 0
        tiles_per_seq = seq_len // tm
    grid = (n // tm,)
    ttab = tabs.shape[1]
    tab_blocks = ttab // tm
    row = lambda i: (i, 0)
    const2 = lambda i: (0, 0)
    prev_spec = pl.BlockSpec((tm, N_SH), row) if sample else pl.BlockSpec((1, N_SH), const2)
    xl_shape = jax.ShapeDtypeStruct((nseq, d), F32)
    xl_spec = pl.BlockSpec((nseq, d), const2)
    scratch = [pltpu.VMEM((tm + 8, N_SH), F32)]
    in_specs = [pl.BlockSpec((tm, d), row), pl.BlockSpec((nseq, d), const2), prev_spec,
                pl.BlockSpec((3, tm, LANES), lambda i: (0, i % tab_blocks, 0))]
    in_specs += [pl.BlockSpec(w.shape, const2) for w in wts]
    out_shapes = [
        jax.ShapeDtypeStruct((n, HCAT), BF16), jax.ShapeDtypeStruct((n, HCAT), BF16),
        jax.ShapeDtypeStruct((n, W_A), BF16), jax.ShapeDtypeStruct((n, KV_LORA), F32),
        jax.ShapeDtypeStruct((n, ROPE), F32), jax.ShapeDtypeStruct((n, W_A), F32),
        jax.ShapeDtypeStruct((n, W_B), F32),
    ] + [jax.ShapeDtypeStruct((n, W_B), F32)] * 6 + [xl_shape]
    out_specs = [pl.BlockSpec((tm, s.shape[1]), row) for s in out_shapes[:-1]] + [xl_spec]
    body = functools.partial(_front_body, tm=tm, tiles_per_seq=tiles_per_seq, seq_len=seq_len, sample=sample)
    return pl.pallas_call(
        body, grid=grid, in_specs=in_specs, out_specs=out_specs, out_shape=out_shapes,
        scratch_shapes=scratch, compiler_params=_cparams(("arbitrary",)),
        name="front_sample" if sample else "front_prompt",
    )(x2d, xlast, prev0, tabs, *wts)


def _matmul_body(a_ref, b_ref, o_ref):
    o_ref[...] = jnp.dot(a_ref[...].astype(BF16), b_ref[...], preferred_element_type=F32)


def _matmul_call(a, b16):
    m, k = a.shape
    n = b16.shape[1]
    return pl.pallas_call(
        _matmul_body, grid=(1,),
        in_specs=[pl.BlockSpec((m, k), lambda i: (0, 0)), pl.BlockSpec((k, n), lambda i: (0, 0))],
        out_specs=pl.BlockSpec((m, n), lambda i: (0, 0)),
        out_shape=jax.ShapeDtypeStruct((m, n), F32),
        compiler_params=_cparams(("arbitrary",)), name="prev_proj",
    )(a, b16)


def _p_attn_body(q_ref, k_ref, v_ref, o_ref, *, tq):
    qi = pl.program_id(2)
    lane = _lane_iota()
    qs = [q_ref[:, hh * LANES:(hh + 1) * LANES] for hh in range(2)]

    def scores(j, hh):
        kb = k_ref[pl.ds(pl.multiple_of(j * tq, tq), tq), hh * LANES:(hh + 1) * LANES]
        return _nt(qs[hh], kb)

    def update(carry, s, vb):
        m, l, acc = carry
        m_new = jnp.maximum(m, jnp.max(s, axis=-1, keepdims=True))
        alpha = jnp.exp(m - m_new)
        e = jnp.exp(s - m_new)
        acc = acc * alpha + jnp.dot(e.astype(BF16), vb, preferred_element_type=F32)
        return m_new, l * alpha + jnp.sum(e, axis=-1, keepdims=True), acc

    def step(j, carry):
        vb = v_ref[pl.ds(pl.multiple_of(j * tq, tq), tq), :]
        ss = [scores(j, hh) for hh in range(2)]
        return tuple(update(carry[hh], ss[hh], vb) for hh in range(2))

    init = (jnp.full((tq, 1), NEG, F32), jnp.zeros((tq, 1), F32), jnp.zeros((tq, LANES), F32))
    carry = lax.fori_loop(0, qi, step, (init, init))
    half = tq // 2
    base = pl.multiple_of(qi * tq, tq)
    tri = (lax.broadcasted_iota(jnp.int32, (half, half), 1) <= lax.broadcasted_iota(jnp.int32, (half, half), 0))
    low = (lax.broadcasted_iota(jnp.int32, (half, tq), 1) <= lax.broadcasted_iota(jnp.int32, (half, tq), 0) + half)
    tops, bots = [], []
    for hh in range(2):
        m, l, acc = carry[hh]
        kb = k_ref[pl.ds(base, tq), hh * LANES:(hh + 1) * LANES]
        s_top = jnp.where(tri, _nt(qs[hh][:half], kb[:half]), NEG)
        s_bot = jnp.where(low, _nt(qs[hh][half:], kb), NEG)
        _, l_t, a_t = update((m[:half], l[:half], acc[:half]), s_top, v_ref[pl.ds(base, half), :])
        _, l_b, a_b = update((m[half:], l[half:], acc[half:]), s_bot, v_ref[pl.ds(base, tq), :])
        tops.append(a_t / l_t)
        bots.append(a_b / l_b)
    o_ref[:half, :] = jnp.where(lane < V_DIM, tops[0], tops[1])
    o_ref[half:, :] = jnp.where(lane < V_DIM, bots[0], bots[1])


def _p_attn_call(qcat, kcat, v16, *, nseq, seq_len):
    n = qcat.shape[0]
    tq = min(ATTN_T, seq_len)
    nq = seq_len // tq
    grid = (nseq, H_A // 2, nq)
    return pl.pallas_call(
        functools.partial(_p_attn_body, tq=tq), grid=grid,
        in_specs=[pl.BlockSpec((tq, 2 * LANES), lambda b, h, i: (b * nq + i, h)),
                  pl.BlockSpec((seq_len, 2 * LANES), lambda b, h, i: (b, h)),
                  pl.BlockSpec((seq_len, LANES), lambda b, h, i: (b, h))],
        out_specs=pl.BlockSpec((tq, LANES), lambda b, h, i: (b * nq + i, h)),
        out_shape=jax.ShapeDtypeStruct((n, W_A), F32),
        compiler_params=_cparams(("arbitrary", "arbitrary", "arbitrary")), name="p_attn",
    )(qcat, kcat, v16)


def _s_attn_body(pt_ref, qn_ref, qp_ref, cnew_ref, pnew_ref, wkt_ref, wuv_ref, *refs, pg, sub):
    c_refs = refs[:pg]
    p_refs = refs[pg:2 * pg]
    o_ref = refs[2 * pg]
    m_scr, l_scr, acc_scr = refs[2 * pg + 1:]
    j = pl.program_id(1)
    nj = pl.num_programs(1)
    nrow = qn_ref.shape[0]

    @pl.when(j == 0)
    def _():
        m_scr[...] = jnp.full(m_scr.shape, NEG, F32)
        l_scr[...] = jnp.zeros(l_scr.shape, F32)
        acc_scr[...] = jnp.zeros(acc_scr.shape, F32)

    def attend(c16s, pt16s, mask):
        krts = [_nt(wkt_ref[...], c16) for c16 in c16s]
        s_list = []
        for krt, pt16 in zip(krts, pt16s):
            ssq = jnp.concatenate(
                [jnp.sum(jnp.square(krt[h * NOPE:(h + 1) * NOPE, :]), axis=0, keepdims=True) for h in range(H_A)],
                axis=0)
            rinv = lax.rsqrt(ssq * (1.0 / NOPE) + EPS)
            sraw = jnp.dot(qn_ref[...], krt.astype(BF16), preferred_element_type=F32)
            spe = jnp.dot(qp_ref[...], pt16, preferred_element_type=F32)
            s = sraw * jnp.concatenate([rinv] * (nrow // H_A), axis=0) + spe
            if mask is not None:
                s = jnp.where(mask, s, NEG)
            s_list.append(s)
        m, l, acc = m_scr[...], l_scr[...], acc_scr[...]
        for s, c16 in zip(s_list, c16s):
            m_new = jnp.maximum(m, jnp.max(s, axis=-1, keepdims=True))
            alpha = jnp.exp(m - m_new)
            e = jnp.exp(s - m_new)
            l = l * alpha + jnp.sum(e, axis=-1, keepdims=True)
            acc = acc * alpha + jnp.dot(e.astype(BF16), c16, preferred_element_type=F32)
            m = m_new
        l_scr[...] = l
        acc_scr[...] = acc
        m_scr[...] = m

    c16s, pt16s = [], []
    for g in range(pg // sub):
        c16s.append(jnp.concatenate([c_refs[t][...].astype(BF16) for t in range(g * sub, (g + 1) * sub)], axis=0))
        pt16s.append(jnp.concatenate([p_refs[t][...].astype(BF16) for t in range(g * sub, (g + 1) * sub)], axis=1))
    attend(c16s, pt16s, None)

    @pl.when(j == nj - 1)
    def _():
        nnew = cnew_ref.shape[0]
        rows = lax.broadcasted_iota(jnp.int32, (nrow, nnew), 0) // H_A
        cols = lax.broadcasted_iota(jnp.int32, (nrow, nnew), 1)
        attend([cnew_ref[...].astype(BF16)], [pnew_ref[...].astype(BF16)], cols <= rows)
        o_lat = acc_scr[...] / l_scr[...]
        hi, lo = _split2(o_lat)
        o_ref[...] = (jnp.dot(hi, wuv_ref[...], preferred_element_type=F32)
                      + jnp.dot(lo, wuv_ref[...], preferred_element_type=F32))


def _s_attn_call(page_table, qn_blk, qp_blk, cnew, pnew_t, wkt16, wuv16, cache_ckv, cache_kpe_t, layer):
    bd, nrow, _ = qn_blk.shape
    n_pages = page_table.shape[1]
    page = cache_ckv.shape[2]
    pg = min(PAGES_PER_STEP, n_pages)
    sub = min(PAGES_PER_SUB, pg)
    assert n_pages % pg == 0 and pg % sub == 0
    nnew = cnew.shape[1]
    grid = (bd, n_pages // pg)
    seq3 = lambda b, j, pt: (b, 0, 0)
    const2 = lambda b, j, pt: (0, 0)

    def page_map(t):
        return lambda b, j, pt: (layer, pt[b, j * pg + t], 0, 0)

    in_specs = [pl.BlockSpec((None, nrow, W_A), seq3), pl.BlockSpec((None, nrow, ROPE), seq3),
                pl.BlockSpec((None, nnew, KV_LORA), seq3), pl.BlockSpec((None, ROPE, nnew), seq3),
                pl.BlockSpec(wkt16.shape, const2), pl.BlockSpec(wuv16.shape, const2)]
    in_specs += [pl.BlockSpec((None, None, page, KV_LORA), page_map(t)) for t in range(pg)]
    in_specs += [pl.BlockSpec((None, None, ROPE, page), page_map(t)) for t in range(pg)]
    gs = pltpu.PrefetchScalarGridSpec(
        num_scalar_prefetch=1, grid=grid, in_specs=in_specs,
        out_specs=pl.BlockSpec((None, nrow, W_A), seq3),
        scratch_shapes=[pltpu.VMEM((nrow, 1), F32), pltpu.VMEM((nrow, 1), F32), pltpu.VMEM((nrow, KV_LORA), F32)])
    return pl.pallas_call(
        functools.partial(_s_attn_body, pg=pg, sub=sub), grid_spec=gs,
        out_shape=jax.ShapeDtypeStruct((bd, nrow, W_A), F32),
        compiler_params=_cparams(("arbitrary", "arbitrary")), name="s_attn",
    )(page_table, qn_blk, qp_blk, cnew, pnew_t, wkt16, wuv16,
      *([cache_ckv] * pg), *([cache_kpe_t] * pg))


def _rwkv_body(r_ref, lw_ref, km_ref, v_ref, kk_ref, kka_ref, gb_ref, s0_ref, rk_ref, lnw_ref, lnb_ref,
               o_ref, sout_ref, s_scr, *, c, nb):
    ci = pl.program_id(1)
    nc = pl.num_programs(1)
    g4 = RW_GROUP
    items = [(s, g) for s in range(nb) for g in range(RW_NGROUPS)]

    rr = lax.broadcasted_iota(jnp.int32, (g4, g4), 0) // N_B
    cc = lax.broadcasted_iota(jnp.int32, (g4, g4), 1) // N_B
    blockmask = rr == cc
    ones_blk = blockmask.astype(BF16)
    t_idx = lax.broadcasted_iota(jnp.int32, (c, g4), 0)
    i_idx = lax.broadcasted_iota(jnp.int32, (c, g4), 1) % N_B
    low_strict = i_idx < t_idx
    low_incl = i_idx <= t_idx
    eye = (i_idx == t_idx).astype(F32)

    def bd_rows(y):
        y16 = y.astype(BF16)
        if c == N_B:
            t = jnp.concatenate([y16] * 4, axis=0)
        else:
            pad = jnp.zeros((N_B - c, g4), BF16)
            t = jnp.concatenate([y16, pad] * 4, axis=0)
        return jnp.where(blockmask, t, jnp.zeros_like(t))

    def mmh(x, y):
        return jnp.dot(x.astype(BF16), bd_rows(y), preferred_element_type=F32)

    @pl.when(ci == 0)
    def _():
        s_scr[...] = jnp.zeros(s_scr.shape, F32)
        for s in range(nb):
            for h in range(H_B):
                g, jh = divmod(h, 4)
                s_scr[s, g, jh * N_B:(jh + 1) * N_B, jh * N_B:(jh + 1) * N_B] = s0_ref[s, h]

    tri = (lax.broadcasted_iota(jnp.int32, (c, c), 1) <= lax.broadcasted_iota(jnp.int32, (c, c), 0)).astype(BF16)
    g_seq = []
    for s in range(nb):
        g_seq.append(sum(jnp.dot(tri, part, preferred_element_type=F32) for part in _split3(lw_ref[s])))

    st = []
    for (s, g) in items:
        sl = slice(g * g4, (g + 1) * g4)
        lw = lw_ref[s, :, sl]
        gc = g_seq[s][:, sl]
        big = jnp.exp(gc)
        ginv = jnp.exp(-gc)
        at = -kk_ref[s, :, sl] * jnp.exp(gc - lw)
        bt = kka_ref[s, :, sl] * ginv
        kt = km_ref[s, :, sl] * ginv
        rt = r_ref[s, :, sl] * big
        s_old = s_scr[s, g]
        st.append(dict(sl=sl, s=s, g=g, bt=bt, kt=kt, g_last=big[c - 1:c, :], s_old=s_old,
                       lhs2=jnp.concatenate([at, rt], axis=0).astype(BF16), v=v_ref[s, :, sl]))

    for d in st:
        x1 = _nt(d["lhs2"], bd_rows(d["bt"]))
        x2 = _nt(d["lhs2"], bd_rows(d["kt"]))
        d["x3"] = _nt(d["lhs2"], d["s_old"].astype(BF16))
        d["a_ab"] = jnp.where(low_strict, x1[:c], 0.0)
        d["a_rb"] = jnp.where(low_incl, x1[c:], 0.0)
        d["a_ak"] = jnp.where(low_strict, x2[:c], 0.0)
        d["a_rk"] = jnp.where(low_incl, x2[c:], 0.0)
    for d in st:
        x4 = jnp.dot(jnp.concatenate([d["a_ak"], d["a_rk"]], axis=0).astype(BF16), bd_rows(d["v"]),
                     preferred_element_type=F32)
        d["b"] = x4[:c] + d["x3"][:c]
        d["ypart"] = x4[c:] + d["x3"][c:]

    if c <= 16:
        for d in st:
            d["tinv"] = eye + d["a_ab"]
            d["pw"] = d["a_ab"]
        n = 1
        while 2 * n < c:
            for d in st:
                d["pw"] = mmh(d["pw"], d["pw"])
            for d in st:
                d["tinv"] = mmh(d["tinv"], eye + d["pw"])
            n *= 2
        for d in st:
            d["u"] = mmh(d["tinv"], d["b"])
    else:
        same16 = (i_idx // 16) == (t_idx // 16)
        for d in st:
            dg = jnp.where(same16, d["a_ab"], 0.0)
            d["lo"] = d["a_ab"] - dg
            d["td"] = eye + dg
            d["pw"] = dg
        for _ in range(3):
            for d in st:
                d["pw"] = mmh(d["pw"], d["pw"])
            for d in st:
                d["td"] = mmh(d["td"], eye + d["pw"])
        for d in st:
            d["nn"] = mmh(d["td"], d["lo"])
            d["w"] = mmh(d["td"], d["b"])
        for d in st:
            d["n2"] = mmh(d["nn"], d["nn"])
        for d in st:
            d["w2"] = d["w"] + mmh(d["n2"], d["w"])
        for d in st:
            d["u"] = d["w2"] + mmh(d["nn"], d["w2"])

    for d in st:
        d["y"] = d["ypart"] + mmh(d["a_rb"], d["u"])
        lhs_t = jnp.concatenate([d["v"], d["u"]], axis=0).astype(BF16)
        rhs_t = jnp.concatenate([d["kt"] * d["g_last"], d["bt"] * d["g_last"]], axis=0).astype(BF16)
        upd = lax.dot_general(lhs_t, rhs_t, (((0,), (0,)), ((), ())), preferred_element_type=F32)
        s_scr[d["s"], d["g"]] = d["s_old"] * d["g_last"] + jnp.where(blockmask, upd, 0.0)

    def head_sums(xs):
        parts = []
        for x in xs:
            parts.extend(_split2(x))
        res = jnp.dot(jnp.concatenate(parts, axis=0), ones_blk, preferred_element_type=F32)
        return [res[(2 * k) * c:(2 * k + 1) * c] + res[(2 * k + 1) * c:(2 * k + 2) * c] for k in range(len(xs))]

    rkr = [r_ref[d["s"], :, d["sl"]] * km_ref[d["s"], :, d["sl"]] * rk_ref[:, d["sl"]] for d in st]
    sums = head_sums([d["y"] for d in st] + rkr)
    devs = [d["y"] - sums[k] * (1.0 / N_B) for k, d in enumerate(st)]
    var_sums = head_sums([dv * dv for dv in devs])
    for k, d in enumerate(st):
        sl = d["sl"]
        yn = devs[k] * lax.rsqrt(var_sums[k] * (1.0 / N_B) + LNX_EPS) * lnw_ref[:, sl] + lnb_ref[:, sl]
        bonus = sums[len(st) + k] * d["v"]
        gb = gb_ref[d["s"], :, sl]
        o_ref[d["s"], :, sl] = (yn + bonus) * (gb * jax.nn.sigmoid(gb))

    @pl.when(ci == nc - 1)
    def _():
        for s in range(nb):
            for h in range(H_B):
                g, jh = divmod(h, 4)
                sout_ref[s, h] = s_scr[s, g, jh * N_B:(jh + 1) * N_B, jh * N_B:(jh + 1) * N_B]


def _rwkv_call(r, lw, km, v, kk, kka, gb, s0, rk, lnw, lnb, *, c, nb):
    nseq, t, _ = r.shape
    nc = t // c
    assert nseq % nb == 0 and t % c == 0
    tok = pl.BlockSpec((nb, c, W_B), lambda b, i: (b, i, 0))
    vec = pl.BlockSpec((1, W_B), lambda b, i: (0, 0))
    st = pl.BlockSpec((nb, H_B, N_B, N_B), lambda b, i: (b, 0, 0, 0))
    return pl.pallas_call(
        functools.partial(_rwkv_body, c=c, nb=nb), grid=(nseq // nb, nc),
        in_specs=[tok] * 7 + [st, vec, vec, vec],
        out_specs=[tok, st],
        out_shape=[jax.ShapeDtypeStruct((nseq, t, W_B), F32), jax.ShapeDtypeStruct((nseq, H_B, N_B, N_B), F32)],
        scratch_shapes=[pltpu.VMEM((nb, RW_NGROUPS, RW_GROUP, RW_GROUP), F32)],
        compiler_params=_cparams(("arbitrary", "arbitrary")), name=f"rwkv_c{c}",
    )(r, lw, km, v, kk, kka, gb, s0, rk, lnw, lnb)


def _back_body(x_ref, oa_ref, ga_ref, ob_ref, p_ref, woa_ref, wob_ref, wple_ref, pn_ref, wg_ref, y_ref):
    ga = ga_ref[...]
    mixed_a = oa_ref[...] * (ga * jax.nn.sigmoid(ga))
    x1 = (x_ref[...] + jnp.dot(mixed_a.astype(BF16), woa_ref[...], preferred_element_type=F32)
          + jnp.dot(ob_ref[...].astype(BF16), wob_ref[...], preferred_element_type=F32))
    xg = _rms(x1, pn_ref[...], x1.shape[-1])
    gate = jax.nn.sigmoid(jnp.dot(xg.astype(BF16), wg_ref[...], preferred_element_type=F32))
    y_ref[...] = x1 + jnp.dot(p_ref[...].astype(BF16), wple_ref[...], preferred_element_type=F32) * gate


def _back_call(x2d, oa, ga, ob, p2d, wts):
    n, d = x2d.shape
    tm = min(BACK_TM, n)
    row = lambda i: (i, 0)
    const2 = lambda i: (0, 0)
    ins = [x2d, oa, ga, ob, p2d]
    return pl.pallas_call(
        _back_body, grid=(n // tm,),
        in_specs=[pl.BlockSpec((tm, a.shape[1]), row) for a in ins] + [pl.BlockSpec(w.shape, const2) for w in wts],
        out_specs=pl.BlockSpec((tm, d), row),
        out_shape=jax.ShapeDtypeStruct((n, d), F32),
        compiler_params=_cparams(("arbitrary",)), name="back",
    )(*ins, *wts)


def _pad_cols(a, width):
    return jnp.pad(a, ((0, 0), (0, width - a.shape[1])))


def _row(vec):
    return vec.reshape(1, -1).astype(F32)


def _rope_tabs(pos):
    inv = ROPE_BASE ** (-jnp.arange(0, ROPE, 2, dtype=F32) / ROPE)
    ang = pos.astype(F32)[:, None] * inv[None, :]
    cos, sin = jnp.cos(ang), jnp.sin(ang)
    t = pos.shape[0]
    z16 = jnp.zeros((t, ROPE // 2), F32)
    c = jnp.concatenate([cos, cos, jnp.ones((t, NOPE), F32), jnp.zeros((t, LANES - ROPE - NOPE), F32)], axis=1)
    s1 = _pad_cols(jnp.concatenate([-sin, z16], axis=1), LANES)
    s2 = _pad_cols(jnp.concatenate([z16, sin], axis=1), LANES)
    return jnp.stack([c, s1, s2])


def _layer_weights(i, norm_in, w_in, mu_shift, q_a_norm, w_uq, kv_a_norm, w_ukv, g_q_nope, g_q_pe, g_k_nope,
                   g_k_pe, w_decay0, w_decay_up, a0, w_iclr_up, k_k, k_a):
    w = w_in[i]
    c_q, c_kv, c_kpe = 0, Q_LORA, Q_LORA + KV_LORA
    c_ga = c_kpe + ROPE
    c_gb = c_ga + W_A
    c_sh = c_gb + W_B
    c_xw = c_sh + 3 * W_B
    c_xa = c_xw + DECAY_LORA
    w_perm = jnp.concatenate([
        w[:, c_q:c_kpe], _pad_cols(w[:, c_kpe:c_ga], LANES), w[:, c_ga:c_sh], w[:, c_sh:c_xw],
        _pad_cols(w[:, c_xw:c_xa], LANES), _pad_cols(w[:, c_xa:], LANES)], axis=1).astype(BF16)
    mu = mu_shift[i][None, :]
    mu_perm = jnp.concatenate([mu[:, :3 * W_B], _pad_cols(mu[:, 3 * W_B:3 * W_B + DECAY_LORA], LANES),
                               _pad_cols(mu[:, 3 * W_B + DECAY_LORA:], LANES)], axis=1)
    uq = w_uq[i]
    uq_cat = jnp.concatenate([uq[..., NOPE:], uq[..., :NOPE],
                              jnp.zeros((Q_LORA, H_A, LANES - NOPE - ROPE), F32)], axis=-1)
    uq_cat = uq_cat.reshape(Q_LORA, HCAT).astype(BF16)
    ukv = w_ukv[i]
    uk = ukv[..., :NOPE]
    uk_cat = jnp.concatenate([jnp.zeros((KV_LORA, H_A, ROPE), F32), uk,
                              jnp.zeros((KV_LORA, H_A, LANES - NOPE - ROPE), F32)], axis=-1)
    uk_cat = uk_cat.reshape(KV_LORA, HCAT).astype(BF16)
    uv = ukv[..., NOPE:].reshape(KV_LORA, W_A).astype(BF16)
    ukt = jnp.transpose(uk.reshape(KV_LORA, H_A * NOPE)).astype(BF16)
    zpad = jnp.zeros((LANES - NOPE - ROPE,), F32)
    gq_prompt = jnp.concatenate([g_q_pe[i], g_q_nope[i], zpad]) * SCALE
    gq_sample = jnp.concatenate([g_q_pe[i], g_q_nope[i] * g_k_nope[i], zpad]) * SCALE
    gk = jnp.concatenate([jnp.zeros((ROPE,), F32), g_k_nope[i], zpad])
    gkp = jnp.concatenate([g_k_pe[i], jnp.zeros((LANES - ROPE,), F32)])
    wdu = jnp.pad(w_decay_up[i], ((0, LANES - DECAY_LORA), (0, 0))).astype(BF16)
    wiu = jnp.pad(w_iclr_up[i], ((0, LANES - ICLR_LORA), (0, 0))).astype(BF16)

    def front_wts(gq):
        return [_row(norm_in[i]), w_perm, mu_perm, _row(q_a_norm[i]), uq_cat, _row(kv_a_norm[i]), uk_cat, uv,
                _row(gq), _row(gk), _row(gkp), _row(w_decay0[i]), wdu, _row(a0[i]), wiu, _row(k_k[i]), _row(k_a[i])]

    return front_wts(gq_prompt), front_wts(gq_sample), w_perm[:, P_SH:], ukt, uv


def kernel(x_prompt, x_sample, p_prompt, p_sample, cache_ckv, cache_kpe, state_wkv, state_shift, page_table,
           norm_in, w_in, mu_shift, q_a_norm, w_uq, kv_a_norm, w_ukv, g_q_nope, g_q_pe, g_k_nope, g_k_pe,
           w_decay0, w_decay_up, a0, w_iclr_up, k_k, k_a, r_k, lnx_w, lnx_b, w_out, w_ple, ple_norm, w_ple_gate):
    b, t, d = x_prompt.shape
    bd, tq, _ = x_sample.shape
    depth = w_in.shape[0]
    page = cache_ckv.shape[2]
    past_len = page_table.shape[1] * page
    tabs_p = _rope_tabs(jnp.arange(t))
    tabs_s = jnp.tile(_rope_tabs(past_len + jnp.arange(tq)), (1, bd, 1))
    cache_kpe_t = jnp.swapaxes(cache_kpe, 2, 3)
    y_p = x_prompt.reshape(b * t, d)
    y_s = x_sample.reshape(bd * tq, d)
    outs = [[] for _ in range(8)]
    for i in range(depth):
        wts_p, wts_s, w_shift16, ukt16, uv16 = _layer_weights(
            i, norm_in, w_in, mu_shift, q_a_norm, w_uq, kv_a_norm, w_ukv, g_q_nope, g_q_pe, g_k_nope, g_k_pe,
            w_decay0, w_decay_up, a0, w_iclr_up, k_k, k_a)
        rw_vecs = [_row(r_k[i]), _row(lnx_w[i]), _row(lnx_b[i])]
        wo = w_out[i].astype(BF16)
        back_wts = [wo[:W_A], wo[W_A:], w_ple[i].astype(BF16), _row(ple_norm[i]), w_ple_gate[i].astype(BF16)]

        (qcat, kcat, v16, ckv, kpe, ga, gb, r, lw, km, vb, kkn, kka, xl) = _front_call(
            y_p.reshape(b, t, d), jnp.zeros((1, N_SH), F32), tabs_p, wts_p, sample=False)
        o_a = _p_attn_call(qcat, kcat, v16, nseq=b, seq_len=t)
        seq3 = lambda a: a.reshape(b, t, W_B)
        o_b, s_p = _rwkv_call(seq3(r), seq3(lw), seq3(km), seq3(vb), seq3(kkn), seq3(kka), seq3(gb),
                              jnp.zeros((b, H_B, N_B, N_B), F32), *rw_vecs, c=min(RW_CHUNK, t), nb=4)
        y_p = _back_call(y_p, o_a, ga, o_b.reshape(b * t, W_B), p_prompt[i].reshape(b * t, -1), back_wts)
        outs[0].append(ckv.reshape(b, t, KV_LORA)); outs[2].append(kpe.reshape(b, t, ROPE))
        outs[4].append(s_p); outs[6].append(xl)

        prev_proj = _matmul_call(state_shift[i], w_shift16)
        prev0 = jnp.pad(prev_proj[:, None, :], ((0, 0), (0, tq - 1), (0, 0))).reshape(bd * tq, N_SH)
        (qcat, _, _, ckv, kpe, ga, gb, r, lw, km, vb, kkn, kka, xl) = _front_call(
            y_s.reshape(bd, tq, d), prev0, tabs_s, wts_s, sample=True)
        q4 = qcat.reshape(bd, tq, H_A, LANES)
        qp_blk = q4[..., :ROPE].reshape(bd, tq * H_A, ROPE)
        eye_h = jnp.eye(H_A, dtype=BF16)
        qn_blk = (q4[..., ROPE:ROPE + NOPE][:, :, :, None, :] * eye_h[None, None, :, :, None]
                  ).reshape(bd, tq * H_A, W_A)
        npad = LANES
        cnew = jnp.pad(ckv.reshape(bd, tq, KV_LORA), ((0, 0), (0, npad - tq), (0, 0)))
        pnew_t = jnp.swapaxes(jnp.pad(kpe.reshape(bd, tq, ROPE), ((0, 0), (0, npad - tq), (0, 0))), 1, 2)
        o_full = _s_attn_call(page_table, qn_blk, qp_blk, cnew, pnew_t, ukt16, uv16, cache_ckv, cache_kpe_t, i)
        o5 = o_full.reshape(bd, tq, H_A, H_A, V_DIM)
        o_a = jnp.einsum('bqhhd->bqhd', o5).reshape(bd * tq, W_A)
        cpad = 16

        def pad_tok(a):
            return jnp.pad(a.reshape(bd, tq, -1), ((0, 0), (0, cpad - tq), (0, 0)))

        o_b16, s_s = _rwkv_call(pad_tok(r), pad_tok(lw), pad_tok(km), pad_tok(vb), pad_tok(kkn), pad_tok(kka),
                                pad_tok(gb), state_wkv[i], *rw_vecs, c=cpad, nb=4)
        o_b = o_b16[:, :tq].reshape(bd * tq, W_B)
        y_s = _back_call(y_s, o_a, ga, o_b, p_sample[i].reshape(bd * tq, -1), back_wts)
        outs[1].append(ckv.reshape(bd, tq, KV_LORA)); outs[3].append(kpe.reshape(bd, tq, ROPE))
        outs[5].append(s_s); outs[7].append(xl)
    st = [jnp.stack(o) for o in outs]
    return (y_p.reshape(b, t, d), y_s.reshape(bd, tq, d), st[0], st[1], st[2], st[3], st[4], st[5], st[6], st[7])
```

```python
import functools

import jax
import jax.numpy as jnp
from jax import lax
from jax.experimental import pallas as pl
from jax.experimental.pallas import tpu as pltpu

F32 = jnp.float32
BF16 = jnp.bfloat16

LANES = 128
H_A = 8
NOPE = 64
ROPE = 32
V_DIM = 64
Q_LORA = 384
KV_LORA = 256
ROPE_BASE = 10000.0
SCALE = (NOPE + ROPE) ** -0.5
H_B = 8
N_B = 64
W_A = H_A * V_DIM
W_B = H_B * N_B
DECAY_LORA = 64
ICLR_LORA = 64
LNX_EPS = 64e-5
EPS = 1e-6
NEG = -1e30

P_Q = 0
P_KV = P_Q + Q_LORA
P_KPE = P_KV + KV_LORA
P_GA = P_KPE + LANES
P_GB = P_GA + W_A
P_SH = P_GB + W_B
N_SH = 3 * W_B + 2 * LANES
P_TOT = P_SH + N_SH

HCAT = H_A * LANES
RW_CHUNK = 64
RW_GROUP = 4 * N_B
RW_NGROUPS = W_B // RW_GROUP
FRONT_TM = 256
BACK_TM = 512
ATTN_T = 512
PAGES_PER_STEP = 16
PAGES_PER_SUB = 4
VMEM_LIMIT = 56 * 1024 * 1024


def _cparams(sem):
    return pltpu.CompilerParams(dimension_semantics=sem, vmem_limit_bytes=VMEM_LIMIT)


def _lane_iota(n=LANES):
    return lax.broadcasted_iota(jnp.int32, (1, n), 1)


def _rms(x, g, n):
    ms = jnp.sum(x * x, axis=-1, keepdims=True) * (1.0 / n)
    return x * lax.rsqrt(ms + EPS) * g


def _rope128(n, c, s1, s2):
    return n * c + pltpu.roll(n, LANES - ROPE // 2, axis=1) * s1 + pltpu.roll(n, ROPE // 2, axis=1) * s2


def _split2(x):
    hi = x.astype(BF16)
    lo = (x - hi.astype(F32)).astype(BF16)
    return hi, lo


def _split3(x):
    hi = x.astype(BF16)
    r1 = x - hi.astype(F32)
    mid = r1.astype(BF16)
    lo = (r1 - mid.astype(F32)).astype(BF16)
    return hi, mid, lo


def _nt(x16, y16):
    return lax.dot_general(x16, y16, (((1,), (1,)), ((), ())), preferred_element_type=F32)


def _front_body(x_ref, xlast_ref, prev0_ref, tab_ref, norm_in_ref, w_in_ref, mu_ref, qan_ref, wuq_ref, kvan_ref,
                wuk_ref, wuv_ref, gq_ref, gk_ref, gkp_ref, wd0_ref, wdu_ref, a0_ref, wiu_ref, kk_ref_, ka_ref,
                qcat_ref, kcat_ref, v_ref, ckv_ref, kpe_ref, ga_ref, gb_ref,
                r_ref, lw_ref, km_ref, vb_ref, kkn_ref, kka_ref, xl_ref,
                cur_scr, *, tm, tiles_per_seq, seq_len, sample):
    i = pl.program_id(0)
    x = x_ref[...]
    xn = _rms(x, norm_in_ref[...], x.shape[-1])
    xl_ref[...] = _rms(xlast_ref[...], norm_in_ref[...], x.shape[-1])
    xn16 = xn.astype(BF16)

    def proj(lo, hi):
        return jnp.dot(xn16, w_in_ref[:, lo:hi], preferred_element_type=F32)

    c_tab, s1_tab, s2_tab = tab_ref[0], tab_ref[1], tab_ref[2]
    lane = _lane_iota()
    is_rope = lane < ROPE

    c_q = _rms(proj(P_Q, P_KV), qan_ref[...], Q_LORA)
    q = jnp.dot(c_q.astype(BF16), wuq_ref[...], preferred_element_type=F32)
    gq = gq_ref[...]
    for h in range(H_A):
        qb = q[:, h * LANES:(h + 1) * LANES]
        sq = qb * qb
        s_r = jnp.sum(jnp.where(is_rope, sq, 0.0), axis=-1, keepdims=True)
        s_n = jnp.sum(jnp.where(is_rope, 0.0, sq), axis=-1, keepdims=True)
        inv = jnp.where(is_rope, lax.rsqrt(s_r * (1.0 / ROPE) + EPS), lax.rsqrt(s_n * (1.0 / NOPE) + EPS))
        qn = qb * inv * gq
        qcat_ref[:, h * LANES:(h + 1) * LANES] = _rope128(qn, c_tab, s1_tab, s2_tab).astype(BF16)

    p_kv = proj(P_KV, P_GA)
    c_kv = _rms(p_kv[:, :KV_LORA], kvan_ref[...], KV_LORA)
    ckv_ref[...] = c_kv
    c_kv16 = c_kv.astype(BF16)
    v_ref[...] = jnp.dot(c_kv16, wuv_ref[...], preferred_element_type=F32).astype(BF16)
    kraw = jnp.dot(c_kv16, wuk_ref[...], preferred_element_type=F32)
    kp = p_kv[:, KV_LORA:]
    kp_n = kp * lax.rsqrt(jnp.sum(kp * kp, axis=-1, keepdims=True) * (1.0 / ROPE) + EPS) * gkp_ref[...]
    kp_r = _rope128(kp_n, c_tab, s1_tab, s2_tab)
    kpe_ref[...] = kp_r[:, :ROPE]
    gk = gk_ref[...]
    for h in range(H_A):
        kb = kraw[:, h * LANES:(h + 1) * LANES]
        s_n = jnp.sum(kb * kb, axis=-1, keepdims=True)
        kn = kb * lax.rsqrt(s_n * (1.0 / NOPE) + EPS) * gk
        kcat_ref[:, h * LANES:(h + 1) * LANES] = (kn + kp_r).astype(BF16)

    ga_ref[...] = proj(P_GA, P_GB)
    gb_ref[...] = proj(P_GB, P_SH)

    cur = proj(P_SH, P_TOT)
    if sample:
        cur_scr[8:8 + tm, :] = cur
        cur_scr[7:8, :] = jnp.zeros((1, N_SH), F32)
        row = lax.broadcasted_iota(jnp.int32, (tm, 1), 0)
        prev = jnp.where(row % seq_len == 0, prev0_ref[...], cur_scr[7:7 + tm, :])
    else:
        @pl.when(i % tiles_per_seq == 0)
        def _():
            cur_scr[7:8, :] = prev0_ref[...]
        cur_scr[8:8 + tm, :] = cur
        prev = cur_scr[7:7 + tm, :]
        cur_scr[7:8, :] = cur[tm - 1:tm, :]
    sh = cur + (prev - cur) * mu_ref[...]
    r = sh[:, 0:W_B]
    k = sh[:, W_B:2 * W_B]
    v = sh[:, 2 * W_B:3 * W_B]
    xw = sh[:, 3 * W_B:3 * W_B + LANES]
    xa = sh[:, 3 * W_B + LANES:3 * W_B + 2 * LANES]
    z = wd0_ref[...] + jnp.dot(jnp.tanh(xw).astype(BF16), wdu_ref[...], preferred_element_type=F32)
    nz = -z
    softplus = jnp.maximum(nz, 0.0) + jnp.log1p(jnp.exp(-jnp.abs(nz)))
    w_log = -softplus - 0.5
    lw_ref[...] = -jnp.exp(w_log)
    a = jax.nn.sigmoid(a0_ref[...] + jnp.dot(xa.astype(BF16), wiu_ref[...], preferred_element_type=F32))
    kk = k * kk_ref_[...]
    lo_half = lane < N_B
    for c in range(W_B // LANES):
        blk = kk[:, c * LANES:(c + 1) * LANES]
        sq = blk * blk
        s0 = jnp.sum(jnp.where(lo_half, sq, 0.0), axis=-1, keepdims=True)
        s1 = jnp.sum(jnp.where(lo_half, 0.0, sq), axis=-1, keepdims=True)
        den = jnp.where(lo_half, jnp.maximum(jnp.sqrt(s0), 1e-12), jnp.maximum(jnp.sqrt(s1), 1e-12))
        kkn = blk / den
        kkn_ref[:, c * LANES:(c + 1) * LANES] = kkn
        kka_ref[:, c * LANES:(c + 1) * LANES] = kkn * a[:, c * LANES:(c + 1) * LANES]
    r_ref[...] = r
    vb_ref[...] = v
    km_ref[...] = k * (1.0 + (a - 1.0) * ka_ref[...])


def _front_call(x3d, prev0, tabs, wts, *, sample):
    nseq, seq_len, d = x3d.shape
    x2d = x3d.reshape(nseq * seq_len, d)
    xlast = x3d[:, seq_len - 1, :]
    n = x2d.shape[0]
    if sample:
        tm = n
        tiles_per_seq = 1
    else:
        tm = FRONT_TM
        assert seq_len % tm == 0
        tiles_per_seq = seq_len // tm
    grid = (n // tm,)
    ttab = tabs.shape[1]
    tab_blocks = ttab // tm
    row = lambda i: (i, 0)
    const2 = lambda i: (0, 0)
    prev_spec = pl.BlockSpec((tm, N_SH), row) if sample else pl.BlockSpec((1, N_SH), const2)
    xl_shape = jax.ShapeDtypeStruct((nseq, d), F32)
    xl_spec = pl.BlockSpec((nseq, d), const2)
    scratch = [pltpu.VMEM((tm + 8, N_SH), F32)]
    in_specs = [pl.BlockSpec((tm, d), row), pl.BlockSpec((nseq, d), const2), prev_spec,
                pl.BlockSpec((3, tm, LANES), lambda i: (0, i % tab_blocks, 0))]
    in_specs += [pl.BlockSpec(w.shape, const2) for w in wts]
    out_shapes = [
        jax.ShapeDtypeStruct((n, HCAT), BF16), jax.ShapeDtypeStruct((n, HCAT), BF16),
        jax.ShapeDtypeStruct((n, W_A), BF16), jax.ShapeDtypeStruct((n, KV_LORA), F32),
        jax.ShapeDtypeStruct((n, ROPE), F32), jax.ShapeDtypeStruct((n, W_A), F32),
        jax.ShapeDtypeStruct((n, W_B), F32),
    ] + [jax.ShapeDtypeStruct((n, W_B), F32)] * 6 + [xl_shape]
    out_specs = [pl.BlockSpec((tm, s.shape[1]), row) for s in out_shapes[:-1]] + [xl_spec]
    body = functools.partial(_front_body, tm=tm, tiles_per_seq=tiles_per_seq, seq_len=seq_len, sample=sample)
    return pl.pallas_call(
        body, grid=grid, in_specs=in_specs, out_specs=out_specs, out_shape=out_shapes,
        scratch_shapes=scratch, compiler_params=_cparams(("arbitrary",)),
        name="front_sample" if sample else "front_prompt",
    )(x2d, xlast, prev0, tabs, *wts)


def _matmul_body(a_ref, b_ref, o_ref):
    o_ref[...] = jnp.dot(a_ref[...].astype(BF16), b_ref[...], preferred_element_type=F32)


def _matmul_call(a, b16):
    m, k = a.shape
    n = b16.shape[1]
    return pl.pallas_call(
        _matmul_body, grid=(1,),
        in_specs=[pl.BlockSpec((m, k), lambda i: (0, 0)), pl.BlockSpec((k, n), lambda i: (0, 0))],
        out_specs=pl.BlockSpec((m, n), lambda i: (0, 0)),
        out_shape=jax.ShapeDtypeStruct((m, n), F32),
        compiler_params=_cparams(("arbitrary",)), name="prev_proj",
    )(a, b16)


def _p_attn_body(q_ref, k_ref, v_ref, o_ref, *, tq):
    qi = pl.program_id(2)
    lane = _lane_iota()
    qs = [q_ref[:, hh * LANES:(hh + 1) * LANES] for hh in range(2)]

    def scores(j, hh):
        kb = k_ref[pl.ds(pl.multiple_of(j * tq, tq), tq), hh * LANES:(hh + 1) * LANES]
        return _nt(qs[hh], kb)

    def update(carry, s, vb):
        m, l, acc = carry
        m_new = jnp.maximum(m, jnp.max(s, axis=-1, keepdims=True))
        alpha = jnp.exp(m - m_new)
        e = jnp.exp(s - m_new)
        acc = acc * alpha + jnp.dot(e.astype(BF16), vb, preferred_element_type=F32)
        return m_new, l * alpha + jnp.sum(e, axis=-1, keepdims=True), acc

    def step(j, carry):
        vb = v_ref[pl.ds(pl.multiple_of(j * tq, tq), tq), :]
        ss = [scores(j, hh) for hh in range(2)]
        return tuple(update(carry[hh], ss[hh], vb) for hh in range(2))

    init = (jnp.full((tq, 1), NEG, F32), jnp.zeros((tq, 1), F32), jnp.zeros((tq, LANES), F32))
    carry = lax.fori_loop(0, qi, step, (init, init))
    rows = lax.broadcasted_iota(jnp.int32, (tq, tq), 0)
    cols = lax.broadcasted_iota(jnp.int32, (tq, tq), 1)
    causal = cols <= rows
    vb = v_ref[pl.ds(pl.multiple_of(qi * tq, tq), tq), :]
    ss = [jnp.where(causal, scores(qi, hh), NEG) for hh in range(2)]
    outs = []
    for hh in range(2):
        _, l, acc = update(carry[hh], ss[hh], vb)
        outs.append(acc / l)
    o_ref[...] = jnp.where(lane < V_DIM, outs[0], outs[1])


def _p_attn_call(qcat, kcat, v16, *, nseq, seq_len):
    n = qcat.shape[0]
    tq = min(ATTN_T, seq_len)
    nq = seq_len // tq
    grid = (nseq, H_A // 2, nq)
    return pl.pallas_call(
        functools.partial(_p_attn_body, tq=tq), grid=grid,
        in_specs=[pl.BlockSpec((tq, 2 * LANES), lambda b, h, i: (b * nq + i, h)),
                  pl.BlockSpec((seq_len, 2 * LANES), lambda b, h, i: (b, h)),
                  pl.BlockSpec((seq_len, LANES), lambda b, h, i: (b, h))],
        out_specs=pl.BlockSpec((tq, LANES), lambda b, h, i: (b * nq + i, h)),
        out_shape=jax.ShapeDtypeStruct((n, W_A), F32),
        compiler_params=_cparams(("arbitrary", "arbitrary", "arbitrary")), name="p_attn",
    )(qcat, kcat, v16)


def _s_attn_body(pt_ref, qn_ref, qp_ref, cnew_ref, pnew_ref, wkt_ref, wuv_ref, ckv_hbm, kpe_hbm, o_ref,
                 cbuf, pbuf, sem, m_scr, l_scr, acc_scr, *, pg, sub, layer):
    b = pl.program_id(0)
    j = pl.program_id(1)
    nj = pl.num_programs(1)
    n = b * nj + j
    total = pl.num_programs(0) * nj
    slot = n % 2
    nrow = qn_ref.shape[0]

    def page_copies(sl, t, idx):
        return (pltpu.make_async_copy(ckv_hbm.at[layer, idx], cbuf.at[sl, t], sem.at[0, sl]),
                pltpu.make_async_copy(kpe_hbm.at[layer, idx], pbuf.at[sl, t], sem.at[1, sl]))

    def start_group(bb, jj, sl):
        for t in range(pg):
            for cp in page_copies(sl, t, pt_ref[bb, jj * pg + t]):
                cp.start()

    def wait_group(sl):
        for t in range(pg):
            for cp in page_copies(sl, t, 0):
                cp.wait()

    @pl.when(n == 0)
    def _():
        start_group(b, j, slot)

    @pl.when(n + 1 < total)
    def _():
        start_group((n + 1) // nj, (n + 1) % nj, 1 - slot)

    wait_group(slot)

    @pl.when(j == 0)
    def _():
        m_scr[...] = jnp.full(m_scr.shape, NEG, F32)
        l_scr[...] = jnp.zeros(l_scr.shape, F32)
        acc_scr[...] = jnp.zeros(acc_scr.shape, F32)

    def attend(c16s, pt16s, mask):
        krts = [_nt(wkt_ref[...], c16) for c16 in c16s]
        s_list = []
        for krt, pt16 in zip(krts, pt16s):
            ssq = jnp.concatenate(
                [jnp.sum(jnp.square(krt[h * NOPE:(h + 1) * NOPE, :]), axis=0, keepdims=True) for h in range(H_A)],
                axis=0)
            rinv = lax.rsqrt(ssq * (1.0 / NOPE) + EPS)
            sraw = jnp.dot(qn_ref[...], krt.astype(BF16), preferred_element_type=F32)
            spe = jnp.dot(qp_ref[...], pt16, preferred_element_type=F32)
            s = sraw * jnp.concatenate([rinv] * (nrow // H_A), axis=0) + spe
            if mask is not None:
                s = jnp.where(mask, s, NEG)
            s_list.append(s)
        m, l, acc = m_scr[...], l_scr[...], acc_scr[...]
        for s, c16 in zip(s_list, c16s):
            m_new = jnp.maximum(m, jnp.max(s, axis=-1, keepdims=True))
            alpha = jnp.exp(m - m_new)
            e = jnp.exp(s - m_new)
            l = l * alpha + jnp.sum(e, axis=-1, keepdims=True)
            acc = acc * alpha + jnp.dot(e.astype(BF16), c16, preferred_element_type=F32)
            m = m_new
        l_scr[...] = l
        acc_scr[...] = acc
        m_scr[...] = m

    c16s, pt16s = [], []
    page = cbuf.shape[2]
    for g in range(pg // sub):
        c16s.append(cbuf[slot, g * sub:(g + 1) * sub].reshape(sub * page, KV_LORA).astype(BF16))
        pt16s.append(jnp.concatenate([pbuf[slot, t].astype(BF16) for t in range(g * sub, (g + 1) * sub)], axis=1))
    attend(c16s, pt16s, None)

    @pl.when(j == nj - 1)
    def _():
        nnew = cnew_ref.shape[0]
        rows = lax.broadcasted_iota(jnp.int32, (nrow, nnew), 0) // H_A
        cols = lax.broadcasted_iota(jnp.int32, (nrow, nnew), 1)
        attend([cnew_ref[...].astype(BF16)], [pnew_ref[...].astype(BF16)], cols <= rows)
        o_lat = acc_scr[...] / l_scr[...]
        hi, lo = _split2(o_lat)
        o_ref[...] = (jnp.dot(hi, wuv_ref[...], preferred_element_type=F32)
                      + jnp.dot(lo, wuv_ref[...], preferred_element_type=F32))


def _s_attn_call(page_table, qn_blk, qp_blk, cnew, pnew_t, wkt16, wuv16, cache_ckv, cache_kpe_t, layer):
    bd, nrow, _ = qn_blk.shape
    n_pages = page_table.shape[1]
    page = cache_ckv.shape[2]
    pg = min(PAGES_PER_STEP, n_pages)
    sub = min(PAGES_PER_SUB, pg)
    assert n_pages % pg == 0 and pg % sub == 0
    nnew = cnew.shape[1]
    grid = (bd, n_pages // pg)
    seq3 = lambda b, j, pt: (b, 0, 0)
    const2 = lambda b, j, pt: (0, 0)

    in_specs = [pl.BlockSpec((None, nrow, W_A), seq3), pl.BlockSpec((None, nrow, ROPE), seq3),
                pl.BlockSpec((None, nnew, KV_LORA), seq3), pl.BlockSpec((None, ROPE, nnew), seq3),
                pl.BlockSpec(wkt16.shape, const2), pl.BlockSpec(wuv16.shape, const2),
                pl.BlockSpec(memory_space=pl.ANY), pl.BlockSpec(memory_space=pl.ANY)]
    gs = pltpu.PrefetchScalarGridSpec(
        num_scalar_prefetch=1, grid=grid, in_specs=in_specs,
        out_specs=pl.BlockSpec((None, nrow, W_A), seq3),
        scratch_shapes=[pltpu.VMEM((2, pg, page, KV_LORA), F32), pltpu.VMEM((2, pg, ROPE, page), F32),
                        pltpu.SemaphoreType.DMA((2, 2)),
                        pltpu.VMEM((nrow, 1), F32), pltpu.VMEM((nrow, 1), F32), pltpu.VMEM((nrow, KV_LORA), F32)])
    return pl.pallas_call(
        functools.partial(_s_attn_body, pg=pg, sub=sub, layer=layer), grid_spec=gs,
        out_shape=jax.ShapeDtypeStruct((bd, nrow, W_A), F32),
        compiler_params=_cparams(("arbitrary", "arbitrary")), name="s_attn",
    )(page_table, qn_blk, qp_blk, cnew, pnew_t, wkt16, wuv16, cache_ckv, cache_kpe_t)


def _rwkv_body(r_ref, lw_ref, km_ref, v_ref, kk_ref, kka_ref, gb_ref, s0_ref, rk_ref, lnw_ref, lnb_ref,
               o_ref, sout_ref, s_scr, *, c, nb):
    ci = pl.program_id(1)
    nc = pl.num_programs(1)
    g4 = RW_GROUP
    items = [(s, g) for s in range(nb) for g in range(RW_NGROUPS)]

    rr = lax.broadcasted_iota(jnp.int32, (g4, g4), 0) // N_B
    cc = lax.broadcasted_iota(jnp.int32, (g4, g4), 1) // N_B
    blockmask = rr == cc
    ones_blk = blockmask.astype(BF16)
    t_idx = lax.broadcasted_iota(jnp.int32, (c, g4), 0)
    i_idx = lax.broadcasted_iota(jnp.int32, (c, g4), 1) % N_B
    low_strict = i_idx < t_idx
    low_incl = i_idx <= t_idx
    eye = (i_idx == t_idx).astype(F32)

    def bd_rows(y):
        y16 = y.astype(BF16)
        if c == N_B:
            t = jnp.concatenate([y16] * 4, axis=0)
        else:
            pad = jnp.zeros((N_B - c, g4), BF16)
            t = jnp.concatenate([y16, pad] * 4, axis=0)
        return jnp.where(blockmask, t, jnp.zeros_like(t))

    def mmh(x, y):
        return jnp.dot(x.astype(BF16), bd_rows(y), preferred_element_type=F32)

    @pl.when(ci == 0)
    def _():
        s_scr[...] = jnp.zeros(s_scr.shape, F32)
        for s in range(nb):
            for h in range(H_B):
                g, jh = divmod(h, 4)
                s_scr[s, g, jh * N_B:(jh + 1) * N_B, jh * N_B:(jh + 1) * N_B] = s0_ref[s, h]

    tri = (lax.broadcasted_iota(jnp.int32, (c, c), 1) <= lax.broadcasted_iota(jnp.int32, (c, c), 0)).astype(BF16)
    g_seq = []
    for s in range(nb):
        g_seq.append(sum(jnp.dot(tri, part, preferred_element_type=F32) for part in _split3(lw_ref[s])))

    st = []
    for (s, g) in items:
        sl = slice(g * g4, (g + 1) * g4)
        lw = lw_ref[s, :, sl]
        gc = g_seq[s][:, sl]
        big = jnp.exp(gc)
        ginv = jnp.exp(-gc)
        at = -kk_ref[s, :, sl] * jnp.exp(gc - lw)
        bt = kka_ref[s, :, sl] * ginv
        kt = km_ref[s, :, sl] * ginv
        rt = r_ref[s, :, sl] * big
        s_old = s_scr[s, g]
        st.append(dict(sl=sl, s=s, g=g, bt=bt, kt=kt, g_last=big[c - 1:c, :], s_old=s_old,
                       lhs2=jnp.concatenate([at, rt], axis=0).astype(BF16), v=v_ref[s, :, sl]))

    for d in st:
        x1 = _nt(d["lhs2"], bd_rows(d["bt"]))
        x2 = _nt(d["lhs2"], bd_rows(d["kt"]))
        d["x3"] = _nt(d["lhs2"], d["s_old"].astype(BF16))
        d["a_ab"] = jnp.where(low_strict, x1[:c], 0.0)
        d["a_rb"] = jnp.where(low_incl, x1[c:], 0.0)
        d["a_ak"] = jnp.where(low_strict, x2[:c], 0.0)
        d["a_rk"] = jnp.where(low_incl, x2[c:], 0.0)
    for d in st:
        x4 = jnp.dot(jnp.concatenate([d["a_ak"], d["a_rk"]], axis=0).astype(BF16), bd_rows(d["v"]),
                     preferred_element_type=F32)
        d["b"] = x4[:c] + d["x3"][:c]
        d["ypart"] = x4[c:] + d["x3"][c:]

    if c <= 16:
        for d in st:
            d["tinv"] = eye + d["a_ab"]
            d["pw"] = d["a_ab"]
        n = 1
        while 2 * n < c:
            for d in st:
                d["pw"] = mmh(d["pw"], d["pw"])
            for d in st:
                d["tinv"] = mmh(d["tinv"], eye + d["pw"])
            n *= 2
        for d in st:
            d["u"] = mmh(d["tinv"], d["b"])
    else:
        same16 = (i_idx // 16) == (t_idx // 16)
        for d in st:
            dg = jnp.where(same16, d["a_ab"], 0.0)
            d["lo"] = d["a_ab"] - dg
            d["td"] = eye + dg
            d["pw"] = dg
        for _ in range(3):
            for d in st:
                d["pw"] = mmh(d["pw"], d["pw"])
            for d in st:
                d["td"] = mmh(d["td"], eye + d["pw"])
        for d in st:
            d["nn"] = mmh(d["td"], d["lo"])
            d["w"] = mmh(d["td"], d["b"])
        for d in st:
            d["n2"] = mmh(d["nn"], d["nn"])
        for d in st:
            d["w2"] = d["w"] + mmh(d["n2"], d["w"])
        for d in st:
            d["u"] = d["w2"] + mmh(d["nn"], d["w2"])

    for d in st:
        d["y"] = d["ypart"] + mmh(d["a_rb"], d["u"])
        lhs_t = jnp.concatenate([d["v"], d["u"]], axis=0).astype(BF16)
        rhs_t = jnp.concatenate([d["kt"] * d["g_last"], d["bt"] * d["g_last"]], axis=0).astype(BF16)
        upd = lax.dot_general(lhs_t, rhs_t, (((0,), (0,)), ((), ())), preferred_element_type=F32)
        s_scr[d["s"], d["g"]] = d["s_old"] * d["g_last"] + jnp.where(blockmask, upd, 0.0)

    def head_sums(xs):
        parts = []
        for x in xs:
            parts.extend(_split2(x))
        res = jnp.dot(jnp.concatenate(parts, axis=0), ones_blk, preferred_element_type=F32)
        return [res[(2 * k) * c:(2 * k + 1) * c] + res[(2 * k + 1) * c:(2 * k + 2) * c] for k in range(len(xs))]

    rkr = [r_ref[d["s"], :, d["sl"]] * km_ref[d["s"], :, d["sl"]] * rk_ref[:, d["sl"]] for d in st]
    sums = head_sums([d["y"] for d in st] + rkr)
    devs = [d["y"] - sums[k] * (1.0 / N_B) for k, d in enumerate(st)]
    var_sums = head_sums([dv * dv for dv in devs])
    for k, d in enumerate(st):
        sl = d["sl"]
        yn = devs[k] * lax.rsqrt(var_sums[k] * (1.0 / N_B) + LNX_EPS) * lnw_ref[:, sl] + lnb_ref[:, sl]
        bonus = sums[len(st) + k] * d["v"]
        gb = gb_ref[d["s"], :, sl]
        o_ref[d["s"], :, sl] = (yn + bonus) * (gb * jax.nn.sigmoid(gb))

    @pl.when(ci == nc - 1)
    def _():
        for s in range(nb):
            for h in range(H_B):
                g, jh = divmod(h, 4)
                sout_ref[s, h] = s_scr[s, g, jh * N_B:(jh + 1) * N_B, jh * N_B:(jh + 1) * N_B]


def _rwkv_call(r, lw, km, v, kk, kka, gb, s0, rk, lnw, lnb, *, c, nb):
    nseq, t, _ = r.shape
    nc = t // c
    assert nseq % nb == 0 and t % c == 0
    tok = pl.BlockSpec((nb, c, W_B), lambda b, i: (b, i, 0))
    vec = pl.BlockSpec((1, W_B), lambda b, i: (0, 0))
    st = pl.BlockSpec((nb, H_B, N_B, N_B), lambda b, i: (b, 0, 0, 0))
    return pl.pallas_call(
        functools.partial(_rwkv_body, c=c, nb=nb), grid=(nseq // nb, nc),
        in_specs=[tok] * 7 + [st, vec, vec, vec],
        out_specs=[tok, st],
        out_shape=[jax.ShapeDtypeStruct((nseq, t, W_B), F32), jax.ShapeDtypeStruct((nseq, H_B, N_B, N_B), F32)],
        scratch_shapes=[pltpu.VMEM((nb, RW_NGROUPS, RW_GROUP, RW_GROUP), F32)],
        compiler_params=_cparams(("arbitrary", "arbitrary")), name=f"rwkv_c{c}",
    )(r, lw, km, v, kk, kka, gb, s0, rk, lnw, lnb)


def _back_body(x_ref, oa_ref, ga_ref, ob_ref, p_ref, woa_ref, wob_ref, wple_ref, pn_ref, wg_ref, y_ref):
    ga = ga_ref[...]
    mixed_a = oa_ref[...] * (ga * jax.nn.sigmoid(ga))
    x1 = (x_ref[...] + jnp.dot(mixed_a.astype(BF16), woa_ref[...], preferred_element_type=F32)
          + jnp.dot(ob_ref[...].astype(BF16), wob_ref[...], preferred_element_type=F32))
    xg = _rms(x1, pn_ref[...], x1.shape[-1])
    gate = jax.nn.sigmoid(jnp.dot(xg.astype(BF16), wg_ref[...], preferred_element_type=F32))
    y_ref[...] = x1 + jnp.dot(p_ref[...].astype(BF16), wple_ref[...], preferred_element_type=F32) * gate


def _back_call(x2d, oa, ga, ob, p2d, wts):
    n, d = x2d.shape
    tm = min(BACK_TM, n)
    row = lambda i: (i, 0)
    const2 = lambda i: (0, 0)
    ins = [x2d, oa, ga, ob, p2d]
    return pl.pallas_call(
        _back_body, grid=(n // tm,),
        in_specs=[pl.BlockSpec((tm, a.shape[1]), row) for a in ins] + [pl.BlockSpec(w.shape, const2) for w in wts],
        out_specs=pl.BlockSpec((tm, d), row),
        out_shape=jax.ShapeDtypeStruct((n, d), F32),
        compiler_params=_cparams(("arbitrary",)), name="back",
    )(*ins, *wts)


def _pad_cols(a, width):
    return jnp.pad(a, ((0, 0), (0, width - a.shape[1])))


def _row(vec):
    return vec.reshape(1, -1).astype(F32)


def _rope_tabs(pos):
    inv = ROPE_BASE ** (-jnp.arange(0, ROPE, 2, dtype=F32) / ROPE)
    ang = pos.astype(F32)[:, None] * inv[None, :]
    cos, sin = jnp.cos(ang), jnp.sin(ang)
    t = pos.shape[0]
    z16 = jnp.zeros((t, ROPE // 2), F32)
    c = jnp.concatenate([cos, cos, jnp.ones((t, NOPE), F32), jnp.zeros((t, LANES - ROPE - NOPE), F32)], axis=1)
    s1 = _pad_cols(jnp.concatenate([-sin, z16], axis=1), LANES)
    s2 = _pad_cols(jnp.concatenate([z16, sin], axis=1), LANES)
    return jnp.stack([c, s1, s2])


def _layer_weights(i, norm_in, w_in, mu_shift, q_a_norm, w_uq, kv_a_norm, w_ukv, g_q_nope, g_q_pe, g_k_nope,
                   g_k_pe, w_decay0, w_decay_up, a0, w_iclr_up, k_k, k_a):
    w = w_in[i]
    c_q, c_kv, c_kpe = 0, Q_LORA, Q_LORA + KV_LORA
    c_ga = c_kpe + ROPE
    c_gb = c_ga + W_A
    c_sh = c_gb + W_B
    c_xw = c_sh + 3 * W_B
    c_xa = c_xw + DECAY_LORA
    w_perm = jnp.concatenate([
        w[:, c_q:c_kpe], _pad_cols(w[:, c_kpe:c_ga], LANES), w[:, c_ga:c_sh], w[:, c_sh:c_xw],
        _pad_cols(w[:, c_xw:c_xa], LANES), _pad_cols(w[:, c_xa:], LANES)], axis=1).astype(BF16)
    mu = mu_shift[i][None, :]
    mu_perm = jnp.concatenate([mu[:, :3 * W_B], _pad_cols(mu[:, 3 * W_B:3 * W_B + DECAY_LORA], LANES),
                               _pad_cols(mu[:, 3 * W_B + DECAY_LORA:], LANES)], axis=1)
    uq = w_uq[i]
    uq_cat = jnp.concatenate([uq[..., NOPE:], uq[..., :NOPE],
                              jnp.zeros((Q_LORA, H_A, LANES - NOPE - ROPE), F32)], axis=-1)
    uq_cat = uq_cat.reshape(Q_LORA, HCAT).astype(BF16)
    ukv = w_ukv[i]
    uk = ukv[..., :NOPE]
    uk_cat = jnp.concatenate([jnp.zeros((KV_LORA, H_A, ROPE), F32), uk,
                              jnp.zeros((KV_LORA, H_A, LANES - NOPE - ROPE), F32)], axis=-1)
    uk_cat = uk_cat.reshape(KV_LORA, HCAT).astype(BF16)
    uv = ukv[..., NOPE:].reshape(KV_LORA, W_A).astype(BF16)
    ukt = jnp.transpose(uk.reshape(KV_LORA, H_A * NOPE)).astype(BF16)
    zpad = jnp.zeros((LANES - NOPE - ROPE,), F32)
    gq_prompt = jnp.concatenate([g_q_pe[i], g_q_nope[i], zpad]) * SCALE
    gq_sample = jnp.concatenate([g_q_pe[i], g_q_nope[i] * g_k_nope[i], zpad]) * SCALE
    gk = jnp.concatenate([jnp.zeros((ROPE,), F32), g_k_nope[i], zpad])
    gkp = jnp.concatenate([g_k_pe[i], jnp.zeros((LANES - ROPE,), F32)])
    wdu = jnp.pad(w_decay_up[i], ((0, LANES - DECAY_LORA), (0, 0))).astype(BF16)
    wiu = jnp.pad(w_iclr_up[i], ((0, LANES - ICLR_LORA), (0, 0))).astype(BF16)

    def front_wts(gq):
        return [_row(norm_in[i]), w_perm, mu_perm, _row(q_a_norm[i]), uq_cat, _row(kv_a_norm[i]), uk_cat, uv,
                _row(gq), _row(gk), _row(gkp), _row(w_decay0[i]), wdu, _row(a0[i]), wiu, _row(k_k[i]), _row(k_a[i])]

    return front_wts(gq_prompt), front_wts(gq_sample), w_perm[:, P_SH:], ukt, uv


def kernel(x_prompt, x_sample, p_prompt, p_sample, cache_ckv, cache_kpe, state_wkv, state_shift, page_table,
           norm_in, w_in, mu_shift, q_a_norm, w_uq, kv_a_norm, w_ukv, g_q_nope, g_q_pe, g_k_nope, g_k_pe,
           w_decay0, w_decay_up, a0, w_iclr_up, k_k, k_a, r_k, lnx_w, lnx_b, w_out, w_ple, ple_norm, w_ple_gate):
    b, t, d = x_prompt.shape
    bd, tq, _ = x_sample.shape
    depth = w_in.shape[0]
    page = cache_ckv.shape[2]
    past_len = page_table.shape[1] * page
    tabs_p = _rope_tabs(jnp.arange(t))
    tabs_s = jnp.tile(_rope_tabs(past_len + jnp.arange(tq)), (1, bd, 1))
    cache_kpe_t = jnp.swapaxes(cache_kpe, 2, 3)
    y_p = x_prompt.reshape(b * t, d)
    y_s = x_sample.reshape(bd * tq, d)
    outs = [[] for _ in range(8)]
    for i in range(depth):
        wts_p, wts_s, w_shift16, ukt16, uv16 = _layer_weights(
            i, norm_in, w_in, mu_shift, q_a_norm, w_uq, kv_a_norm, w_ukv, g_q_nope, g_q_pe, g_k_nope, g_k_pe,
            w_decay0, w_decay_up, a0, w_iclr_up, k_k, k_a)
        rw_vecs = [_row(r_k[i]), _row(lnx_w[i]), _row(lnx_b[i])]
        wo = w_out[i].astype(BF16)
        back_wts = [wo[:W_A], wo[W_A:], w_ple[i].astype(BF16), _row(ple_norm[i]), w_ple_gate[i].astype(BF16)]

        (qcat, kcat, v16, ckv, kpe, ga, gb, r, lw, km, vb, kkn, kka, xl) = _front_call(
            y_p.reshape(b, t, d), jnp.zeros((1, N_SH), F32), tabs_p, wts_p, sample=False)
        o_a = _p_attn_call(qcat, kcat, v16, nseq=b, seq_len=t)
        seq3 = lambda a: a.reshape(b, t, W_B)
        o_b, s_p = _rwkv_call(seq3(r), seq3(lw), seq3(km), seq3(vb), seq3(kkn), seq3(kka), seq3(gb),
                              jnp.zeros((b, H_B, N_B, N_B), F32), *rw_vecs, c=min(RW_CHUNK, t), nb=4)
        y_p = _back_call(y_p, o_a, ga, o_b.reshape(b * t, W_B), p_prompt[i].reshape(b * t, -1), back_wts)
        outs[0].append(ckv.reshape(b, t, KV_LORA)); outs[2].append(kpe.reshape(b, t, ROPE))
        outs[4].append(s_p); outs[6].append(xl)

        prev_proj = _matmul_call(state_shift[i], w_shift16)
        prev0 = jnp.pad(prev_proj[:, None, :], ((0, 0), (0, tq - 1), (0, 0))).reshape(bd * tq, N_SH)
        (qcat, _, _, ckv, kpe, ga, gb, r, lw, km, vb, kkn, kka, xl) = _front_call(
            y_s.reshape(bd, tq, d), prev0, tabs_s, wts_s, sample=True)
        q4 = qcat.reshape(bd, tq, H_A, LANES)
        qp_blk = q4[..., :ROPE].reshape(bd, tq * H_A, ROPE)
        eye_h = jnp.eye(H_A, dtype=BF16)
        qn_blk = (q4[..., ROPE:ROPE + NOPE][:, :, :, None, :] * eye_h[None, None, :, :, None]
                  ).reshape(bd, tq * H_A, W_A)
        npad = LANES
        cnew = jnp.pad(ckv.reshape(bd, tq, KV_LORA), ((0, 0), (0, npad - tq), (0, 0)))
        pnew_t = jnp.swapaxes(jnp.pad(kpe.reshape(bd, tq, ROPE), ((0, 0), (0, npad - tq), (0, 0))), 1, 2)
        o_full = _s_attn_call(page_table, qn_blk, qp_blk, cnew, pnew_t, ukt16, uv16, cache_ckv, cache_kpe_t, i)
        o5 = o_full.reshape(bd, tq, H_A, H_A, V_DIM)
        o_a = jnp.einsum('bqhhd->bqhd', o5).reshape(bd * tq, W_A)
        cpad = 16

        def pad_tok(a):
            return jnp.pad(a.reshape(bd, tq, -1), ((0, 0), (0, cpad - tq), (0, 0)))

        o_b16, s_s = _rwkv_call(pad_tok(r), pad_tok(lw), pad_tok(km), pad_tok(vb), pad_tok(kkn), pad_tok(kka),
                                pad_tok(gb), state_wkv[i], *rw_vecs, c=cpad, nb=4)
        o_b = o_b16[:, :tq].reshape(bd * tq, W_B)
        y_s = _back_call(y_s, o_a, ga, o_b, p_sample[i].reshape(bd * tq, -1), back_wts)
        outs[1].append(ckv.reshape(bd, tq, KV_LORA)); outs[3].append(kpe.reshape(bd, tq, ROPE))
        outs[5].append(s_s); outs[7].append(xl)
    st = [jnp.stack(o) for o in outs]
    return (y_p.reshape(b, t, d), y_s.reshape(bd, tq, d), st[0], st[1], st[2], st[3], st[4], st[5], st[6], st[7])
```

```python
import functools

import jax
import jax.numpy as jnp
from jax import lax
from jax.experimental import pallas as pl
from jax.experimental.pallas import tpu as pltpu

F32 = jnp.float32
BF16 = jnp.bfloat16

LANES = 128
H_A = 8
NOPE = 64
ROPE = 32
V_DIM = 64
Q_LORA = 384
KV_LORA = 256
ROPE_BASE = 10000.0
SCALE = (NOPE + ROPE) ** -0.5
H_B = 8
N_B = 64
W_A = H_A * V_DIM
W_B = H_B * N_B
DECAY_LORA = 64
ICLR_LORA = 64
LNX_EPS = 64e-5
EPS = 1e-6
NEG = -1e30

P_Q = 0
P_KV = P_Q + Q_LORA
P_KPE = P_KV + KV_LORA
P_GA = P_KPE + LANES
P_GB = P_GA + W_A
P_SH = P_GB + W_B
N_SH = 3 * W_B + 2 * LANES
P_TOT = P_SH + N_SH

HCAT = H_A * LANES
RW_CHUNK = 64
RW_GROUP = 4 * N_B
RW_NGROUPS = W_B // RW_GROUP
FRONT_TM = 512
FRONT_SUB = 256
BACK_TM = 512
ATTN_TQ = 512
ATTN_TK = 512
PAGES_PER_STEP = 32
PAGES_PER_SUB = 4
VMEM_LIMIT = 56 * 1024 * 1024


def _cparams(sem):
    return pltpu.CompilerParams(dimension_semantics=sem, vmem_limit_bytes=VMEM_LIMIT)


def _lane_iota(n=LANES):
    return lax.broadcasted_iota(jnp.int32, (1, n), 1)


def _rms(x, g, n):
    ms = jnp.sum(x * x, axis=-1, keepdims=True) * (1.0 / n)
    return x * lax.rsqrt(ms + EPS) * g


def _rope128(n, c, s1, s2):
    return n * c + pltpu.roll(n, LANES - ROPE // 2, axis=1) * s1 + pltpu.roll(n, ROPE // 2, axis=1) * s2


def _split2(x):
    hi = x.astype(BF16)
    lo = (x - hi.astype(F32)).astype(BF16)
    return hi, lo


def _split3(x):
    hi = x.astype(BF16)
    r1 = x - hi.astype(F32)
    mid = r1.astype(BF16)
    lo = (r1 - mid.astype(F32)).astype(BF16)
    return hi, mid, lo


def _nt(x16, y16):
    return lax.dot_general(x16, y16, (((1,), (1,)), ((), ())), preferred_element_type=F32)


def _front_body(x_ref, xlast_ref, prev0_ref, tab_ref, norm_in_ref, w_in_ref, mu_ref, qan_ref, wuq_ref, kvan_ref,
                wuk_ref, wuv_ref, gq_ref, gk_ref, gkp_ref, wd0_ref, wdu_ref, a0_ref, wiu_ref, kk_ref_, ka_ref,
                qcat_ref, kcat_ref, v_ref, ckv_ref, kpe_ref, ga_ref, gb_ref,
                r_ref, lw_ref, km_ref, vb_ref, kkn_ref, kka_ref, xl_ref,
                cur_scr, *, tm, sub_rows, tiles_per_seq, seq_len, sample):
    i = pl.program_id(0)
    d_model = x_ref.shape[-1]
    xl_ref[...] = _rms(xlast_ref[...], norm_in_ref[...], d_model)
    subs = [slice(r0, r0 + sub_rows) for r0 in range(0, tm, sub_rows)]

    if not sample:
        @pl.when(i % tiles_per_seq == 0)
        def _():
            cur_scr[7:8, :] = prev0_ref[...]
    else:
        cur_scr[7:8, :] = jnp.zeros((1, N_SH), F32)
    xn16s = []
    for rs in subs:
        xn16s.append(_rms(x_ref[rs, :], norm_in_ref[...], d_model).astype(BF16))
        cur_scr[8 + rs.start:8 + rs.stop, :] = jnp.dot(xn16s[-1], w_in_ref[:, P_SH:P_TOT],
                                                        preferred_element_type=F32)
    for rs, xn16 in zip(subs, xn16s):
        _front_rows(rs, xn16, prev0_ref, tab_ref, w_in_ref, mu_ref, qan_ref, wuq_ref, kvan_ref,
                    wuk_ref, wuv_ref, gq_ref, gk_ref, gkp_ref, wd0_ref, wdu_ref, a0_ref, wiu_ref, kk_ref_, ka_ref,
                    qcat_ref, kcat_ref, v_ref, ckv_ref, kpe_ref, ga_ref, gb_ref,
                    r_ref, lw_ref, km_ref, vb_ref, kkn_ref, kka_ref, cur_scr, seq_len=seq_len, sample=sample)
    if not sample:
        cur_scr[7:8, :] = cur_scr[7 + tm:8 + tm, :]


def _front_rows(rs, xn16, prev0_ref, tab_ref, w_in_ref, mu_ref, qan_ref, wuq_ref, kvan_ref,
                wuk_ref, wuv_ref, gq_ref, gk_ref, gkp_ref, wd0_ref, wdu_ref, a0_ref, wiu_ref, kk_ref_, ka_ref,
                qcat_ref, kcat_ref, v_ref, ckv_ref, kpe_ref, ga_ref, gb_ref,
                r_ref, lw_ref, km_ref, vb_ref, kkn_ref, kka_ref, cur_scr, *, seq_len, sample):
    nrows = rs.stop - rs.start

    def proj(lo, hi):
        return jnp.dot(xn16, w_in_ref[:, lo:hi], preferred_element_type=F32)

    c_tab, s1_tab, s2_tab = tab_ref[0, rs, :], tab_ref[1, rs, :], tab_ref[2, rs, :]
    lane = _lane_iota()
    is_rope = lane < ROPE

    c_q = _rms(proj(P_Q, P_KV), qan_ref[...], Q_LORA)
    q = jnp.dot(c_q.astype(BF16), wuq_ref[...], preferred_element_type=F32)
    gq = gq_ref[...]
    for h in range(H_A):
        qb = q[:, h * LANES:(h + 1) * LANES]
        sq = qb * qb
        s_r = jnp.sum(jnp.where(is_rope, sq, 0.0), axis=-1, keepdims=True)
        s_n = jnp.sum(jnp.where(is_rope, 0.0, sq), axis=-1, keepdims=True)
        inv = jnp.where(is_rope, lax.rsqrt(s_r * (1.0 / ROPE) + EPS), lax.rsqrt(s_n * (1.0 / NOPE) + EPS))
        qn = qb * inv * gq
        qcat_ref[rs, h * LANES:(h + 1) * LANES] = _rope128(qn, c_tab, s1_tab, s2_tab).astype(BF16)

    p_kv = proj(P_KV, P_GA)
    c_kv = _rms(p_kv[:, :KV_LORA], kvan_ref[...], KV_LORA)
    ckv_ref[rs, :] = c_kv
    c_kv16 = c_kv.astype(BF16)
    is_one = (_lane_iota(HCAT) % LANES) == V_DIM
    v_ref[rs, :] = jnp.where(is_one, 1.0, jnp.dot(c_kv16, wuv_ref[...], preferred_element_type=F32)).astype(BF16)
    kraw = jnp.dot(c_kv16, wuk_ref[...], preferred_element_type=F32)
    kp = p_kv[:, KV_LORA:]
    kp_n = kp * lax.rsqrt(jnp.sum(kp * kp, axis=-1, keepdims=True) * (1.0 / ROPE) + EPS) * gkp_ref[...]
    kp_r = _rope128(kp_n, c_tab, s1_tab, s2_tab)
    kpe_ref[rs, :] = kp_r[:, :ROPE]
    gk = gk_ref[...]
    for h in range(H_A):
        kb = kraw[:, h * LANES:(h + 1) * LANES]
        s_n = jnp.sum(kb * kb, axis=-1, keepdims=True)
        kn = kb * lax.rsqrt(s_n * (1.0 / NOPE) + EPS) * gk
        kcat_ref[rs, h * LANES:(h + 1) * LANES] = (kn + kp_r).astype(BF16)

    ga_ref[rs, :] = proj(P_GA, P_GB)
    gb_ref[rs, :] = proj(P_GB, P_SH)

    cur = cur_scr[8 + rs.start:8 + rs.stop, :]
    prev = cur_scr[7 + rs.start:7 + rs.stop, :]
    if sample:
        row = lax.broadcasted_iota(jnp.int32, (nrows, 1), 0) + rs.start
        prev = jnp.where(row % seq_len == 0, prev0_ref[rs, :], prev)
    sh = cur + (prev - cur) * mu_ref[...]
    r = sh[:, 0:W_B]
    k = sh[:, W_B:2 * W_B]
    v = sh[:, 2 * W_B:3 * W_B]
    xw = sh[:, 3 * W_B:3 * W_B + LANES]
    xa = sh[:, 3 * W_B + LANES:3 * W_B + 2 * LANES]
    z = wd0_ref[...] + jnp.dot(jnp.tanh(xw).astype(BF16), wdu_ref[...], preferred_element_type=F32)
    nz = -z
    softplus = jnp.maximum(nz, 0.0) + jnp.log1p(jnp.exp(-jnp.abs(nz)))
    w_log = -softplus - 0.5
    lw_ref[rs, :] = -jnp.exp(w_log)
    a = jax.nn.sigmoid(a0_ref[...] + jnp.dot(xa.astype(BF16), wiu_ref[...], preferred_element_type=F32))
    kk = k * kk_ref_[...]
    lo_half = lane < N_B
    for c in range(W_B // LANES):
        blk = kk[:, c * LANES:(c + 1) * LANES]
        sq = blk * blk
        s0 = jnp.sum(jnp.where(lo_half, sq, 0.0), axis=-1, keepdims=True)
        s1 = jnp.sum(jnp.where(lo_half, 0.0, sq), axis=-1, keepdims=True)
        den = jnp.where(lo_half, jnp.maximum(jnp.sqrt(s0), 1e-12), jnp.maximum(jnp.sqrt(s1), 1e-12))
        kkn = blk / den
        kkn_ref[rs, c * LANES:(c + 1) * LANES] = kkn
        kka_ref[rs, c * LANES:(c + 1) * LANES] = kkn * a[:, c * LANES:(c + 1) * LANES]
    r_ref[rs, :] = r
    vb_ref[rs, :] = v
    km_ref[rs, :] = k * (1.0 + (a - 1.0) * ka_ref[...])


def _front_call(x3d, prev0, tabs, wts, *, sample):
    nseq, seq_len, d = x3d.shape
    x2d = x3d.reshape(nseq * seq_len, d)
    xlast = x3d[:, seq_len - 1, :]
    n = x2d.shape[0]
    if sample:
        tm = n
        tiles_per_seq = 1
    else:
        tm = FRONT_TM
        assert seq_len % tm == 0
        tiles_per_seq = seq_len // tm
    grid = (n // tm,)
    ttab = tabs.shape[1]
    tab_blocks = ttab // tm
    row = lambda i: (i, 0)
    const2 = lambda i: (0, 0)
    prev_spec = pl.BlockSpec((tm, N_SH), row) if sample else pl.BlockSpec((1, N_SH), const2)
    xl_shape = jax.ShapeDtypeStruct((nseq, d), F32)
    xl_spec = pl.BlockSpec((nseq, d), const2)
    scratch = [pltpu.VMEM((tm + 8, N_SH), F32)]
    in_specs = [pl.BlockSpec((tm, d), row), pl.BlockSpec((nseq, d), const2), prev_spec,
                pl.BlockSpec((3, tm, LANES), lambda i: (0, i % tab_blocks, 0))]
    in_specs += [pl.BlockSpec(w.shape, const2, pipeline_mode=pl.Buffered(1)) for w in wts]
    out_shapes = [
        jax.ShapeDtypeStruct((n, HCAT), BF16), jax.ShapeDtypeStruct((n, HCAT), BF16),
        jax.ShapeDtypeStruct((n, HCAT), BF16), jax.ShapeDtypeStruct((n, KV_LORA), F32),
        jax.ShapeDtypeStruct((n, ROPE), F32), jax.ShapeDtypeStruct((n, W_A), F32),
        jax.ShapeDtypeStruct((n, W_B), F32),
    ] + [jax.ShapeDtypeStruct((n, W_B), F32)] * 6 + [xl_shape]
    out_specs = [pl.BlockSpec((tm, s.shape[1]), row) for s in out_shapes[:-1]] + [xl_spec]
    sub_rows = min(FRONT_SUB, tm)
    assert tm % sub_rows == 0
    body = functools.partial(_front_body, tm=tm, sub_rows=sub_rows, tiles_per_seq=tiles_per_seq, seq_len=seq_len,
                             sample=sample)
    return pl.pallas_call(
        body, grid=grid, in_specs=in_specs, out_specs=out_specs, out_shape=out_shapes,
        scratch_shapes=scratch, compiler_params=_cparams(("arbitrary",)),
        name="front_sample" if sample else "front_prompt",
    )(x2d, xlast, prev0, tabs, *wts)


def _matmul_body(a_ref, b_ref, o_ref):
    o_ref[...] = jnp.dot(a_ref[...].astype(BF16), b_ref[...], preferred_element_type=F32)


def _matmul_call(a, b16):
    m, k = a.shape
    n = b16.shape[1]
    return pl.pallas_call(
        _matmul_body, grid=(1,),
        in_specs=[pl.BlockSpec((m, k), lambda i: (0, 0)), pl.BlockSpec((k, n), lambda i: (0, 0))],
        out_specs=pl.BlockSpec((m, n), lambda i: (0, 0)),
        out_shape=jax.ShapeDtypeStruct((m, n), F32),
        compiler_params=_cparams(("arbitrary",)), name="prev_proj",
    )(a, b16)


def _p_attn_body(q_ref, k_ref, v_ref, o_ref, *, tq, tk):
    qi = pl.program_id(2)
    lane = _lane_iota()
    qs = [q_ref[:, hh * LANES:(hh + 1) * LANES] for hh in range(2)]
    nsub = tq // tk

    def scores(j, hh):
        kb = k_ref[pl.ds(pl.multiple_of(j * tk, tk), tk), hh * LANES:(hh + 1) * LANES]
        return _nt(qs[hh], kb)

    def values(j, hh):
        return v_ref[pl.ds(pl.multiple_of(j * tk, tk), tk), hh * LANES:(hh + 1) * LANES]

    def update(carry, s, vb):
        m, acc = carry
        m_new = jnp.maximum(m, jnp.max(s, axis=-1, keepdims=True))
        e16 = jnp.exp(s - m_new).astype(BF16)
        return m_new, acc * jnp.exp(m - m_new) + jnp.dot(e16, vb, preferred_element_type=F32)

    def step(j, carry):
        ss = [scores(j, hh) for hh in range(2)]
        return tuple(update(carry[hh], ss[hh], values(j, hh)) for hh in range(2))

    init = (jnp.full((tq, 1), NEG, F32), jnp.zeros((tq, LANES), F32))
    carry = lax.fori_loop(0, qi * nsub, step, (init, init))
    rows = lax.broadcasted_iota(jnp.int32, (tq, tk), 0)
    cols = lax.broadcasted_iota(jnp.int32, (tq, tk), 1)
    for d in range(nsub):
        j = qi * nsub + d
        causal = cols + d * tk <= rows
        ss = [jnp.where(causal, scores(j, hh), NEG) for hh in range(2)]
        carry = tuple(update(carry[hh], ss[hh], values(j, hh)) for hh in range(2))
    outs = [carry[hh][1] / carry[hh][1][:, V_DIM:V_DIM + 1] for hh in range(2)]
    o_ref[...] = jnp.where(lane < V_DIM, outs[0], pltpu.roll(outs[1], V_DIM, axis=1))


def _p_attn_call(qcat, kcat, v16, *, nseq, seq_len):
    n = qcat.shape[0]
    tq = min(ATTN_TQ, seq_len)
    tk = min(ATTN_TK, tq)
    assert seq_len % tq == 0 and tq % tk == 0
    nq = seq_len // tq
    grid = (nseq, H_A // 2, nq)
    return pl.pallas_call(
        functools.partial(_p_attn_body, tq=tq, tk=tk), grid=grid,
        in_specs=[pl.BlockSpec((tq, 2 * LANES), lambda b, h, i: (b * nq + i, h)),
                  pl.BlockSpec((seq_len, 2 * LANES), lambda b, h, i: (b, h)),
                  pl.BlockSpec((seq_len, 2 * LANES), lambda b, h, i: (b, h))],
        out_specs=pl.BlockSpec((tq, LANES), lambda b, h, i: (b * nq + i, h)),
        out_shape=jax.ShapeDtypeStruct((n, W_A), F32),
        compiler_params=_cparams(("arbitrary", "arbitrary", "arbitrary")), name="p_attn",
    )(qcat, kcat, v16)


def _s_attn_body(pt_ref, qn_ref, qp_ref, cnew_ref, pnew_ref, wkt_ref, wuv_ref, ckv_hbm, kpe_hbm, o_ref,
                 cbuf, pbuf, sem, m_scr, l_scr, acc_scr, *, pg, sub, layer):
    b = pl.program_id(0)
    j = pl.program_id(1)
    nj = pl.num_programs(1)
    n = b * nj + j
    total = pl.num_programs(0) * nj
    slot = n % 2
    nrow = qn_ref.shape[0]

    def page_copies(sl, t, idx):
        return (pltpu.make_async_copy(ckv_hbm.at[layer, idx], cbuf.at[sl, t], sem.at[0, sl]),
                pltpu.make_async_copy(kpe_hbm.at[layer, idx], pbuf.at[sl, t], sem.at[1, sl]))

    def start_group(bb, jj, sl):
        for t in range(pg):
            for cp in page_copies(sl, t, pt_ref[bb, jj * pg + t]):
                cp.start()

    def wait_group(sl):
        pltpu.make_async_copy(cbuf.at[sl], cbuf.at[sl], sem.at[0, sl]).wait()
        pltpu.make_async_copy(pbuf.at[sl], pbuf.at[sl], sem.at[1, sl]).wait()

    @pl.when(n == 0)
    def _():
        start_group(b, j, slot)

    wait_group(slot)
    nxt = jnp.minimum(n + 1, total - 1)
    start_group(nxt // nj, nxt % nj, 1 - slot)

    @pl.when(j == 0)
    def _():
        m_scr[...] = jnp.full(m_scr.shape, NEG, F32)
        l_scr[...] = jnp.zeros(l_scr.shape, F32)
        acc_scr[...] = jnp.zeros(acc_scr.shape, F32)

    def attend(c16s, pt16s, mask):
        krts = [_nt(wkt_ref[...], c16) for c16 in c16s]
        s_list = []
        for krt, pt16 in zip(krts, pt16s):
            ssq = jnp.concatenate(
                [jnp.sum(jnp.square(krt[h * NOPE:(h + 1) * NOPE, :]), axis=0, keepdims=True) for h in range(H_A)],
                axis=0)
            rinv = lax.rsqrt(ssq * (1.0 / NOPE) + EPS)
            sraw = jnp.dot(qn_ref[...], krt.astype(BF16), preferred_element_type=F32)
            spe = jnp.dot(qp_ref[...], pt16, preferred_element_type=F32)
            s = sraw * jnp.concatenate([rinv] * (nrow // H_A), axis=0) + spe
            if mask is not None:
                s = jnp.where(mask, s, NEG)
            s_list.append(s)
        m, l, acc = m_scr[...], l_scr[...], acc_scr[...]
        for s, c16 in zip(s_list, c16s):
            m_new = jnp.maximum(m, jnp.max(s, axis=-1, keepdims=True))
            alpha = jnp.exp(m - m_new)
            e = jnp.exp(s - m_new)
            l = l * alpha + jnp.sum(e, axis=-1, keepdims=True)
            acc = acc * alpha + jnp.dot(e.astype(BF16), c16, preferred_element_type=F32)
            m = m_new
        l_scr[...] = l
        acc_scr[...] = acc
        m_scr[...] = m

    c16s, pt16s = [], []
    page = cbuf.shape[2]
    for g in range(pg // sub):
        c16s.append(cbuf[slot, g * sub:(g + 1) * sub].reshape(sub * page, KV_LORA).astype(BF16))
        pt16s.append(jnp.concatenate([pbuf[slot, t].astype(BF16) for t in range(g * sub, (g + 1) * sub)], axis=1))
    attend(c16s, pt16s, None)

    @pl.when(j == nj - 1)
    def _():
        nnew = cnew_ref.shape[0]
        rows = lax.broadcasted_iota(jnp.int32, (nrow, nnew), 0) // H_A
        cols = lax.broadcasted_iota(jnp.int32, (nrow, nnew), 1)
        attend([cnew_ref[...].astype(BF16)], [pnew_ref[...].astype(BF16)], cols <= rows)
        o_lat = acc_scr[...] / l_scr[...]
        hi, lo = _split2(o_lat)
        o_ref[...] = (jnp.dot(hi, wuv_ref[...], preferred_element_type=F32)
                      + jnp.dot(lo, wuv_ref[...], preferred_element_type=F32))

    @pl.when(n == total - 1)
    def _():
        wait_group(1 - slot)


def _s_attn_call(page_table, qn_blk, qp_blk, cnew, pnew_t, wkt16, wuv16, cache_ckv, cache_kpe_t, layer):
    bd, nrow, _ = qn_blk.shape
    n_pages = page_table.shape[1]
    page = cache_ckv.shape[2]
    pg = min(PAGES_PER_STEP, n_pages)
    sub = min(PAGES_PER_SUB, pg)
    assert n_pages % pg == 0 and pg % sub == 0
    nnew = cnew.shape[1]
    grid = (bd, n_pages // pg)
    seq3 = lambda b, j, pt: (b, 0, 0)
    const2 = lambda b, j, pt: (0, 0)

    in_specs = [pl.BlockSpec((None, nrow, W_A), seq3), pl.BlockSpec((None, nrow, ROPE), seq3),
                pl.BlockSpec((None, nnew, KV_LORA), seq3), pl.BlockSpec((None, ROPE, nnew), seq3),
                pl.BlockSpec(wkt16.shape, const2), pl.BlockSpec(wuv16.shape, const2),
                pl.BlockSpec(memory_space=pl.ANY), pl.BlockSpec(memory_space=pl.ANY)]
    gs = pltpu.PrefetchScalarGridSpec(
        num_scalar_prefetch=1, grid=grid, in_specs=in_specs,
        out_specs=pl.BlockSpec((None, nrow, W_A), seq3),
        scratch_shapes=[pltpu.VMEM((2, pg, page, KV_LORA), F32), pltpu.VMEM((2, pg, ROPE, page), F32),
                        pltpu.SemaphoreType.DMA((2, 2)),
                        pltpu.VMEM((nrow, 1), F32), pltpu.VMEM((nrow, 1), F32), pltpu.VMEM((nrow, KV_LORA), F32)])
    return pl.pallas_call(
        functools.partial(_s_attn_body, pg=pg, sub=sub, layer=layer), grid_spec=gs,
        out_shape=jax.ShapeDtypeStruct((bd, nrow, W_A), F32),
        compiler_params=_cparams(("arbitrary", "arbitrary")), name="s_attn",
    )(page_table, qn_blk, qp_blk, cnew, pnew_t, wkt16, wuv16, cache_ckv, cache_kpe_t)


def _rwkv_body(r_ref, lw_ref, km_ref, v_ref, kk_ref, kka_ref, gb_ref, s0_ref, rk_ref, lnw_ref, lnb_ref,
               o_ref, sout_ref, s_scr, *, c, nb):
    ci = pl.program_id(1)
    nc = pl.num_programs(1)
    g4 = RW_GROUP
    items = [(s, g) for s in range(nb) for g in range(RW_NGROUPS)]

    rr = lax.broadcasted_iota(jnp.int32, (g4, g4), 0) // N_B
    cc = lax.broadcasted_iota(jnp.int32, (g4, g4), 1) // N_B
    blockmask = rr == cc
    ones_blk = blockmask.astype(BF16)
    t_idx = lax.broadcasted_iota(jnp.int32, (c, g4), 0)
    i_idx = lax.broadcasted_iota(jnp.int32, (c, g4), 1) % N_B
    low_strict = i_idx < t_idx
    low_incl = i_idx <= t_idx
    eye = (i_idx == t_idx).astype(F32)

    def bd_rows(y):
        y16 = y.astype(BF16)
        if c == N_B:
            t = jnp.concatenate([y16] * 4, axis=0)
        else:
            pad = jnp.zeros((N_B - c, g4), BF16)
            t = jnp.concatenate([y16, pad] * 4, axis=0)
        return jnp.where(blockmask, t, jnp.zeros_like(t))

    def mmh(x, y):
        return jnp.dot(x.astype(BF16), bd_rows(y), preferred_element_type=F32)

    @pl.when(ci == 0)
    def _():
        s_scr[...] = jnp.zeros(s_scr.shape, F32)
        for s in range(nb):
            for h in range(H_B):
                g, jh = divmod(h, 4)
                s_scr[s, g, jh * N_B:(jh + 1) * N_B, jh * N_B:(jh + 1) * N_B] = s0_ref[s, h]

    tri = (lax.broadcasted_iota(jnp.int32, (c, c), 1) <= lax.broadcasted_iota(jnp.int32, (c, c), 0)).astype(BF16)
    g_seq = []
    for s in range(nb):
        g_seq.append(sum(jnp.dot(tri, part, preferred_element_type=F32) for part in _split3(lw_ref[s])))

    st = []
    for (s, g) in items:
        sl = slice(g * g4, (g + 1) * g4)
        lw = lw_ref[s, :, sl]
        gc = g_seq[s][:, sl]
        big = jnp.exp(gc)
        ginv = jnp.exp(-gc)
        at = -kk_ref[s, :, sl] * jnp.exp(gc - lw)
        bt = kka_ref[s, :, sl] * ginv
        kt = km_ref[s, :, sl] * ginv
        rt = r_ref[s, :, sl] * big
        s_old = s_scr[s, g]
        st.append(dict(sl=sl, s=s, g=g, bt=bt, kt=kt, g_last=big[c - 1:c, :], s_old=s_old,
                       lhs2=jnp.concatenate([at, rt], axis=0).astype(BF16), v=v_ref[s, :, sl]))

    for d in st:
        x1 = _nt(d["lhs2"], bd_rows(d["bt"]))
        x2 = _nt(d["lhs2"], bd_rows(d["kt"]))
        d["x3"] = _nt(d["lhs2"], d["s_old"].astype(BF16))
        d["a_ab"] = jnp.where(low_strict, x1[:c], 0.0)
        d["a_rb"] = jnp.where(low_incl, x1[c:], 0.0)
        d["a_ak"] = jnp.where(low_strict, x2[:c], 0.0)
        d["a_rk"] = jnp.where(low_incl, x2[c:], 0.0)
    for d in st:
        x4 = jnp.dot(jnp.concatenate([d["a_ak"], d["a_rk"]], axis=0).astype(BF16), bd_rows(d["v"]),
                     preferred_element_type=F32)
        d["b"] = x4[:c] + d["x3"][:c]
        d["ypart"] = x4[c:] + d["x3"][c:]

    if c <= 16:
        for d in st:
            d["tinv"] = eye + d["a_ab"]
            d["pw"] = d["a_ab"]
        n = 1
        while 2 * n < c:
            for d in st:
                d["pw"] = mmh(d["pw"], d["pw"])
            for d in st:
                d["tinv"] = mmh(d["tinv"], eye + d["pw"])
            n *= 2
        for d in st:
            d["u"] = mmh(d["tinv"], d["b"])
    else:
        same16 = (i_idx // 16) == (t_idx // 16)
        for d in st:
            dg = jnp.where(same16, d["a_ab"], 0.0)
            d["lo"] = d["a_ab"] - dg
            d["td"] = eye + dg
            d["pw"] = dg
        for _ in range(3):
            for d in st:
                d["pw"] = mmh(d["pw"], d["pw"])
            for d in st:
                d["td"] = mmh(d["td"], eye + d["pw"])
        for d in st:
            d["nn"] = mmh(d["td"], d["lo"])
            d["w"] = mmh(d["td"], d["b"])
        for d in st:
            d["n2"] = mmh(d["nn"], d["nn"])
        for d in st:
            d["w2"] = d["w"] + mmh(d["n2"], d["w"])
        for d in st:
            d["u"] = d["w2"] + mmh(d["nn"], d["w2"])

    for d in st:
        d["y"] = d["ypart"] + mmh(d["a_rb"], d["u"])
        lhs_t = jnp.concatenate([d["v"], d["u"]], axis=0).astype(BF16)
        rhs_t = jnp.concatenate([d["kt"] * d["g_last"], d["bt"] * d["g_last"]], axis=0).astype(BF16)
        upd = lax.dot_general(lhs_t, rhs_t, (((0,), (0,)), ((), ())), preferred_element_type=F32)
        s_scr[d["s"], d["g"]] = d["s_old"] * d["g_last"] + jnp.where(blockmask, upd, 0.0)

    def head_sums(xs):
        parts = []
        for x in xs:
            parts.extend(_split2(x))
        res = jnp.dot(jnp.concatenate(parts, axis=0), ones_blk, preferred_element_type=F32)
        return [res[(2 * k) * c:(2 * k + 1) * c] + res[(2 * k + 1) * c:(2 * k + 2) * c] for k in range(len(xs))]

    rkr = [r_ref[d["s"], :, d["sl"]] * km_ref[d["s"], :, d["sl"]] * rk_ref[:, d["sl"]] for d in st]
    sums = head_sums([d["y"] for d in st] + rkr)
    devs = [d["y"] - sums[k] * (1.0 / N_B) for k, d in enumerate(st)]
    var_sums = head_sums([dv * dv for dv in devs])
    for k, d in enumerate(st):
        sl = d["sl"]
        yn = devs[k] * lax.rsqrt(var_sums[k] * (1.0 / N_B) + LNX_EPS) * lnw_ref[:, sl] + lnb_ref[:, sl]
        bonus = sums[len(st) + k] * d["v"]
        gb = gb_ref[d["s"], :, sl]
        o_ref[d["s"], :, sl] = (yn + bonus) * (gb * jax.nn.sigmoid(gb))

    @pl.when(ci == nc - 1)
    def _():
        for s in range(nb):
            for h in range(H_B):
                g, jh = divmod(h, 4)
                sout_ref[s, h] = s_scr[s, g, jh * N_B:(jh + 1) * N_B, jh * N_B:(jh + 1) * N_B]


def _rwkv_call(r, lw, km, v, kk, kka, gb, s0, rk, lnw, lnb, *, c, nb):
    nseq, t, _ = r.shape
    nc = t // c
    assert nseq % nb == 0 and t % c == 0
    tok = pl.BlockSpec((nb, c, W_B), lambda b, i: (b, i, 0))
    vec = pl.BlockSpec((1, W_B), lambda b, i: (0, 0))
    st = pl.BlockSpec((nb, H_B, N_B, N_B), lambda b, i: (b, 0, 0, 0))
    return pl.pallas_call(
        functools.partial(_rwkv_body, c=c, nb=nb), grid=(nseq // nb, nc),
        in_specs=[tok] * 7 + [st, vec, vec, vec],
        out_specs=[tok, st],
        out_shape=[jax.ShapeDtypeStruct((nseq, t, W_B), F32), jax.ShapeDtypeStruct((nseq, H_B, N_B, N_B), F32)],
        scratch_shapes=[pltpu.VMEM((nb, RW_NGROUPS, RW_GROUP, RW_GROUP), F32)],
        compiler_params=_cparams(("arbitrary", "arbitrary")), name=f"rwkv_c{c}",
    )(r, lw, km, v, kk, kka, gb, s0, rk, lnw, lnb)


def _back_body(x_ref, oa_ref, ga_ref, ob_ref, p_ref, woa_ref, wob_ref, wple_ref, pn_ref, wg_ref, y_ref):
    ga = ga_ref[...]
    mixed_a = oa_ref[...] * (ga * jax.nn.sigmoid(ga))
    x1 = (x_ref[...] + jnp.dot(mixed_a.astype(BF16), woa_ref[...], preferred_element_type=F32)
          + jnp.dot(ob_ref[...].astype(BF16), wob_ref[...], preferred_element_type=F32))
    xg = _rms(x1, pn_ref[...], x1.shape[-1])
    gate = jax.nn.sigmoid(jnp.dot(xg.astype(BF16), wg_ref[...], preferred_element_type=F32))
    y_ref[...] = x1 + jnp.dot(p_ref[...].astype(BF16), wple_ref[...], preferred_element_type=F32) * gate


def _back_call(x2d, oa, ga, ob, p2d, wts):
    n, d = x2d.shape
    tm = min(BACK_TM, n)
    row = lambda i: (i, 0)
    const2 = lambda i: (0, 0)
    ins = [x2d, oa, ga, ob, p2d]
    return pl.pallas_call(
        _back_body, grid=(n // tm,),
        in_specs=[pl.BlockSpec((tm, a.shape[1]), row) for a in ins] + [pl.BlockSpec(w.shape, const2) for w in wts],
        out_specs=pl.BlockSpec((tm, d), row),
        out_shape=jax.ShapeDtypeStruct((n, d), F32),
        compiler_params=_cparams(("arbitrary",)), name="back",
    )(*ins, *wts)


def _pad_cols(a, width):
    return jnp.pad(a, ((0, 0), (0, width - a.shape[1])))


def _row(vec):
    return vec.reshape(1, -1).astype(F32)


def _rope_tabs(pos):
    inv = ROPE_BASE ** (-jnp.arange(0, ROPE, 2, dtype=F32) / ROPE)
    ang = pos.astype(F32)[:, None] * inv[None, :]
    cos, sin = jnp.cos(ang), jnp.sin(ang)
    t = pos.shape[0]
    z16 = jnp.zeros((t, ROPE // 2), F32)
    c = jnp.concatenate([cos, cos, jnp.ones((t, NOPE), F32), jnp.zeros((t, LANES - ROPE - NOPE), F32)], axis=1)
    s1 = _pad_cols(jnp.concatenate([-sin, z16], axis=1), LANES)
    s2 = _pad_cols(jnp.concatenate([z16, sin], axis=1), LANES)
    return jnp.stack([c, s1, s2])


def _layer_weights(i, norm_in, w_in, mu_shift, q_a_norm, w_uq, kv_a_norm, w_ukv, g_q_nope, g_q_pe, g_k_nope,
                   g_k_pe, w_decay0, w_decay_up, a0, w_iclr_up, k_k, k_a):
    w = w_in[i]
    c_q, c_kv, c_kpe = 0, Q_LORA, Q_LORA + KV_LORA
    c_ga = c_kpe + ROPE
    c_gb = c_ga + W_A
    c_sh = c_gb + W_B
    c_xw = c_sh + 3 * W_B
    c_xa = c_xw + DECAY_LORA
    w_perm = jnp.concatenate([
        w[:, c_q:c_kpe], _pad_cols(w[:, c_kpe:c_ga], LANES), w[:, c_ga:c_sh], w[:, c_sh:c_xw],
        _pad_cols(w[:, c_xw:c_xa], LANES), _pad_cols(w[:, c_xa:], LANES)], axis=1).astype(BF16)
    mu = mu_shift[i][None, :]
    mu_perm = jnp.concatenate([mu[:, :3 * W_B], _pad_cols(mu[:, 3 * W_B:3 * W_B + DECAY_LORA], LANES),
                               _pad_cols(mu[:, 3 * W_B + DECAY_LORA:], LANES)], axis=1)
    uq = w_uq[i]
    uq_cat = jnp.concatenate([uq[..., NOPE:], uq[..., :NOPE],
                              jnp.zeros((Q_LORA, H_A, LANES - NOPE - ROPE), F32)], axis=-1)
    uq_cat = uq_cat.reshape(Q_LORA, HCAT).astype(BF16)
    ukv = w_ukv[i]
    uk = ukv[..., :NOPE]
    uk_cat = jnp.concatenate([jnp.zeros((KV_LORA, H_A, ROPE), F32), uk,
                              jnp.zeros((KV_LORA, H_A, LANES - NOPE - ROPE), F32)], axis=-1)
    uk_cat = uk_cat.reshape(KV_LORA, HCAT).astype(BF16)
    uv = ukv[..., NOPE:].reshape(KV_LORA, W_A).astype(BF16)
    uv_cat = jnp.concatenate([ukv[..., NOPE:], jnp.zeros((KV_LORA, H_A, LANES - V_DIM), F32)], axis=-1)
    uv_cat = uv_cat.reshape(KV_LORA, HCAT).astype(BF16)
    ukt = jnp.transpose(uk.reshape(KV_LORA, H_A * NOPE)).astype(BF16)
    zpad = jnp.zeros((LANES - NOPE - ROPE,), F32)
    gq_prompt = jnp.concatenate([g_q_pe[i], g_q_nope[i], zpad]) * SCALE
    gq_sample = jnp.concatenate([g_q_pe[i], g_q_nope[i] * g_k_nope[i], zpad]) * SCALE
    gk = jnp.concatenate([jnp.zeros((ROPE,), F32), g_k_nope[i], zpad])
    gkp = jnp.concatenate([g_k_pe[i], jnp.zeros((LANES - ROPE,), F32)])
    wdu = jnp.pad(w_decay_up[i], ((0, LANES - DECAY_LORA), (0, 0))).astype(BF16)
    wiu = jnp.pad(w_iclr_up[i], ((0, LANES - ICLR_LORA), (0, 0))).astype(BF16)

    def front_wts(gq):
        return [_row(norm_in[i]), w_perm, mu_perm, _row(q_a_norm[i]), uq_cat, _row(kv_a_norm[i]), uk_cat, uv_cat,
                _row(gq), _row(gk), _row(gkp), _row(w_decay0[i]), wdu, _row(a0[i]), wiu, _row(k_k[i]), _row(k_a[i])]

    return front_wts(gq_prompt), front_wts(gq_sample), w_perm[:, P_SH:], ukt, uv


def kernel(x_prompt, x_sample, p_prompt, p_sample, cache_ckv, cache_kpe, state_wkv, state_shift, page_table,
           norm_in, w_in, mu_shift, q_a_norm, w_uq, kv_a_norm, w_ukv, g_q_nope, g_q_pe, g_k_nope, g_k_pe,
           w_decay0, w_decay_up, a0, w_iclr_up, k_k, k_a, r_k, lnx_w, lnx_b, w_out, w_ple, ple_norm, w_ple_gate):
    b, t, d = x_prompt.shape
    bd, tq, _ = x_sample.shape
    depth = w_in.shape[0]
    page = cache_ckv.shape[2]
    past_len = page_table.shape[1] * page
    tabs_p = _rope_tabs(jnp.arange(t))
    tabs_s = jnp.tile(_rope_tabs(past_len + jnp.arange(tq)), (1, bd, 1))
    cache_kpe_t = jnp.swapaxes(cache_kpe, 2, 3)
    y_p = x_prompt.reshape(b * t, d)
    y_s = x_sample.reshape(bd * tq, d)
    outs = [[] for _ in range(8)]
    for i in range(depth):
        wts_p, wts_s, w_shift16, ukt16, uv16 = _layer_weights(
            i, norm_in, w_in, mu_shift, q_a_norm, w_uq, kv_a_norm, w_ukv, g_q_nope, g_q_pe, g_k_nope, g_k_pe,
            w_decay0, w_decay_up, a0, w_iclr_up, k_k, k_a)
        rw_vecs = [_row(r_k[i]), _row(lnx_w[i]), _row(lnx_b[i])]
        wo = w_out[i].astype(BF16)
        back_wts = [wo[:W_A], wo[W_A:], w_ple[i].astype(BF16), _row(ple_norm[i]), w_ple_gate[i].astype(BF16)]

        (qcat, kcat, v16, ckv, kpe, ga, gb, r, lw, km, vb, kkn, kka, xl) = _front_call(
            y_p.reshape(b, t, d), jnp.zeros((1, N_SH), F32), tabs_p, wts_p, sample=False)
        o_a = _p_attn_call(qcat, kcat, v16, nseq=b, seq_len=t)
        seq3 = lambda a: a.reshape(b, t, W_B)
        o_b, s_p = _rwkv_call(seq3(r), seq3(lw), seq3(km), seq3(vb), seq3(kkn), seq3(kka), seq3(gb),
                              jnp.zeros((b, H_B, N_B, N_B), F32), *rw_vecs, c=min(RW_CHUNK, t), nb=4)
        y_p = _back_call(y_p, o_a, ga, o_b.reshape(b * t, W_B), p_prompt[i].reshape(b * t, -1), back_wts)
        outs[0].append(ckv.reshape(b, t, KV_LORA)); outs[2].append(kpe.reshape(b, t, ROPE))
        outs[4].append(s_p); outs[6].append(xl)

        prev_proj = _matmul_call(state_shift[i], w_shift16)
        prev0 = jnp.pad(prev_proj[:, None, :], ((0, 0), (0, tq - 1), (0, 0))).reshape(bd * tq, N_SH)
        (qcat, _, _, ckv, kpe, ga, gb, r, lw, km, vb, kkn, kka, xl) = _front_call(
            y_s.reshape(bd, tq, d), prev0, tabs_s, wts_s, sample=True)
        q4 = qcat.reshape(bd, tq, H_A, LANES)
        qp_blk = q4[..., :ROPE].reshape(bd, tq * H_A, ROPE)
        eye_h = jnp.eye(H_A, dtype=BF16)
        qn_blk = (q4[..., ROPE:ROPE + NOPE][:, :, :, None, :] * eye_h[None, None, :, :, None]
                  ).reshape(bd, tq * H_A, W_A)
        npad = LANES
        cnew = jnp.pad(ckv.reshape(bd, tq, KV_LORA), ((0, 0), (0, npad - tq), (0, 0)))
        pnew_t = jnp.swapaxes(jnp.pad(kpe.reshape(bd, tq, ROPE), ((0, 0), (0, npad - tq), (0, 0))), 1, 2)
        o_full = _s_attn_call(page_table, qn_blk, qp_blk, cnew, pnew_t, ukt16, uv16, cache_ckv, cache_kpe_t, i)
        o5 = o_full.reshape(bd, tq, H_A, H_A, V_DIM)
        o_a = jnp.einsum('bqhhd->bqhd', o5).reshape(bd * tq, W_A)
        cpad = 16

        def pad_tok(a):
            return jnp.pad(a.reshape(bd, tq, -1), ((0, 0), (0, cpad - tq), (0, 0)))

        o_b16, s_s = _rwkv_call(pad_tok(r), pad_tok(lw), pad_tok(km), pad_tok(vb), pad_tok(kkn), pad_tok(kka),
                                pad_tok(gb), state_wkv[i], *rw_vecs, c=cpad, nb=4)
        o_b = o_b16[:, :tq].reshape(bd * tq, W_B)
        y_s = _back_call(y_s, o_a, ga, o_b, p_sample[i].reshape(bd * tq, -1), back_wts)
        outs[1].append(ckv.reshape(bd, tq, KV_LORA)); outs[3].append(kpe.reshape(bd, tq, ROPE))
        outs[5].append(s_s); outs[7].append(xl)
    st = [jnp.stack(o) for o in outs]
    return (y_p.reshape(b, t, d), y_s.reshape(bd, tq, d), st[0], st[1], st[2], st[3], st[4], st[5], st[6], st[7])
```

```python
import functools

import jax
import jax.numpy as jnp
from jax import lax
from jax.experimental import pallas as pl
from jax.experimental.pallas import tpu as pltpu

F32 = jnp.float32
BF16 = jnp.bfloat16

LANES = 128
H_A = 8
NOPE = 64
ROPE = 32
V_DIM = 64
Q_LORA = 384
KV_LORA = 256
ROPE_BASE = 10000.0
SCALE = (NOPE + ROPE) ** -0.5
H_B = 8
N_B = 64
W_A = H_A * V_DIM
W_B = H_B * N_B
DECAY_LORA = 64
ICLR_LORA = 64
LNX_EPS = 64e-5
EPS = 1e-6
NEG = -1e30

P_Q = 0
P_KV = P_Q + Q_LORA
P_KPE = P_KV + KV_LORA
P_GA = P_KPE + LANES
P_GB = P_GA + W_A
P_SH = P_GB + W_B
N_SH = 3 * W_B + 2 * LANES
P_TOT = P_SH + N_SH

HCAT = H_A * LANES
RW_CHUNK = 64
RW_GROUP = 4 * N_B
RW_NGROUPS = W_B // RW_GROUP
FRONT_TM = 512
FRONT_SUB = 256
BACK_TM = 512
ATTN_TQ = 512
ATTN_TK = 512
PAGES_PER_STEP = 32
PAGES_PER_SUB = 4
VMEM_LIMIT = 56 * 1024 * 1024


def _cparams(sem):
    return pltpu.CompilerParams(dimension_semantics=sem, vmem_limit_bytes=VMEM_LIMIT)


def _lane_iota(n=LANES):
    return lax.broadcasted_iota(jnp.int32, (1, n), 1)


def _rms(x, g, n):
    ms = jnp.sum(x * x, axis=-1, keepdims=True) * (1.0 / n)
    return x * lax.rsqrt(ms + EPS) * g


def _rope128(n, c, s1, s2):
    return n * c + pltpu.roll(n, LANES - ROPE // 2, axis=1) * s1 + pltpu.roll(n, ROPE // 2, axis=1) * s2


def _split2(x):
    hi = x.astype(BF16)
    lo = (x - hi.astype(F32)).astype(BF16)
    return hi, lo


def _split3(x):
    hi = x.astype(BF16)
    r1 = x - hi.astype(F32)
    mid = r1.astype(BF16)
    lo = (r1 - mid.astype(F32)).astype(BF16)
    return hi, mid, lo


def _nt(x16, y16):
    return lax.dot_general(x16, y16, (((1,), (1,)), ((), ())), preferred_element_type=F32)


def _front_body(x_ref, xlast_ref, prev0_ref, tab_ref, norm_in_ref, w_in_ref, mu_ref, qan_ref, wuq_ref, kvan_ref,
                wuk_ref, wuv_ref, gq_ref, gk_ref, gkp_ref, wd0_ref, wdu_ref, a0_ref, wiu_ref, kk_ref_, ka_ref,
                qcat_ref, kcat_ref, v_ref, ckv_ref, kpe_ref, ga_ref, gb_ref,
                r_ref, lw_ref, km_ref, vb_ref, kkn_ref, kka_ref, xl_ref,
                cur_scr, *, tm, sub_rows, tiles_per_seq, seq_len, sample):
    i = pl.program_id(0)
    d_model = x_ref.shape[-1]
    xl_ref[...] = _rms(xlast_ref[...], norm_in_ref[...], d_model)
    subs = [slice(r0, r0 + sub_rows) for r0 in range(0, tm, sub_rows)]

    if not sample:
        @pl.when(i % tiles_per_seq == 0)
        def _():
            cur_scr[7:8, :] = prev0_ref[...]
    else:
        cur_scr[7:8, :] = jnp.zeros((1, N_SH), F32)
    xn16s = []
    for rs in subs:
        xn16s.append(_rms(x_ref[rs, :], norm_in_ref[...], d_model).astype(BF16))
        cur_scr[8 + rs.start:8 + rs.stop, :] = jnp.dot(xn16s[-1], w_in_ref[:, P_SH:P_TOT],
                                                        preferred_element_type=F32)
    for rs, xn16 in zip(subs, xn16s):
        _front_rows(rs, xn16, prev0_ref, tab_ref, w_in_ref, mu_ref, qan_ref, wuq_ref, kvan_ref,
                    wuk_ref, wuv_ref, gq_ref, gk_ref, gkp_ref, wd0_ref, wdu_ref, a0_ref, wiu_ref, kk_ref_, ka_ref,
                    qcat_ref, kcat_ref, v_ref, ckv_ref, kpe_ref, ga_ref, gb_ref,
                    r_ref, lw_ref, km_ref, vb_ref, kkn_ref, kka_ref, cur_scr, seq_len=seq_len, sample=sample)
    if not sample:
        cur_scr[7:8, :] = cur_scr[7 + tm:8 + tm, :]


def _front_rows(rs, xn16, prev0_ref, tab_ref, w_in_ref, mu_ref, qan_ref, wuq_ref, kvan_ref,
                wuk_ref, wuv_ref, gq_ref, gk_ref, gkp_ref, wd0_ref, wdu_ref, a0_ref, wiu_ref, kk_ref_, ka_ref,
                qcat_ref, kcat_ref, v_ref, ckv_ref, kpe_ref, ga_ref, gb_ref,
                r_ref, lw_ref, km_ref, vb_ref, kkn_ref, kka_ref, cur_scr, *, seq_len, sample):
    nrows = rs.stop - rs.start

    def proj(lo, hi):
        return jnp.dot(xn16, w_in_ref[:, lo:hi], preferred_element_type=F32)

    c_tab, s1_tab, s2_tab = tab_ref[0, rs, :], tab_ref[1, rs, :], tab_ref[2, rs, :]
    lane = _lane_iota()
    is_rope = lane < ROPE

    c_q = _rms(proj(P_Q, P_KV), qan_ref[...], Q_LORA)
    q = jnp.dot(c_q.astype(BF16), wuq_ref[...], preferred_element_type=F32)
    gq = gq_ref[...]
    for h in range(H_A):
        qb = q[:, h * LANES:(h + 1) * LANES]
        sq = qb * qb
        s_r = jnp.sum(jnp.where(is_rope, sq, 0.0), axis=-1, keepdims=True)
        s_n = jnp.sum(jnp.where(is_rope, 0.0, sq), axis=-1, keepdims=True)
        inv = jnp.where(is_rope, lax.rsqrt(s_r * (1.0 / ROPE) + EPS), lax.rsqrt(s_n * (1.0 / NOPE) + EPS))
        qn = qb * inv * gq
        qcat_ref[rs, h * LANES:(h + 1) * LANES] = _rope128(qn, c_tab, s1_tab, s2_tab).astype(BF16)

    p_kv = proj(P_KV, P_GA)
    c_kv = _rms(p_kv[:, :KV_LORA], kvan_ref[...], KV_LORA)
    ckv_ref[rs, :] = c_kv
    c_kv16 = c_kv.astype(BF16)
    is_one = (_lane_iota(HCAT) % LANES) == V_DIM
    v_ref[rs, :] = jnp.where(is_one, 1.0, jnp.dot(c_kv16, wuv_ref[...], preferred_element_type=F32)).astype(BF16)
    kraw = jnp.dot(c_kv16, wuk_ref[...], preferred_element_type=F32)
    kp = p_kv[:, KV_LORA:]
    kp_n = kp * lax.rsqrt(jnp.sum(kp * kp, axis=-1, keepdims=True) * (1.0 / ROPE) + EPS) * gkp_ref[...]
    kp_r = _rope128(kp_n, c_tab, s1_tab, s2_tab)
    kpe_ref[rs, :] = kp_r[:, :ROPE]
    gk = gk_ref[...]
    for h in range(H_A):
        kb = kraw[:, h * LANES:(h + 1) * LANES]
        s_n = jnp.sum(kb * kb, axis=-1, keepdims=True)
        kn = kb * lax.rsqrt(s_n * (1.0 / NOPE) + EPS) * gk
        kcat_ref[rs, h * LANES:(h + 1) * LANES] = (kn + kp_r).astype(BF16)

    ga_ref[rs, :] = proj(P_GA, P_GB)
    gb_ref[rs, :] = proj(P_GB, P_SH)

    cur = cur_scr[8 + rs.start:8 + rs.stop, :]
    prev = cur_scr[7 + rs.start:7 + rs.stop, :]
    if sample:
        row = lax.broadcasted_iota(jnp.int32, (nrows, 1), 0) + rs.start
        prev = jnp.where(row % seq_len == 0, prev0_ref[rs, :], prev)
    sh = cur + (prev - cur) * mu_ref[...]
    r = sh[:, 0:W_B]
    k = sh[:, W_B:2 * W_B]
    v = sh[:, 2 * W_B:3 * W_B]
    xw = sh[:, 3 * W_B:3 * W_B + LANES]
    xa = sh[:, 3 * W_B + LANES:3 * W_B + 2 * LANES]
    z = wd0_ref[...] + jnp.dot(jnp.tanh(xw).astype(BF16), wdu_ref[...], preferred_element_type=F32)
    nz = -z
    softplus = jnp.maximum(nz, 0.0) + jnp.log1p(jnp.exp(-jnp.abs(nz)))
    w_log = -softplus - 0.5
    lw_ref[rs, :] = -jnp.exp(w_log)
    a = jax.nn.sigmoid(a0_ref[...] + jnp.dot(xa.astype(BF16), wiu_ref[...], preferred_element_type=F32))
    kk = k * kk_ref_[...]
    lo_half = lane < N_B
    for c in range(W_B // LANES):
        blk = kk[:, c * LANES:(c + 1) * LANES]
        sq = blk * blk
        s0 = jnp.sum(jnp.where(lo_half, sq, 0.0), axis=-1, keepdims=True)
        s1 = jnp.sum(jnp.where(lo_half, 0.0, sq), axis=-1, keepdims=True)
        den = jnp.where(lo_half, jnp.maximum(jnp.sqrt(s0), 1e-12), jnp.maximum(jnp.sqrt(s1), 1e-12))
        kkn = blk / den
        kkn_ref[rs, c * LANES:(c + 1) * LANES] = kkn
        kka_ref[rs, c * LANES:(c + 1) * LANES] = kkn * a[:, c * LANES:(c + 1) * LANES]
    r_ref[rs, :] = r
    vb_ref[rs, :] = v
    km_ref[rs, :] = k * (1.0 + (a - 1.0) * ka_ref[...])


def _front_call(x3d, prev0, tabs, wts, *, sample):
    nseq, seq_len, d = x3d.shape
    x2d = x3d.reshape(nseq * seq_len, d)
    xlast = x3d[:, seq_len - 1, :]
    n = x2d.shape[0]
    if sample:
        tm = n
        tiles_per_seq = 1
    else:
        tm = FRONT_TM
        assert seq_len % tm == 0
        tiles_per_seq = seq_len // tm
    grid = (n // tm,)
    ttab = tabs.shape[1]
    tab_blocks = ttab // tm
    row = lambda i: (i, 0)
    const2 = lambda i: (0, 0)
    prev_spec = pl.BlockSpec((tm, N_SH), row) if sample else pl.BlockSpec((1, N_SH), const2)
    xl_shape = jax.ShapeDtypeStruct((nseq, d), F32)
    xl_spec = pl.BlockSpec((nseq, d), const2)
    scratch = [pltpu.VMEM((tm + 8, N_SH), F32)]
    in_specs = [pl.BlockSpec((tm, d), row), pl.BlockSpec((nseq, d), const2), prev_spec,
                pl.BlockSpec((3, tm, LANES), lambda i: (0, i % tab_blocks, 0))]
    in_specs += [pl.BlockSpec(w.shape, const2, pipeline_mode=pl.Buffered(1)) for w in wts]
    out_shapes = [
        jax.ShapeDtypeStruct((n, HCAT), BF16), jax.ShapeDtypeStruct((n, HCAT), BF16),
        jax.ShapeDtypeStruct((n, HCAT), BF16), jax.ShapeDtypeStruct((n, KV_LORA), F32),
        jax.ShapeDtypeStruct((n, ROPE), F32), jax.ShapeDtypeStruct((n, W_A), F32),
        jax.ShapeDtypeStruct((n, W_B), F32),
    ] + [jax.ShapeDtypeStruct((n, W_B), F32)] * 6 + [xl_shape]
    out_specs = [pl.BlockSpec((tm, s.shape[1]), row) for s in out_shapes[:-1]] + [xl_spec]
    sub_rows = min(FRONT_SUB, tm)
    assert tm % sub_rows == 0
    body = functools.partial(_front_body, tm=tm, sub_rows=sub_rows, tiles_per_seq=tiles_per_seq, seq_len=seq_len,
                             sample=sample)
    return pl.pallas_call(
        body, grid=grid, in_specs=in_specs, out_specs=out_specs, out_shape=out_shapes,
        scratch_shapes=scratch, compiler_params=_cparams(("arbitrary",)),
        name="front_sample" if sample else "front_prompt",
    )(x2d, xlast, prev0, tabs, *wts)


def _matmul_body(a_ref, b_ref, o_ref):
    o_ref[...] = jnp.dot(a_ref[...].astype(BF16), b_ref[...], preferred_element_type=F32)


def _matmul_call(a, b16):
    m, k = a.shape
    n = b16.shape[1]
    return pl.pallas_call(
        _matmul_body, grid=(1,),
        in_specs=[pl.BlockSpec((m, k), lambda i: (0, 0)), pl.BlockSpec((k, n), lambda i: (0, 0))],
        out_specs=pl.BlockSpec((m, n), lambda i: (0, 0)),
        out_shape=jax.ShapeDtypeStruct((m, n), F32),
        compiler_params=_cparams(("arbitrary",)), name="prev_proj",
    )(a, b16)


def _p_attn_body(q_ref, k_ref, v_ref, o_ref, *, tq, tk):
    qi = pl.program_id(2)
    lane = _lane_iota()
    qs = [q_ref[:, hh * LANES:(hh + 1) * LANES] for hh in range(2)]
    nsub = tq // tk

    def scores(j, hh):
        kb = k_ref[pl.ds(pl.multiple_of(j * tk, tk), tk), hh * LANES:(hh + 1) * LANES]
        return _nt(qs[hh], kb)

    def values(j, hh):
        return v_ref[pl.ds(pl.multiple_of(j * tk, tk), tk), hh * LANES:(hh + 1) * LANES]

    def update(carry, s, vb):
        m, acc = carry
        m_new = jnp.maximum(m, jnp.max(s, axis=-1, keepdims=True))
        e16 = jnp.exp(s - m_new).astype(BF16)
        return m_new, acc * jnp.exp(m - m_new) + jnp.dot(e16, vb, preferred_element_type=F32)

    def step(j, carry):
        ss = [scores(j, hh) for hh in range(2)]
        return tuple(update(carry[hh], ss[hh], values(j, hh)) for hh in range(2))

    init = (jnp.full((tq, 1), NEG, F32), jnp.zeros((tq, LANES), F32))
    carry = lax.fori_loop(0, qi * nsub, step, (init, init))
    rows = lax.broadcasted_iota(jnp.int32, (tq, tk), 0)
    cols = lax.broadcasted_iota(jnp.int32, (tq, tk), 1)
    for d in range(nsub):
        j = qi * nsub + d
        causal = cols + d * tk <= rows
        ss = [jnp.where(causal, scores(j, hh), NEG) for hh in range(2)]
        carry = tuple(update(carry[hh], ss[hh], values(j, hh)) for hh in range(2))
    outs = [carry[hh][1] / carry[hh][1][:, V_DIM:V_DIM + 1] for hh in range(2)]
    o_ref[...] = jnp.where(lane < V_DIM, outs[0], pltpu.roll(outs[1], V_DIM, axis=1))


def _p_attn_call(qcat, kcat, v16, *, nseq, seq_len):
    n = qcat.shape[0]
    tq = min(ATTN_TQ, seq_len)
    tk = min(ATTN_TK, tq)
    assert seq_len % tq == 0 and tq % tk == 0
    nq = seq_len // tq
    grid = (nseq, H_A // 2, nq)
    return pl.pallas_call(
        functools.partial(_p_attn_body, tq=tq, tk=tk), grid=grid,
        in_specs=[pl.BlockSpec((tq, 2 * LANES), lambda b, h, i: (b * nq + i, h)),
                  pl.BlockSpec((seq_len, 2 * LANES), lambda b, h, i: (b, h)),
                  pl.BlockSpec((seq_len, 2 * LANES), lambda b, h, i: (b, h))],
        out_specs=pl.BlockSpec((tq, LANES), lambda b, h, i: (b * nq + i, h)),
        out_shape=jax.ShapeDtypeStruct((n, W_A), F32),
        compiler_params=_cparams(("arbitrary", "arbitrary", "arbitrary")), name="p_attn",
    )(qcat, kcat, v16)


def _s_attn_body(pt_ref, qn_ref, qp_ref, cnew_ref, pnew_ref, wkt_ref, wuv_ref, ckv_hbm, kpe_hbm, o_ref,
                 cbuf0, pbuf0, cbuf1, pbuf1, sem, m_scr, l_scr, acc_scr, *, pg, sub, layer):
    b = pl.program_id(0)
    j = pl.program_id(1)
    nj = pl.num_programs(1)
    n = b * nj + j
    total = pl.num_programs(0) * nj
    nrow = qn_ref.shape[0]
    bufs = ((cbuf0, pbuf0), (cbuf1, pbuf1))

    def start_group(bb, first_page, sl):
        cbuf, pbuf = bufs[sl]
        for t in range(pg):
            idx = pt_ref[bb, first_page + t]
            pltpu.make_async_copy(ckv_hbm.at[layer, idx], cbuf.at[t], sem.at[0, sl]).start()
            pltpu.make_async_copy(kpe_hbm.at[layer, idx], pbuf.at[t], sem.at[1, sl]).start()

    def wait_group(sl):
        cbuf, pbuf = bufs[sl]
        pltpu.make_async_copy(cbuf, cbuf, sem.at[0, sl]).wait()
        pltpu.make_async_copy(pbuf, pbuf, sem.at[1, sl]).wait()

    @pl.when(n == 0)
    def _():
        start_group(b, 0, 0)

    @pl.when(j == 0)
    def _():
        m_scr[...] = jnp.full(m_scr.shape, NEG, F32)
        l_scr[...] = jnp.zeros(l_scr.shape, F32)
        acc_scr[...] = jnp.zeros(acc_scr.shape, F32)

    def attend(c16s, pt16s, mask):
        krts = [_nt(wkt_ref[...], c16) for c16 in c16s]
        s_list = []
        for krt, pt16 in zip(krts, pt16s):
            ssq = jnp.concatenate(
                [jnp.sum(jnp.square(krt[h * NOPE:(h + 1) * NOPE, :]), axis=0, keepdims=True) for h in range(H_A)],
                axis=0)
            rinv = lax.rsqrt(ssq * (1.0 / NOPE) + EPS)
            sraw = jnp.dot(qn_ref[...], krt.astype(BF16), preferred_element_type=F32)
            spe = jnp.dot(qp_ref[...], pt16, preferred_element_type=F32)
            s = sraw * jnp.concatenate([rinv] * (nrow // H_A), axis=0) + spe
            if mask is not None:
                s = jnp.where(mask, s, NEG)
            s_list.append(s)
        m, l, acc = m_scr[...], l_scr[...], acc_scr[...]
        for s, c16 in zip(s_list, c16s):
            m_new = jnp.maximum(m, jnp.max(s, axis=-1, keepdims=True))
            alpha = jnp.exp(m - m_new)
            e = jnp.exp(s - m_new)
            l = l * alpha + jnp.sum(e, axis=-1, keepdims=True)
            acc = acc * alpha + jnp.dot(e.astype(BF16), c16, preferred_element_type=F32)
            m = m_new
        l_scr[...] = l
        acc_scr[...] = acc
        m_scr[...] = m

    def attend_buffer(sl):
        cbuf, pbuf = bufs[sl]
        page = cbuf.shape[1]
        c16s, pt16s = [], []
        for g in range(pg // sub):
            c16s.append(cbuf[g * sub:(g + 1) * sub].reshape(sub * page, KV_LORA).astype(BF16))
            pt16s.append(jnp.concatenate([pbuf[t].astype(BF16) for t in range(g * sub, (g + 1) * sub)], axis=1))
        attend(c16s, pt16s, None)

    wait_group(0)
    start_group(b, (2 * j + 1) * pg, 1)
    attend_buffer(0)
    wait_group(1)
    nxt = jnp.minimum(n + 1, total - 1)
    start_group(nxt // nj, (nxt % nj) * 2 * pg, 0)
    attend_buffer(1)

    @pl.when(j == nj - 1)
    def _():
        nnew = cnew_ref.shape[0]
        rows = lax.broadcasted_iota(jnp.int32, (nrow, nnew), 0) // H_A
        cols = lax.broadcasted_iota(jnp.int32, (nrow, nnew), 1)
        attend([cnew_ref[...].astype(BF16)], [pnew_ref[...].astype(BF16)], cols <= rows)
        o_lat = acc_scr[...] / l_scr[...]
        hi, lo = _split2(o_lat)
        o_ref[...] = (jnp.dot(hi, wuv_ref[...], preferred_element_type=F32)
                      + jnp.dot(lo, wuv_ref[...], preferred_element_type=F32))

    @pl.when(n == total - 1)
    def _():
        wait_group(0)


def _s_attn_call(page_table, qn_blk, qp_blk, cnew, pnew_t, wkt16, wuv16, cache_ckv, cache_kpe_t, layer):
    bd, nrow, _ = qn_blk.shape
    n_pages = page_table.shape[1]
    page = cache_ckv.shape[2]
    pg = min(PAGES_PER_STEP, n_pages // 2)
    sub = min(PAGES_PER_SUB, pg)
    assert n_pages % (2 * pg) == 0 and pg % sub == 0
    nnew = cnew.shape[1]
    grid = (bd, n_pages // (2 * pg))
    seq3 = lambda b, j, pt: (b, 0, 0)
    const2 = lambda b, j, pt: (0, 0)

    in_specs = [pl.BlockSpec((None, nrow, W_A), seq3), pl.BlockSpec((None, nrow, ROPE), seq3),
                pl.BlockSpec((None, nnew, KV_LORA), seq3), pl.BlockSpec((None, ROPE, nnew), seq3),
                pl.BlockSpec(wkt16.shape, const2), pl.BlockSpec(wuv16.shape, const2),
                pl.BlockSpec(memory_space=pl.ANY), pl.BlockSpec(memory_space=pl.ANY)]
    gs = pltpu.PrefetchScalarGridSpec(
        num_scalar_prefetch=1, grid=grid, in_specs=in_specs,
        out_specs=pl.BlockSpec((None, nrow, W_A), seq3),
        scratch_shapes=[pltpu.VMEM((pg, page, KV_LORA), F32), pltpu.VMEM((pg, ROPE, page), F32),
                        pltpu.VMEM((pg, page, KV_LORA), F32), pltpu.VMEM((pg, ROPE, page), F32),
                        pltpu.SemaphoreType.DMA((2, 2)),
                        pltpu.VMEM((nrow, 1), F32), pltpu.VMEM((nrow, 1), F32), pltpu.VMEM((nrow, KV_LORA), F32)])
    return pl.pallas_call(
        functools.partial(_s_attn_body, pg=pg, sub=sub, layer=layer), grid_spec=gs,
        out_shape=jax.ShapeDtypeStruct((bd, nrow, W_A), F32),
        compiler_params=_cparams(("arbitrary", "arbitrary")), name="s_attn",
    )(page_table, qn_blk, qp_blk, cnew, pnew_t, wkt16, wuv16, cache_ckv, cache_kpe_t)


def _rwkv_body(r_ref, lw_ref, km_ref, v_ref, kk_ref, kka_ref, gb_ref, s0_ref, rk_ref, lnw_ref, lnb_ref,
               o_ref, sout_ref, s_scr, *, c, nb):
    ci = pl.program_id(1)
    nc = pl.num_programs(1)
    g4 = RW_GROUP
    items = [(s, g) for s in range(nb) for g in range(RW_NGROUPS)]

    rr = lax.broadcasted_iota(jnp.int32, (g4, g4), 0) // N_B
    cc = lax.broadcasted_iota(jnp.int32, (g4, g4), 1) // N_B
    blockmask = rr == cc
    ones_blk = blockmask.astype(BF16)
    t_idx = lax.broadcasted_iota(jnp.int32, (c, g4), 0)
    i_idx = lax.broadcasted_iota(jnp.int32, (c, g4), 1) % N_B
    low_strict = i_idx < t_idx
    low_incl = i_idx <= t_idx
    eye = (i_idx == t_idx).astype(F32)
    lane_half = [_lane_iota() < N_B, _lane_iota() >= N_B]

    def bd_rows(y):
        y16 = y.astype(BF16)
        zero = jnp.zeros((c, LANES), BF16)
        blocks = []
        for h in range(g4 // N_B):
            col = y16[:, (h // 2) * LANES:(h // 2 + 1) * LANES]
            piece = jnp.where(lane_half[h % 2], col, zero)
            blocks.append(jnp.concatenate([piece, zero] if h < 2 else [zero, piece], axis=1))
            if c < N_B:
                blocks.append(jnp.zeros((N_B - c, g4), BF16))
        return jnp.concatenate(blocks, axis=0)

    def mmh(x, y):
        return jnp.dot(x.astype(BF16), bd_rows(y), preferred_element_type=F32)

    @pl.when(ci == 0)
    def _():
        s_scr[...] = jnp.zeros(s_scr.shape, F32)
        for s in range(nb):
            for h in range(H_B):
                g, jh = divmod(h, 4)
                s_scr[s, g, jh * N_B:(jh + 1) * N_B, jh * N_B:(jh + 1) * N_B] = s0_ref[s, h]

    tri = (lax.broadcasted_iota(jnp.int32, (c, c), 1) <= lax.broadcasted_iota(jnp.int32, (c, c), 0)).astype(BF16)
    g_seq = []
    for s in range(nb):
        g_seq.append(sum(jnp.dot(tri, part, preferred_element_type=F32) for part in _split3(lw_ref[s])))

    st = []
    for (s, g) in items:
        sl = slice(g * g4, (g + 1) * g4)
        lw = lw_ref[s, :, sl]
        gc = g_seq[s][:, sl]
        big = jnp.exp(gc)
        ginv = jnp.exp(-gc)
        at = -kk_ref[s, :, sl] * jnp.exp(gc - lw)
        bt = kka_ref[s, :, sl] * ginv
        kt = km_ref[s, :, sl] * ginv
        rt = r_ref[s, :, sl] * big
        s_old = s_scr[s, g]
        st.append(dict(sl=sl, s=s, g=g, bt=bt, kt=kt, g_last=big[c - 1:c, :], s_old=s_old,
                       lhs2=jnp.concatenate([at, rt], axis=0).astype(BF16), v=v_ref[s, :, sl]))

    for d in st:
        x1 = _nt(d["lhs2"], bd_rows(d["bt"]))
        x2 = _nt(d["lhs2"], bd_rows(d["kt"]))
        d["x3"] = _nt(d["lhs2"], d["s_old"].astype(BF16))
        d["a_ab"] = jnp.where(low_strict, x1[:c], 0.0)
        d["a_rb"] = jnp.where(low_incl, x1[c:], 0.0)
        d["a_ak"] = jnp.where(low_strict, x2[:c], 0.0)
        d["a_rk"] = jnp.where(low_incl, x2[c:], 0.0)
    for d in st:
        x4 = jnp.dot(jnp.concatenate([d["a_ak"], d["a_rk"]], axis=0).astype(BF16), bd_rows(d["v"]),
                     preferred_element_type=F32)
        d["b"] = x4[:c] + d["x3"][:c]
        d["ypart"] = x4[c:] + d["x3"][c:]

    if c <= 16:
        for d in st:
            d["tinv"] = eye + d["a_ab"]
            d["pw"] = d["a_ab"]
        n = 1
        while 2 * n < c:
            for d in st:
                d["pw"] = mmh(d["pw"], d["pw"])
            for d in st:
                d["tinv"] = mmh(d["tinv"], eye + d["pw"])
            n *= 2
        for d in st:
            d["u"] = mmh(d["tinv"], d["b"])
    else:
        same16 = (i_idx // 16) == (t_idx // 16)
        for d in st:
            dg = jnp.where(same16, d["a_ab"], 0.0)
            d["lo"] = d["a_ab"] - dg
            d["td"] = eye + dg
            d["pw"] = dg
        for _ in range(3):
            for d in st:
                d["pw"] = mmh(d["pw"], d["pw"])
            for d in st:
                d["td"] = mmh(d["td"], eye + d["pw"])
        for d in st:
            d["nn"] = mmh(d["td"], d["lo"])
            d["w"] = mmh(d["td"], d["b"])
        for d in st:
            d["n2"] = mmh(d["nn"], d["nn"])
        for d in st:
            d["w2"] = d["w"] + mmh(d["n2"], d["w"])
        for d in st:
            d["u"] = d["w2"] + mmh(d["nn"], d["w2"])

    for d in st:
        d["y"] = d["ypart"] + mmh(d["a_rb"], d["u"])
        lhs_t = jnp.concatenate([d["v"], d["u"]], axis=0).astype(BF16)
        rhs_t = jnp.concatenate([d["kt"] * d["g_last"], d["bt"] * d["g_last"]], axis=0).astype(BF16)
        upd = lax.dot_general(lhs_t, rhs_t, (((0,), (0,)), ((), ())), preferred_element_type=F32)
        s_scr[d["s"], d["g"]] = d["s_old"] * d["g_last"] + jnp.where(blockmask, upd, 0.0)

    def head_sums(xs):
        parts = []
        for x in xs:
            parts.extend(_split2(x))
        res = jnp.dot(jnp.concatenate(parts, axis=0), ones_blk, preferred_element_type=F32)
        return [res[(2 * k) * c:(2 * k + 1) * c] + res[(2 * k + 1) * c:(2 * k + 2) * c] for k in range(len(xs))]

    rkr = [r_ref[d["s"], :, d["sl"]] * km_ref[d["s"], :, d["sl"]] * rk_ref[:, d["sl"]] for d in st]
    sums = head_sums([d["y"] for d in st] + rkr)
    devs = [d["y"] - sums[k] * (1.0 / N_B) for k, d in enumerate(st)]
    var_sums = head_sums([dv * dv for dv in devs])
    for k, d in enumerate(st):
        sl = d["sl"]
        yn = devs[k] * lax.rsqrt(var_sums[k] * (1.0 / N_B) + LNX_EPS) * lnw_ref[:, sl] + lnb_ref[:, sl]
        bonus = sums[len(st) + k] * d["v"]
        gb = gb_ref[d["s"], :, sl]
        o_ref[d["s"], :, sl] = (yn + bonus) * (gb * jax.nn.sigmoid(gb))

    @pl.when(ci == nc - 1)
    def _():
        for s in range(nb):
            for h in range(H_B):
                g, jh = divmod(h, 4)
                sout_ref[s, h] = s_scr[s, g, jh * N_B:(jh + 1) * N_B, jh * N_B:(jh + 1) * N_B]


def _rwkv_call(r, lw, km, v, kk, kka, gb, s0, rk, lnw, lnb, *, c, nb):
    nseq, t, _ = r.shape
    nc = t // c
    assert nseq % nb == 0 and t % c == 0
    tok = pl.BlockSpec((nb, c, W_B), lambda b, i: (b, i, 0))
    vec = pl.BlockSpec((1, W_B), lambda b, i: (0, 0))
    st = pl.BlockSpec((nb, H_B, N_B, N_B), lambda b, i: (b, 0, 0, 0))
    return pl.pallas_call(
        functools.partial(_rwkv_body, c=c, nb=nb), grid=(nseq // nb, nc),
        in_specs=[tok] * 7 + [st, vec, vec, vec],
        out_specs=[tok, st],
        out_shape=[jax.ShapeDtypeStruct((nseq, t, W_B), F32), jax.ShapeDtypeStruct((nseq, H_B, N_B, N_B), F32)],
        scratch_shapes=[pltpu.VMEM((nb, RW_NGROUPS, RW_GROUP, RW_GROUP), F32)],
        compiler_params=_cparams(("arbitrary", "arbitrary")), name=f"rwkv_c{c}",
    )(r, lw, km, v, kk, kka, gb, s0, rk, lnw, lnb)


def _back_body(x_ref, oa_ref, ga_ref, ob_ref, p_ref, woa_ref, wob_ref, wple_ref, pn_ref, wg_ref, y_ref):
    ga = ga_ref[...]
    mixed_a = oa_ref[...] * (ga * jax.nn.sigmoid(ga))
    x1 = (x_ref[...] + jnp.dot(mixed_a.astype(BF16), woa_ref[...], preferred_element_type=F32)
          + jnp.dot(ob_ref[...].astype(BF16), wob_ref[...], preferred_element_type=F32))
    xg = _rms(x1, pn_ref[...], x1.shape[-1])
    gate = jax.nn.sigmoid(jnp.dot(xg.astype(BF16), wg_ref[...], preferred_element_type=F32))
    y_ref[...] = x1 + jnp.dot(p_ref[...].astype(BF16), wple_ref[...], preferred_element_type=F32) * gate


def _back_call(x2d, oa, ga, ob, p2d, wts):
    n, d = x2d.shape
    tm = min(BACK_TM, n)
    row = lambda i: (i, 0)
    const2 = lambda i: (0, 0)
    ins = [x2d, oa, ga, ob, p2d]
    return pl.pallas_call(
        _back_body, grid=(n // tm,),
        in_specs=[pl.BlockSpec((tm, a.shape[1]), row) for a in ins] + [pl.BlockSpec(w.shape, const2) for w in wts],
        out_specs=pl.BlockSpec((tm, d), row),
        out_shape=jax.ShapeDtypeStruct((n, d), F32),
        compiler_params=_cparams(("arbitrary",)), name="back",
    )(*ins, *wts)


def _pad_cols(a, width):
    return jnp.pad(a, ((0, 0), (0, width - a.shape[1])))


def _row(vec):
    return vec.reshape(1, -1).astype(F32)


def _rope_tabs(pos):
    inv = ROPE_BASE ** (-jnp.arange(0, ROPE, 2, dtype=F32) / ROPE)
    ang = pos.astype(F32)[:, None] * inv[None, :]
    cos, sin = jnp.cos(ang), jnp.sin(ang)
    t = pos.shape[0]
    z16 = jnp.zeros((t, ROPE // 2), F32)
    c = jnp.concatenate([cos, cos, jnp.ones((t, NOPE), F32), jnp.zeros((t, LANES - ROPE - NOPE), F32)], axis=1)
    s1 = _pad_cols(jnp.concatenate([-sin, z16], axis=1), LANES)
    s2 = _pad_cols(jnp.concatenate([z16, sin], axis=1), LANES)
    return jnp.stack([c, s1, s2])


def _layer_weights(i, norm_in, w_in, mu_shift, q_a_norm, w_uq, kv_a_norm, w_ukv, g_q_nope, g_q_pe, g_k_nope,
                   g_k_pe, w_decay0, w_decay_up, a0, w_iclr_up, k_k, k_a):
    w = w_in[i]
    c_q, c_kv, c_kpe = 0, Q_LORA, Q_LORA + KV_LORA
    c_ga = c_kpe + ROPE
    c_gb = c_ga + W_A
    c_sh = c_gb + W_B
    c_xw = c_sh + 3 * W_B
    c_xa = c_xw + DECAY_LORA
    w_perm = jnp.concatenate([
        w[:, c_q:c_kpe], _pad_cols(w[:, c_kpe:c_ga], LANES), w[:, c_ga:c_sh], w[:, c_sh:c_xw],
        _pad_cols(w[:, c_xw:c_xa], LANES), _pad_cols(w[:, c_xa:], LANES)], axis=1).astype(BF16)
    mu = mu_shift[i][None, :]
    mu_perm = jnp.concatenate([mu[:, :3 * W_B], _pad_cols(mu[:, 3 * W_B:3 * W_B + DECAY_LORA], LANES),
                               _pad_cols(mu[:, 3 * W_B + DECAY_LORA:], LANES)], axis=1)
    uq = w_uq[i]
    uq_cat = jnp.concatenate([uq[..., NOPE:], uq[..., :NOPE],
                              jnp.zeros((Q_LORA, H_A, LANES - NOPE - ROPE), F32)], axis=-1)
    uq_cat = uq_cat.reshape(Q_LORA, HCAT).astype(BF16)
    ukv = w_ukv[i]
    uk = ukv[..., :NOPE]
    uk_cat = jnp.concatenate([jnp.zeros((KV_LORA, H_A, ROPE), F32), uk,
                              jnp.zeros((KV_LORA, H_A, LANES - NOPE - ROPE), F32)], axis=-1)
    uk_cat = uk_cat.reshape(KV_LORA, HCAT).astype(BF16)
    uv = ukv[..., NOPE:].reshape(KV_LORA, W_A).astype(BF16)
    uv_cat = jnp.concatenate([ukv[..., NOPE:], jnp.zeros((KV_LORA, H_A, LANES - V_DIM), F32)], axis=-1)
    uv_cat = uv_cat.reshape(KV_LORA, HCAT).astype(BF16)
    ukt = jnp.transpose(uk.reshape(KV_LORA, H_A * NOPE)).astype(BF16)
    zpad = jnp.zeros((LANES - NOPE - ROPE,), F32)
    gq_prompt = jnp.concatenate([g_q_pe[i], g_q_nope[i], zpad]) * SCALE
    gq_sample = jnp.concatenate([g_q_pe[i], g_q_nope[i] * g_k_nope[i], zpad]) * SCALE
    gk = jnp.concatenate([jnp.zeros((ROPE,), F32), g_k_nope[i], zpad])
    gkp = jnp.concatenate([g_k_pe[i], jnp.zeros((LANES - ROPE,), F32)])
    wdu = jnp.pad(w_decay_up[i], ((0, LANES - DECAY_LORA), (0, 0))).astype(BF16)
    wiu = jnp.pad(w_iclr_up[i], ((0, LANES - ICLR_LORA), (0, 0))).astype(BF16)

    def front_wts(gq):
        return [_row(norm_in[i]), w_perm, mu_perm, _row(q_a_norm[i]), uq_cat, _row(kv_a_norm[i]), uk_cat, uv_cat,
                _row(gq), _row(gk), _row(gkp), _row(w_decay0[i]), wdu, _row(a0[i]), wiu, _row(k_k[i]), _row(k_a[i])]

    return front_wts(gq_prompt), front_wts(gq_sample), w_perm[:, P_SH:], ukt, uv


def kernel(x_prompt, x_sample, p_prompt, p_sample, cache_ckv, cache_kpe, state_wkv, state_shift, page_table,
           norm_in, w_in, mu_shift, q_a_norm, w_uq, kv_a_norm, w_ukv, g_q_nope, g_q_pe, g_k_nope, g_k_pe,
           w_decay0, w_decay_up, a0, w_iclr_up, k_k, k_a, r_k, lnx_w, lnx_b, w_out, w_ple, ple_norm, w_ple_gate):
    b, t, d = x_prompt.shape
    bd, tq, _ = x_sample.shape
    depth = w_in.shape[0]
    page = cache_ckv.shape[2]
    past_len = page_table.shape[1] * page
    tabs_p = _rope_tabs(jnp.arange(t))
    tabs_s = jnp.tile(_rope_tabs(past_len + jnp.arange(tq)), (1, bd, 1))
    cache_kpe_t = jnp.swapaxes(cache_kpe, 2, 3)
    y_p = x_prompt.reshape(b * t, d)
    y_s = x_sample.reshape(bd * tq, d)
    outs = [[] for _ in range(8)]
    for i in range(depth):
        wts_p, wts_s, w_shift16, ukt16, uv16 = _layer_weights(
            i, norm_in, w_in, mu_shift, q_a_norm, w_uq, kv_a_norm, w_ukv, g_q_nope, g_q_pe, g_k_nope, g_k_pe,
            w_decay0, w_decay_up, a0, w_iclr_up, k_k, k_a)
        rw_vecs = [_row(r_k[i]), _row(lnx_w[i]), _row(lnx_b[i])]
        wo = w_out[i].astype(BF16)
        back_wts = [wo[:W_A], wo[W_A:], w_ple[i].astype(BF16), _row(ple_norm[i]), w_ple_gate[i].astype(BF16)]

        (qcat, kcat, v16, ckv, kpe, ga, gb, r, lw, km, vb, kkn, kka, xl) = _front_call(
            y_p.reshape(b, t, d), jnp.zeros((1, N_SH), F32), tabs_p, wts_p, sample=False)
        o_a = _p_attn_call(qcat, kcat, v16, nseq=b, seq_len=t)
        seq3 = lambda a: a.reshape(b, t, W_B)
        o_b, s_p = _rwkv_call(seq3(r), seq3(lw), seq3(km), seq3(vb), seq3(kkn), seq3(kka), seq3(gb),
                              jnp.zeros((b, H_B, N_B, N_B), F32), *rw_vecs, c=min(RW_CHUNK, t), nb=8)
        y_p = _back_call(y_p, o_a, ga, o_b.reshape(b * t, W_B), p_prompt[i].reshape(b * t, -1), back_wts)
        outs[0].append(ckv.reshape(b, t, KV_LORA)); outs[2].append(kpe.reshape(b, t, ROPE))
        outs[4].append(s_p); outs[6].append(xl)

        prev_proj = _matmul_call(state_shift[i], w_shift16)
        prev0 = jnp.pad(prev_proj[:, None, :], ((0, 0), (0, tq - 1), (0, 0))).reshape(bd * tq, N_SH)
        (qcat, _, _, ckv, kpe, ga, gb, r, lw, km, vb, kkn, kka, xl) = _front_call(
            y_s.reshape(bd, tq, d), prev0, tabs_s, wts_s, sample=True)
        q4 = qcat.reshape(bd, tq, H_A, LANES)
        qp_blk = q4[..., :ROPE].reshape(bd, tq * H_A, ROPE)
        eye_h = jnp.eye(H_A, dtype=BF16)
        qn_blk = (q4[..., ROPE:ROPE + NOPE][:, :, :, None, :] * eye_h[None, None, :, :, None]
                  ).reshape(bd, tq * H_A, W_A)
        npad = LANES
        cnew = jnp.pad(ckv.reshape(bd, tq, KV_LORA), ((0, 0), (0, npad - tq), (0, 0)))
        pnew_t = jnp.swapaxes(jnp.pad(kpe.reshape(bd, tq, ROPE), ((0, 0), (0, npad - tq), (0, 0))), 1, 2)
        o_full = _s_attn_call(page_table, qn_blk, qp_blk, cnew, pnew_t, ukt16, uv16, cache_ckv, cache_kpe_t, i)
        o5 = o_full.reshape(bd, tq, H_A, H_A, V_DIM)
        o_a = jnp.einsum('bqhhd->bqhd', o5).reshape(bd * tq, W_A)
        cpad = 16

        def pad_tok(a):
            return jnp.pad(a.reshape(bd, tq, -1), ((0, 0), (0, cpad - tq), (0, 0)))

        o_b16, s_s = _rwkv_call(pad_tok(r), pad_tok(lw), pad_tok(km), pad_tok(vb), pad_tok(kkn), pad_tok(kka),
                                pad_tok(gb), state_wkv[i], *rw_vecs, c=cpad, nb=4)
        o_b = o_b16[:, :tq].reshape(bd * tq, W_B)
        y_s = _back_call(y_s, o_a, ga, o_b, p_sample[i].reshape(bd * tq, -1), back_wts)
        outs[1].append(ckv.reshape(bd, tq, KV_LORA)); outs[3].append(kpe.reshape(bd, tq, ROPE))
        outs[5].append(s_s); outs[7].append(xl)
    st = [jnp.stack(o) for o in outs]
    return (y_p.reshape(b, t, d), y_s.reshape(bd, tq, d), st[0], st[1], st[2], st[3], st[4], st[5], st[6], st[7])
```

```python
import functools

import jax
import jax.numpy as jnp
from jax import lax
from jax.experimental import pallas as pl
from jax.experimental.pallas import tpu as pltpu

F32 = jnp.float32
BF16 = jnp.bfloat16

LANES = 128
H_A = 8
NOPE = 64
ROPE = 32
V_DIM = 64
Q_LORA = 384
KV_LORA = 256
ROPE_BASE = 10000.0
SCALE = (NOPE + ROPE) ** -0.5
H_B = 8
N_B = 64
W_A = H_A * V_DIM
W_B = H_B * N_B
DECAY_LORA = 64
ICLR_LORA = 64
LNX_EPS = 64e-5
EPS = 1e-6
NEG = -1e30

P_Q = 0
P_KV = P_Q + Q_LORA
P_KPE = P_KV + KV_LORA
P_GA = P_KPE + LANES
P_GB = P_GA + W_A
P_SH = P_GB + W_B
N_SH = 3 * W_B + 2 * LANES
P_TOT = P_SH + N_SH

HCAT = H_A * LANES
RW_CHUNK = 64
RW_GROUP = 4 * N_B
RW_NGROUPS = W_B // RW_GROUP
FRONT_TM = 512
FRONT_SUB = 256
BACK_TM = 512
ATTN_TQ = 512
ATTN_TK = 512
PAGES_PER_GROUP = 32
PAGES_PER_SUB = 4
VMEM_LIMIT = 56 * 1024 * 1024


def _cparams(sem):
    return pltpu.CompilerParams(dimension_semantics=sem, vmem_limit_bytes=VMEM_LIMIT)


def _lane_iota(n=LANES):
    return lax.broadcasted_iota(jnp.int32, (1, n), 1)


def _rms(x, g, n):
    ms = jnp.sum(x * x, axis=-1, keepdims=True) * (1.0 / n)
    return x * lax.rsqrt(ms + EPS) * g


def _rope128(n, c, s1, s2):
    return n * c + pltpu.roll(n, LANES - ROPE // 2, axis=1) * s1 + pltpu.roll(n, ROPE // 2, axis=1) * s2


def _split2(x):
    hi = x.astype(BF16)
    lo = (x - hi.astype(F32)).astype(BF16)
    return hi, lo


def _split3(x):
    hi = x.astype(BF16)
    r1 = x - hi.astype(F32)
    mid = r1.astype(BF16)
    lo = (r1 - mid.astype(F32)).astype(BF16)
    return hi, mid, lo


def _nt(x16, y16):
    return lax.dot_general(x16, y16, (((1,), (1,)), ((), ())), preferred_element_type=F32)


def _front_body(x_ref, xlast_ref, prev0_ref, tab_ref, norm_in_ref, w_in_ref, mu_ref, qan_ref, wuq_ref, kvan_ref,
                wuk_ref, wuv_ref, gq_ref, gk_ref, gkp_ref, wd0_ref, wdu_ref, a0_ref, wiu_ref, kk_ref_, ka_ref,
                qcat_ref, kcat_ref, v_ref, ckv_ref, kpe_ref, ga_ref, gb_ref,
                r_ref, lw_ref, km_ref, vb_ref, kkn_ref, kka_ref, xl_ref,
                cur_scr, *, tm, sub_rows, tiles_per_seq, seq_len, sample):
    i = pl.program_id(0)
    d_model = x_ref.shape[-1]
    xl_ref[...] = _rms(xlast_ref[...], norm_in_ref[...], d_model)
    subs = [slice(r0, r0 + sub_rows) for r0 in range(0, tm, sub_rows)]

    if not sample:
        @pl.when(i % tiles_per_seq == 0)
        def _():
            cur_scr[7:8, :] = prev0_ref[...]
    else:
        cur_scr[7:8, :] = jnp.zeros((1, N_SH), F32)
    xn16s = []
    for rs in subs:
        xn16s.append(_rms(x_ref[rs, :], norm_in_ref[...], d_model).astype(BF16))
        cur_scr[8 + rs.start:8 + rs.stop, :] = jnp.dot(xn16s[-1], w_in_ref[:, P_SH:P_TOT],
                                                        preferred_element_type=F32)
    for rs, xn16 in zip(subs, xn16s):
        _front_rows(rs, xn16, prev0_ref, tab_ref, w_in_ref, mu_ref, qan_ref, wuq_ref, kvan_ref,
                    wuk_ref, wuv_ref, gq_ref, gk_ref, gkp_ref, wd0_ref, wdu_ref, a0_ref, wiu_ref, kk_ref_, ka_ref,
                    qcat_ref, kcat_ref, v_ref, ckv_ref, kpe_ref, ga_ref, gb_ref,
                    r_ref, lw_ref, km_ref, vb_ref, kkn_ref, kka_ref, cur_scr, seq_len=seq_len, sample=sample)
    if not sample:
        cur_scr[7:8, :] = cur_scr[7 + tm:8 + tm, :]


def _front_rows(rs, xn16, prev0_ref, tab_ref, w_in_ref, mu_ref, qan_ref, wuq_ref, kvan_ref,
                wuk_ref, wuv_ref, gq_ref, gk_ref, gkp_ref, wd0_ref, wdu_ref, a0_ref, wiu_ref, kk_ref_, ka_ref,
                qcat_ref, kcat_ref, v_ref, ckv_ref, kpe_ref, ga_ref, gb_ref,
                r_ref, lw_ref, km_ref, vb_ref, kkn_ref, kka_ref, cur_scr, *, seq_len, sample):
    nrows = rs.stop - rs.start

    def proj(lo, hi):
        return jnp.dot(xn16, w_in_ref[:, lo:hi], preferred_element_type=F32)

    c_tab, s1_tab, s2_tab = tab_ref[0, rs, :], tab_ref[1, rs, :], tab_ref[2, rs, :]
    lane = _lane_iota()
    is_rope = lane < ROPE

    c_q = _rms(proj(P_Q, P_KV), qan_ref[...], Q_LORA)
    q = jnp.dot(c_q.astype(BF16), wuq_ref[...], preferred_element_type=F32)
    gq = gq_ref[...]
    for h in range(H_A):
        qb = q[:, h * LANES:(h + 1) * LANES]
        sq = qb * qb
        s_r = jnp.sum(jnp.where(is_rope, sq, 0.0), axis=-1, keepdims=True)
        s_n = jnp.sum(jnp.where(is_rope, 0.0, sq), axis=-1, keepdims=True)
        inv = jnp.where(is_rope, lax.rsqrt(s_r * (1.0 / ROPE) + EPS), lax.rsqrt(s_n * (1.0 / NOPE) + EPS))
        qn = qb * inv * gq
        qcat_ref[rs, h * LANES:(h + 1) * LANES] = _rope128(qn, c_tab, s1_tab, s2_tab).astype(BF16)

    p_kv = proj(P_KV, P_GA)
    c_kv = _rms(p_kv[:, :KV_LORA], kvan_ref[...], KV_LORA)
    ckv_ref[rs, :] = c_kv
    c_kv16 = c_kv.astype(BF16)
    is_one = (_lane_iota(HCAT) % LANES) == V_DIM
    v_ref[rs, :] = jnp.where(is_one, 1.0, jnp.dot(c_kv16, wuv_ref[...], preferred_element_type=F32)).astype(BF16)
    kraw = jnp.dot(c_kv16, wuk_ref[...], preferred_element_type=F32)
    kp = p_kv[:, KV_LORA:]
    kp_n = kp * lax.rsqrt(jnp.sum(kp * kp, axis=-1, keepdims=True) * (1.0 / ROPE) + EPS) * gkp_ref[...]
    kp_r = _rope128(kp_n, c_tab, s1_tab, s2_tab)
    kpe_ref[rs, :] = kp_r[:, :ROPE]
    gk = gk_ref[...]
    for h in range(H_A):
        kb = kraw[:, h * LANES:(h + 1) * LANES]
        s_n = jnp.sum(kb * kb, axis=-1, keepdims=True)
        kn = kb * lax.rsqrt(s_n * (1.0 / NOPE) + EPS) * gk
        kcat_ref[rs, h * LANES:(h + 1) * LANES] = (kn + kp_r).astype(BF16)

    ga_ref[rs, :] = proj(P_GA, P_GB)
    gb_ref[rs, :] = proj(P_GB, P_SH)

    cur = cur_scr[8 + rs.start:8 + rs.stop, :]
    prev = cur_scr[7 + rs.start:7 + rs.stop, :]
    if sample:
        row = lax.broadcasted_iota(jnp.int32, (nrows, 1), 0) + rs.start
        prev = jnp.where(row % seq_len == 0, prev0_ref[rs, :], prev)
    sh = cur + (prev - cur) * mu_ref[...]
    r = sh[:, 0:W_B]
    k = sh[:, W_B:2 * W_B]
    v = sh[:, 2 * W_B:3 * W_B]
    xw = sh[:, 3 * W_B:3 * W_B + LANES]
    xa = sh[:, 3 * W_B + LANES:3 * W_B + 2 * LANES]
    z = wd0_ref[...] + jnp.dot(jnp.tanh(xw).astype(BF16), wdu_ref[...], preferred_element_type=F32)
    nz = -z
    softplus = jnp.maximum(nz, 0.0) + jnp.log1p(jnp.exp(-jnp.abs(nz)))
    w_log = -softplus - 0.5
    lw_ref[rs, :] = -jnp.exp(w_log)
    a = jax.nn.sigmoid(a0_ref[...] + jnp.dot(xa.astype(BF16), wiu_ref[...], preferred_element_type=F32))
    kk = k * kk_ref_[...]
    lo_half = lane < N_B
    for c in range(W_B // LANES):
        blk = kk[:, c * LANES:(c + 1) * LANES]
        sq = blk * blk
        s0 = jnp.sum(jnp.where(lo_half, sq, 0.0), axis=-1, keepdims=True)
        s1 = jnp.sum(jnp.where(lo_half, 0.0, sq), axis=-1, keepdims=True)
        den = jnp.where(lo_half, jnp.maximum(jnp.sqrt(s0), 1e-12), jnp.maximum(jnp.sqrt(s1), 1e-12))
        kkn = blk / den
        kkn_ref[rs, c * LANES:(c + 1) * LANES] = kkn
        kka_ref[rs, c * LANES:(c + 1) * LANES] = kkn * a[:, c * LANES:(c + 1) * LANES]
    r_ref[rs, :] = r
    vb_ref[rs, :] = v
    km_ref[rs, :] = k * (1.0 + (a - 1.0) * ka_ref[...])


def _front_call(x3d, prev0, tabs, wts, *, sample):
    nseq, seq_len, d = x3d.shape
    x2d = x3d.reshape(nseq * seq_len, d)
    xlast = x3d[:, seq_len - 1, :]
    n = x2d.shape[0]
    if sample:
        tm = n
        tiles_per_seq = 1
    else:
        tm = FRONT_TM
        assert seq_len % tm == 0
        tiles_per_seq = seq_len // tm
    grid = (n // tm,)
    ttab = tabs.shape[1]
    tab_blocks = ttab // tm
    row = lambda i: (i, 0)
    const2 = lambda i: (0, 0)
    prev_spec = pl.BlockSpec((tm, N_SH), row) if sample else pl.BlockSpec((1, N_SH), const2)
    xl_shape = jax.ShapeDtypeStruct((nseq, d), F32)
    xl_spec = pl.BlockSpec((nseq, d), const2)
    scratch = [pltpu.VMEM((tm + 8, N_SH), F32)]
    in_specs = [pl.BlockSpec((tm, d), row), pl.BlockSpec((nseq, d), const2), prev_spec,
                pl.BlockSpec((3, tm, LANES), lambda i: (0, i % tab_blocks, 0))]
    in_specs += [pl.BlockSpec(w.shape, const2, pipeline_mode=pl.Buffered(1)) for w in wts]
    out_shapes = [
        jax.ShapeDtypeStruct((n, HCAT), BF16), jax.ShapeDtypeStruct((n, HCAT), BF16),
        jax.ShapeDtypeStruct((n, HCAT), BF16), jax.ShapeDtypeStruct((n, KV_LORA), F32),
        jax.ShapeDtypeStruct((n, ROPE), F32), jax.ShapeDtypeStruct((n, W_A), F32),
        jax.ShapeDtypeStruct((n, W_B), F32),
    ] + [jax.ShapeDtypeStruct((n, W_B), F32)] * 6 + [xl_shape]
    out_specs = [pl.BlockSpec((tm, s.shape[1]), row) for s in out_shapes[:-1]] + [xl_spec]
    sub_rows = min(FRONT_SUB, tm)
    assert tm % sub_rows == 0
    body = functools.partial(_front_body, tm=tm, sub_rows=sub_rows, tiles_per_seq=tiles_per_seq, seq_len=seq_len,
                             sample=sample)
    return pl.pallas_call(
        body, grid=grid, in_specs=in_specs, out_specs=out_specs, out_shape=out_shapes,
        scratch_shapes=scratch, compiler_params=_cparams(("arbitrary",)),
        name="front_sample" if sample else "front_prompt",
    )(x2d, xlast, prev0, tabs, *wts)


def _matmul_body(a_ref, b_ref, o_ref):
    o_ref[...] = jnp.dot(a_ref[...].astype(BF16), b_ref[...], preferred_element_type=F32)


def _matmul_call(a, b16):
    m, k = a.shape
    n = b16.shape[1]
    return pl.pallas_call(
        _matmul_body, grid=(1,),
        in_specs=[pl.BlockSpec((m, k), lambda i: (0, 0)), pl.BlockSpec((k, n), lambda i: (0, 0))],
        out_specs=pl.BlockSpec((m, n), lambda i: (0, 0)),
        out_shape=jax.ShapeDtypeStruct((m, n), F32),
        compiler_params=_cparams(("arbitrary",)), name="prev_proj",
    )(a, b16)


def _p_attn_body(q_ref, k_ref, v_ref, o_ref, *, tq, tk):
    qi = pl.program_id(2)
    lane = _lane_iota()
    qs = [q_ref[:, hh * LANES:(hh + 1) * LANES] for hh in range(2)]
    nsub = tq // tk

    def scores(j, hh):
        kb = k_ref[pl.ds(pl.multiple_of(j * tk, tk), tk), hh * LANES:(hh + 1) * LANES]
        return _nt(qs[hh], kb)

    def values(j, hh):
        return v_ref[pl.ds(pl.multiple_of(j * tk, tk), tk), hh * LANES:(hh + 1) * LANES]

    def update(carry, s, vb):
        m, acc = carry
        m_new = jnp.maximum(m, jnp.max(s, axis=-1, keepdims=True))
        e16 = jnp.exp(s - m_new).astype(BF16)
        return m_new, acc * jnp.exp(m - m_new) + jnp.dot(e16, vb, preferred_element_type=F32)

    def step(j, carry):
        ss = [scores(j, hh) for hh in range(2)]
        return tuple(update(carry[hh], ss[hh], values(j, hh)) for hh in range(2))

    init = (jnp.full((tq, 1), NEG, F32), jnp.zeros((tq, LANES), F32))
    carry = lax.fori_loop(0, qi * nsub, step, (init, init))
    rows = lax.broadcasted_iota(jnp.int32, (tq, tk), 0)
    cols = lax.broadcasted_iota(jnp.int32, (tq, tk), 1)
    for d in range(nsub):
        j = qi * nsub + d
        causal = cols + d * tk <= rows
        ss = [jnp.where(causal, scores(j, hh), NEG) for hh in range(2)]
        carry = tuple(update(carry[hh], ss[hh], values(j, hh)) for hh in range(2))
    outs = [carry[hh][1] / carry[hh][1][:, V_DIM:V_DIM + 1] for hh in range(2)]
    o_ref[...] = jnp.where(lane < V_DIM, outs[0], pltpu.roll(outs[1], V_DIM, axis=1))


def _p_attn_call(qcat, kcat, v16, *, nseq, seq_len):
    n = qcat.shape[0]
    tq = min(ATTN_TQ, seq_len)
    tk = min(ATTN_TK, tq)
    assert seq_len % tq == 0 and tq % tk == 0
    nq = seq_len // tq
    grid = (nseq, H_A // 2, nq)
    return pl.pallas_call(
        functools.partial(_p_attn_body, tq=tq, tk=tk), grid=grid,
        in_specs=[pl.BlockSpec((tq, 2 * LANES), lambda b, h, i: (b * nq + i, h)),
                  pl.BlockSpec((seq_len, 2 * LANES), lambda b, h, i: (b, h)),
                  pl.BlockSpec((seq_len, 2 * LANES), lambda b, h, i: (b, h))],
        out_specs=pl.BlockSpec((tq, LANES), lambda b, h, i: (b * nq + i, h)),
        out_shape=jax.ShapeDtypeStruct((n, W_A), F32),
        compiler_params=_cparams(("arbitrary", "arbitrary", "arbitrary")), name="p_attn",
    )(qcat, kcat, v16)


def _s_attn_body(pt_ref, qn_ref, qp_ref, cnew_ref, pnew_ref, wkt_ref, wuv_ref, ckv_hbm, kpe_hbm, o_ref,
                 *scr, pg, sub, layer, ngroups):
    bufs = tuple((scr[2 * g], scr[2 * g + 1]) for g in range(ngroups))
    sem = scr[2 * ngroups]
    m_scr, l_scr, acc_scr = scr[2 * ngroups + 1:]
    b = pl.program_id(0)
    last = pl.num_programs(0) - 1
    nrow = qn_ref.shape[0]

    def start_group(bb, sl):
        cbuf, pbuf = bufs[sl]
        for t in range(pg):
            idx = pt_ref[bb, sl * pg + t]
            pltpu.make_async_copy(ckv_hbm.at[layer, idx], cbuf.at[t], sem.at[0, sl]).start()
            pltpu.make_async_copy(kpe_hbm.at[layer, idx], pbuf.at[t], sem.at[1, sl]).start()

    def wait_group(sl):
        cbuf, pbuf = bufs[sl]
        pltpu.make_async_copy(cbuf, cbuf, sem.at[0, sl]).wait()
        pltpu.make_async_copy(pbuf, pbuf, sem.at[1, sl]).wait()

    @pl.when(b == 0)
    def _():
        for g in range(ngroups):
            start_group(b, g)

    m_scr[...] = jnp.full(m_scr.shape, NEG, F32)
    l_scr[...] = jnp.zeros(l_scr.shape, F32)
    acc_scr[...] = jnp.zeros(acc_scr.shape, F32)

    def attend(c16s, pt16s, masks):
        krts = [_nt(wkt_ref[...], c16) for c16 in c16s]
        s_list = []
        for krt, pt16, mask in zip(krts, pt16s, masks):
            ssq = jnp.concatenate(
                [jnp.sum(jnp.square(krt[h * NOPE:(h + 1) * NOPE, :]), axis=0, keepdims=True) for h in range(H_A)],
                axis=0)
            rinv = lax.rsqrt(ssq * (1.0 / NOPE) + EPS)
            sraw = jnp.dot(qn_ref[...], krt.astype(BF16), preferred_element_type=F32)
            spe = jnp.dot(qp_ref[...], pt16, preferred_element_type=F32)
            s = sraw * jnp.concatenate([rinv] * (nrow // H_A), axis=0) + spe
            if mask is not None:
                s = jnp.where(mask, s, NEG)
            s_list.append(s)
        m, l, acc = m_scr[...], l_scr[...], acc_scr[...]
        for s, c16 in zip(s_list, c16s):
            m_new = jnp.maximum(m, jnp.max(s, axis=-1, keepdims=True))
            alpha = jnp.exp(m - m_new)
            e = jnp.exp(s - m_new)
            l = l * alpha + jnp.sum(e, axis=-1, keepdims=True)
            acc = acc * alpha + jnp.dot(e.astype(BF16), c16, preferred_element_type=F32)
            m = m_new
        l_scr[...] = l
        acc_scr[...] = acc
        m_scr[...] = m

    def attend_buffer(sl, with_new):
        cbuf, pbuf = bufs[sl]
        page = cbuf.shape[1]
        c16s, pt16s = [], []
        for g in range(pg // sub):
            c16s.append(cbuf[g * sub:(g + 1) * sub].reshape(sub * page, KV_LORA).astype(BF16))
            pt16s.append(jnp.concatenate([pbuf[t].astype(BF16) for t in range(g * sub, (g + 1) * sub)], axis=1))
        masks = [None] * len(c16s)
        if with_new:
            nnew = cnew_ref.shape[0]
            rows = lax.broadcasted_iota(jnp.int32, (nrow, nnew), 0) // H_A
            cols = lax.broadcasted_iota(jnp.int32, (nrow, nnew), 1)
            c16s.append(cnew_ref[...].astype(BF16))
            pt16s.append(pnew_ref[...].astype(BF16))
            masks.append(cols <= rows)
        attend(c16s, pt16s, masks)

    nxt = jnp.minimum(b + 1, last)
    for g in range(ngroups):
        wait_group(g)
        attend_buffer(g, with_new=(g == ngroups - 1))
        start_group(nxt, g)

    o_lat = acc_scr[...] / l_scr[...]
    hi, lo = _split2(o_lat)
    o_ref[...] = (jnp.dot(hi, wuv_ref[...], preferred_element_type=F32)
                  + jnp.dot(lo, wuv_ref[...], preferred_element_type=F32))

    @pl.when(b == last)
    def _():
        for g in range(ngroups):
            wait_group(g)


def _s_attn_call(page_table, qn_blk, qp_blk, cnew, pnew_t, wkt16, wuv16, cache_ckv, cache_kpe_t, layer):
    bd, nrow, _ = qn_blk.shape
    n_pages = page_table.shape[1]
    page = cache_ckv.shape[2]
    pg = min(PAGES_PER_GROUP, n_pages)
    sub = min(PAGES_PER_SUB, pg)
    assert n_pages % pg == 0 and pg % sub == 0
    ngroups = n_pages // pg
    nnew = cnew.shape[1]
    grid = (bd,)
    seq3 = lambda b, pt: (b, 0, 0)
    const2 = lambda b, pt: (0, 0)

    in_specs = [pl.BlockSpec((None, nrow, W_A), seq3), pl.BlockSpec((None, nrow, ROPE), seq3),
                pl.BlockSpec((None, nnew, KV_LORA), seq3), pl.BlockSpec((None, ROPE, nnew), seq3),
                pl.BlockSpec(wkt16.shape, const2), pl.BlockSpec(wuv16.shape, const2),
                pl.BlockSpec(memory_space=pl.ANY), pl.BlockSpec(memory_space=pl.ANY)]
    gs = pltpu.PrefetchScalarGridSpec(
        num_scalar_prefetch=1, grid=grid, in_specs=in_specs,
        out_specs=pl.BlockSpec((None, nrow, W_A), seq3),
        scratch_shapes=[pltpu.VMEM((pg, page, KV_LORA), F32), pltpu.VMEM((pg, ROPE, page), F32)] * ngroups
        + [pltpu.SemaphoreType.DMA((2, ngroups)),
           pltpu.VMEM((nrow, 1), F32), pltpu.VMEM((nrow, 1), F32), pltpu.VMEM((nrow, KV_LORA), F32)])
    return pl.pallas_call(
        functools.partial(_s_attn_body, pg=pg, sub=sub, layer=layer, ngroups=ngroups), grid_spec=gs,
        out_shape=jax.ShapeDtypeStruct((bd, nrow, W_A), F32),
        compiler_params=_cparams(("arbitrary",)), name="s_attn",
    )(page_table, qn_blk, qp_blk, cnew, pnew_t, wkt16, wuv16, cache_ckv, cache_kpe_t)


def _rwkv_body(r_ref, lw_ref, km_ref, v_ref, kk_ref, kka_ref, gb_ref, s0_ref, rk_ref, lnw_ref, lnb_ref,
               o_ref, sout_ref, s_scr, *, c, nb):
    ci = pl.program_id(1)
    nc = pl.num_programs(1)
    g4 = RW_GROUP
    items = [(s, g) for s in range(nb) for g in range(RW_NGROUPS)]

    rr = lax.broadcasted_iota(jnp.int32, (g4, g4), 0) // N_B
    cc = lax.broadcasted_iota(jnp.int32, (g4, g4), 1) // N_B
    blockmask = rr == cc
    ones_blk = blockmask.astype(BF16)
    t_idx = lax.broadcasted_iota(jnp.int32, (c, g4), 0)
    i_idx = lax.broadcasted_iota(jnp.int32, (c, g4), 1) % N_B
    low_strict = i_idx < t_idx
    low_incl = i_idx <= t_idx
    eye = (i_idx == t_idx).astype(F32)
    lane_half = [_lane_iota() < N_B, _lane_iota() >= N_B]

    def bd_rows(y):
        y16 = y.astype(BF16)
        zero = jnp.zeros((c, LANES), BF16)
        blocks = []
        for h in range(g4 // N_B):
            col = y16[:, (h // 2) * LANES:(h // 2 + 1) * LANES]
            piece = jnp.where(lane_half[h % 2], col, zero)
            blocks.append(jnp.concatenate([piece, zero] if h < 2 else [zero, piece], axis=1))
            if c < N_B:
                blocks.append(jnp.zeros((N_B - c, g4), BF16))
        return jnp.concatenate(blocks, axis=0)

    def mmh(x, y):
        return jnp.dot(x.astype(BF16), bd_rows(y), preferred_element_type=F32)

    @pl.when(ci == 0)
    def _():
        s_scr[...] = jnp.zeros(s_scr.shape, F32)
        for s in range(nb):
            for h in range(H_B):
                g, jh = divmod(h, 4)
                s_scr[s, g, jh * N_B:(jh + 1) * N_B, jh * N_B:(jh + 1) * N_B] = s0_ref[s, h]

    tri = (lax.broadcasted_iota(jnp.int32, (c, c), 1) <= lax.broadcasted_iota(jnp.int32, (c, c), 0)).astype(BF16)
    g_seq = []
    for s in range(nb):
        g_seq.append(sum(jnp.dot(tri, part, preferred_element_type=F32) for part in _split3(lw_ref[s])))

    st = []
    for (s, g) in items:
        sl = slice(g * g4, (g + 1) * g4)
        lw = lw_ref[s, :, sl]
        gc = g_seq[s][:, sl]
        big = jnp.exp(gc)
        ginv = jnp.exp(-gc)
        at = -kk_ref[s, :, sl] * jnp.exp(gc - lw)
        bt = kka_ref[s, :, sl] * ginv
        kt = km_ref[s, :, sl] * ginv
        rt = r_ref[s, :, sl] * big
        s_old = s_scr[s, g]
        st.append(dict(sl=sl, s=s, g=g, bt=bt, kt=kt, g_last=big[c - 1:c, :], s_old=s_old,
                       lhs2=jnp.concatenate([at, rt], axis=0).astype(BF16), v=v_ref[s, :, sl]))

    for d in st:
        x1 = _nt(d["lhs2"], bd_rows(d["bt"]))
        x2 = _nt(d["lhs2"], bd_rows(d["kt"]))
        d["x3"] = _nt(d["lhs2"], d["s_old"].astype(BF16))
        d["a_ab"] = jnp.where(low_strict, x1[:c], 0.0)
        d["a_rb"] = jnp.where(low_incl, x1[c:], 0.0)
        d["a_ak"] = jnp.where(low_strict, x2[:c], 0.0)
        d["a_rk"] = jnp.where(low_incl, x2[c:], 0.0)
    for d in st:
        x4 = jnp.dot(jnp.concatenate([d["a_ak"], d["a_rk"]], axis=0).astype(BF16), bd_rows(d["v"]),
                     preferred_element_type=F32)
        d["b"] = x4[:c] + d["x3"][:c]
        d["ypart"] = x4[c:] + d["x3"][c:]

    if c <= 16:
        for d in st:
            d["tinv"] = eye + d["a_ab"]
            d["pw"] = d["a_ab"]
        n = 1
        while 2 * n < c:
            for d in st:
                d["pw"] = mmh(d["pw"], d["pw"])
            for d in st:
                d["tinv"] = mmh(d["tinv"], eye + d["pw"])
            n *= 2
        for d in st:
            d["u"] = mmh(d["tinv"], d["b"])
    else:
        same16 = (i_idx // 16) == (t_idx // 16)
        for d in st:
            dg = jnp.where(same16, d["a_ab"], 0.0)
            d["lo"] = d["a_ab"] - dg
            d["td"] = eye + dg
            d["pw"] = dg
        for _ in range(3):
            for d in st:
                d["pw"] = mmh(d["pw"], d["pw"])
            for d in st:
                d["td"] = mmh(d["td"], eye + d["pw"])
        for d in st:
            d["nn"] = mmh(d["td"], d["lo"])
            d["w"] = mmh(d["td"], d["b"])
        for d in st:
            d["n2"] = mmh(d["nn"], d["nn"])
        for d in st:
            d["w2"] = d["w"] + mmh(d["n2"], d["w"])
        for d in st:
            d["u"] = d["w2"] + mmh(d["nn"], d["w2"])

    for d in st:
        d["y"] = d["ypart"] + mmh(d["a_rb"], d["u"])
        lhs_t = jnp.concatenate([d["v"], d["u"]], axis=0).astype(BF16)
        rhs_t = jnp.concatenate([d["kt"] * d["g_last"], d["bt"] * d["g_last"]], axis=0).astype(BF16)
        upd = lax.dot_general(lhs_t, rhs_t, (((0,), (0,)), ((), ())), preferred_element_type=F32)
        s_scr[d["s"], d["g"]] = d["s_old"] * d["g_last"] + jnp.where(blockmask, upd, 0.0)

    def head_sums(xs):
        parts = []
        for x in xs:
            parts.extend(_split2(x))
        res = jnp.dot(jnp.concatenate(parts, axis=0), ones_blk, preferred_element_type=F32)
        return [res[(2 * k) * c:(2 * k + 1) * c] + res[(2 * k + 1) * c:(2 * k + 2) * c] for k in range(len(xs))]

    rkr = [r_ref[d["s"], :, d["sl"]] * km_ref[d["s"], :, d["sl"]] * rk_ref[:, d["sl"]] for d in st]
    sums = head_sums([d["y"] for d in st] + rkr)
    devs = [d["y"] - sums[k] * (1.0 / N_B) for k, d in enumerate(st)]
    var_sums = head_sums([dv * dv for dv in devs])
    for k, d in enumerate(st):
        sl = d["sl"]
        yn = devs[k] * lax.rsqrt(var_sums[k] * (1.0 / N_B) + LNX_EPS) * lnw_ref[:, sl] + lnb_ref[:, sl]
        bonus = sums[len(st) + k] * d["v"]
        gb = gb_ref[d["s"], :, sl]
        o_ref[d["s"], :, sl] = (yn + bonus) * (gb * jax.nn.sigmoid(gb))

    @pl.when(ci == nc - 1)
    def _():
        for s in range(nb):
            for h in range(H_B):
                g, jh = divmod(h, 4)
                sout_ref[s, h] = s_scr[s, g, jh * N_B:(jh + 1) * N_B, jh * N_B:(jh + 1) * N_B]


def _rwkv_call(r, lw, km, v, kk, kka, gb, s0, rk, lnw, lnb, *, c, nb):
    nseq, t, _ = r.shape
    nc = t // c
    assert nseq % nb == 0 and t % c == 0
    tok = pl.BlockSpec((nb, c, W_B), lambda b, i: (b, i, 0))
    vec = pl.BlockSpec((1, W_B), lambda b, i: (0, 0))
    st = pl.BlockSpec((nb, H_B, N_B, N_B), lambda b, i: (b, 0, 0, 0))
    return pl.pallas_call(
        functools.partial(_rwkv_body, c=c, nb=nb), grid=(nseq // nb, nc),
        in_specs=[tok] * 7 + [st, vec, vec, vec],
        out_specs=[tok, st],
        out_shape=[jax.ShapeDtypeStruct((nseq, t, W_B), F32), jax.ShapeDtypeStruct((nseq, H_B, N_B, N_B), F32)],
        scratch_shapes=[pltpu.VMEM((nb, RW_NGROUPS, RW_GROUP, RW_GROUP), F32)],
        compiler_params=_cparams(("arbitrary", "arbitrary")), name=f"rwkv_c{c}",
    )(r, lw, km, v, kk, kka, gb, s0, rk, lnw, lnb)


def _back_body(x_ref, oa_ref, ga_ref, ob_ref, p_ref, woa_ref, wob_ref, wple_ref, pn_ref, wg_ref, y_ref):
    ga = ga_ref[...]
    mixed_a = oa_ref[...] * (ga * jax.nn.sigmoid(ga))
    x1 = (x_ref[...] + jnp.dot(mixed_a.astype(BF16), woa_ref[...], preferred_element_type=F32)
          + jnp.dot(ob_ref[...].astype(BF16), wob_ref[...], preferred_element_type=F32))
    xg = _rms(x1, pn_ref[...], x1.shape[-1])
    gate = jax.nn.sigmoid(jnp.dot(xg.astype(BF16), wg_ref[...], preferred_element_type=F32))
    y_ref[...] = x1 + jnp.dot(p_ref[...].astype(BF16), wple_ref[...], preferred_element_type=F32) * gate


def _back_call(x2d, oa, ga, ob, p2d, wts):
    n, d = x2d.shape
    tm = min(BACK_TM, n)
    row = lambda i: (i, 0)
    const2 = lambda i: (0, 0)
    ins = [x2d, oa, ga, ob, p2d]
    return pl.pallas_call(
        _back_body, grid=(n // tm,),
        in_specs=[pl.BlockSpec((tm, a.shape[1]), row) for a in ins] + [pl.BlockSpec(w.shape, const2) for w in wts],
        out_specs=pl.BlockSpec((tm, d), row),
        out_shape=jax.ShapeDtypeStruct((n, d), F32),
        compiler_params=_cparams(("arbitrary",)), name="back",
    )(*ins, *wts)


def _pad_cols(a, width):
    return jnp.pad(a, ((0, 0), (0, width - a.shape[1])))


def _row(vec):
    return vec.reshape(1, -1).astype(F32)


def _rope_tabs(pos):
    inv = ROPE_BASE ** (-jnp.arange(0, ROPE, 2, dtype=F32) / ROPE)
    ang = pos.astype(F32)[:, None] * inv[None, :]
    cos, sin = jnp.cos(ang), jnp.sin(ang)
    t = pos.shape[0]
    z16 = jnp.zeros((t, ROPE // 2), F32)
    c = jnp.concatenate([cos, cos, jnp.ones((t, NOPE), F32), jnp.zeros((t, LANES - ROPE - NOPE), F32)], axis=1)
    s1 = _pad_cols(jnp.concatenate([-sin, z16], axis=1), LANES)
    s2 = _pad_cols(jnp.concatenate([z16, sin], axis=1), LANES)
    return jnp.stack([c, s1, s2])


def _layer_weights(i, norm_in, w_in, mu_shift, q_a_norm, w_uq, kv_a_norm, w_ukv, g_q_nope, g_q_pe, g_k_nope,
                   g_k_pe, w_decay0, w_decay_up, a0, w_iclr_up, k_k, k_a):
    w = w_in[i]
    c_q, c_kv, c_kpe = 0, Q_LORA, Q_LORA + KV_LORA
    c_ga = c_kpe + ROPE
    c_gb = c_ga + W_A
    c_sh = c_gb + W_B
    c_xw = c_sh + 3 * W_B
    c_xa = c_xw + DECAY_LORA
    w_perm = jnp.concatenate([
        w[:, c_q:c_kpe], _pad_cols(w[:, c_kpe:c_ga], LANES), w[:, c_ga:c_sh], w[:, c_sh:c_xw],
        _pad_cols(w[:, c_xw:c_xa], LANES), _pad_cols(w[:, c_xa:], LANES)], axis=1).astype(BF16)
    mu = mu_shift[i][None, :]
    mu_perm = jnp.concatenate([mu[:, :3 * W_B], _pad_cols(mu[:, 3 * W_B:3 * W_B + DECAY_LORA], LANES),
                               _pad_cols(mu[:, 3 * W_B + DECAY_LORA:], LANES)], axis=1)
    uq = w_uq[i]
    uq_cat = jnp.concatenate([uq[..., NOPE:], uq[..., :NOPE],
                              jnp.zeros((Q_LORA, H_A, LANES - NOPE - ROPE), F32)], axis=-1)
    uq_cat = uq_cat.reshape(Q_LORA, HCAT).astype(BF16)
    ukv = w_ukv[i]
    uk = ukv[..., :NOPE]
    uk_cat = jnp.concatenate([jnp.zeros((KV_LORA, H_A, ROPE), F32), uk,
                              jnp.zeros((KV_LORA, H_A, LANES - NOPE - ROPE), F32)], axis=-1)
    uk_cat = uk_cat.reshape(KV_LORA, HCAT).astype(BF16)
    uv = ukv[..., NOPE:].reshape(KV_LORA, W_A).astype(BF16)
    uv_cat = jnp.concatenate([ukv[..., NOPE:], jnp.zeros((KV_LORA, H_A, LANES - V_DIM), F32)], axis=-1)
    uv_cat = uv_cat.reshape(KV_LORA, HCAT).astype(BF16)
    ukt = jnp.transpose(uk.reshape(KV_LORA, H_A * NOPE)).astype(BF16)
    zpad = jnp.zeros((LANES - NOPE - ROPE,), F32)
    gq_prompt = jnp.concatenate([g_q_pe[i], g_q_nope[i], zpad]) * SCALE
    gq_sample = jnp.concatenate([g_q_pe[i], g_q_nope[i] * g_k_nope[i], zpad]) * SCALE
    gk = jnp.concatenate([jnp.zeros((ROPE,), F32), g_k_nope[i], zpad])
    gkp = jnp.concatenate([g_k_pe[i], jnp.zeros((LANES - ROPE,), F32)])
    wdu = jnp.pad(w_decay_up[i], ((0, LANES - DECAY_LORA), (0, 0))).astype(BF16)
    wiu = jnp.pad(w_iclr_up[i], ((0, LANES - ICLR_LORA), (0, 0))).astype(BF16)

    def front_wts(gq):
        return [_row(norm_in[i]), w_perm, mu_perm, _row(q_a_norm[i]), uq_cat, _row(kv_a_norm[i]), uk_cat, uv_cat,
                _row(gq), _row(gk), _row(gkp), _row(w_decay0[i]), wdu, _row(a0[i]), wiu, _row(k_k[i]), _row(k_a[i])]

    return front_wts(gq_prompt), front_wts(gq_sample), w_perm[:, P_SH:], ukt, uv


def kernel(x_prompt, x_sample, p_prompt, p_sample, cache_ckv, cache_kpe, state_wkv, state_shift, page_table,
           norm_in, w_in, mu_shift, q_a_norm, w_uq, kv_a_norm, w_ukv, g_q_nope, g_q_pe, g_k_nope, g_k_pe,
           w_decay0, w_decay_up, a0, w_iclr_up, k_k, k_a, r_k, lnx_w, lnx_b, w_out, w_ple, ple_norm, w_ple_gate):
    b, t, d = x_prompt.shape
    bd, tq, _ = x_sample.shape
    depth = w_in.shape[0]
    page = cache_ckv.shape[2]
    past_len = page_table.shape[1] * page
    tabs_p = _rope_tabs(jnp.arange(t))
    tabs_s = jnp.tile(_rope_tabs(past_len + jnp.arange(tq)), (1, bd, 1))
    cache_kpe_t = jnp.swapaxes(cache_kpe, 2, 3)
    y_p = x_prompt.reshape(b * t, d)
    y_s = x_sample.reshape(bd * tq, d)
    outs = [[] for _ in range(8)]
    for i in range(depth):
        wts_p, wts_s, w_shift16, ukt16, uv16 = _layer_weights(
            i, norm_in, w_in, mu_shift, q_a_norm, w_uq, kv_a_norm, w_ukv, g_q_nope, g_q_pe, g_k_nope, g_k_pe,
            w_decay0, w_decay_up, a0, w_iclr_up, k_k, k_a)
        rw_vecs = [_row(r_k[i]), _row(lnx_w[i]), _row(lnx_b[i])]
        wo = w_out[i].astype(BF16)
        back_wts = [wo[:W_A], wo[W_A:], w_ple[i].astype(BF16), _row(ple_norm[i]), w_ple_gate[i].astype(BF16)]

        (qcat, kcat, v16, ckv, kpe, ga, gb, r, lw, km, vb, kkn, kka, xl) = _front_call(
            y_p.reshape(b, t, d), jnp.zeros((1, N_SH), F32), tabs_p, wts_p, sample=False)
        o_a = _p_attn_call(qcat, kcat, v16, nseq=b, seq_len=t)
        seq3 = lambda a: a.reshape(b, t, W_B)
        o_b, s_p = _rwkv_call(seq3(r), seq3(lw), seq3(km), seq3(vb), seq3(kkn), seq3(kka), seq3(gb),
                              jnp.zeros((b, H_B, N_B, N_B), F32), *rw_vecs, c=min(RW_CHUNK, t), nb=8)
        y_p = _back_call(y_p, o_a, ga, o_b.reshape(b * t, W_B), p_prompt[i].reshape(b * t, -1), back_wts)
        outs[0].append(ckv.reshape(b, t, KV_LORA)); outs[2].append(kpe.reshape(b, t, ROPE))
        outs[4].append(s_p); outs[6].append(xl)

        prev_proj = _matmul_call(state_shift[i], w_shift16)
        prev0 = jnp.pad(prev_proj[:, None, :], ((0, 0), (0, tq - 1), (0, 0))).reshape(bd * tq, N_SH)
        (qcat, _, _, ckv, kpe, ga, gb, r, lw, km, vb, kkn, kka, xl) = _front_call(
            y_s.reshape(bd, tq, d), prev0, tabs_s, wts_s, sample=True)
        q4 = qcat.reshape(bd, tq, H_A, LANES)
        qp_blk = q4[..., :ROPE].reshape(bd, tq * H_A, ROPE)
        eye_h = jnp.eye(H_A, dtype=BF16)
        qn_blk = (q4[..., ROPE:ROPE + NOPE][:, :, :, None, :] * eye_h[None, None, :, :, None]
                  ).reshape(bd, tq * H_A, W_A)
        npad = LANES
        cnew = jnp.pad(ckv.reshape(bd, tq, KV_LORA), ((0, 0), (0, npad - tq), (0, 0)))
        pnew_t = jnp.swapaxes(jnp.pad(kpe.reshape(bd, tq, ROPE), ((0, 0), (0, npad - tq), (0, 0))), 1, 2)
        o_full = _s_attn_call(page_table, qn_blk, qp_blk, cnew, pnew_t, ukt16, uv16, cache_ckv, cache_kpe_t, i)
        o5 = o_full.reshape(bd, tq, H_A, H_A, V_DIM)
        o_a = jnp.einsum('bqhhd->bqhd', o5).reshape(bd * tq, W_A)
        cpad = 16

        def pad_tok(a):
            return jnp.pad(a.reshape(bd, tq, -1), ((0, 0), (0, cpad - tq), (0, 0)))

        o_b16, s_s = _rwkv_call(pad_tok(r), pad_tok(lw), pad_tok(km), pad_tok(vb), pad_tok(kkn), pad_tok(kka),
                                pad_tok(gb), state_wkv[i], *rw_vecs, c=cpad, nb=4)
        o_b = o_b16[:, :tq].reshape(bd * tq, W_B)
        y_s = _back_call(y_s, o_a, ga, o_b, p_sample[i].reshape(bd * tq, -1), back_wts)
        outs[1].append(ckv.reshape(bd, tq, KV_LORA)); outs[3].append(kpe.reshape(bd, tq, ROPE))
        outs[5].append(s_s); outs[7].append(xl)
    st = [jnp.stack(o) for o in outs]
    return (y_p.reshape(b, t, d), y_s.reshape(bd, tq, d), st[0], st[1], st[2], st[3], st[4], st[5], st[6], st[7])
```

```python
import functools

import jax
import jax.numpy as jnp
from jax import lax
from jax.experimental import pallas as pl
from jax.experimental.pallas import tpu as pltpu

F32 = jnp.float32
BF16 = jnp.bfloat16

LANES = 128
H_A = 8
NOPE = 64
ROPE = 32
V_DIM = 64
Q_LORA = 384
KV_LORA = 256
ROPE_BASE = 10000.0
SCALE = (NOPE + ROPE) ** -0.5
LOG2E = 1.4426950408889634
H_B = 8
N_B = 64
W_A = H_A * V_DIM
W_B = H_B * N_B
DECAY_LORA = 64
ICLR_LORA = 64
LNX_EPS = 64e-5
EPS = 1e-6
NEG = -1e30

P_Q = 0
P_KV = P_Q + Q_LORA
P_KPE = P_KV + KV_LORA
P_GA = P_KPE + LANES
P_GB = P_GA + W_A
P_SH = P_GB + W_B
N_SH = 3 * W_B + 2 * LANES
P_TOT = P_SH + N_SH

HCAT = H_A * LANES
RW_CHUNK = 64
RW_GROUP = 4 * N_B
RW_NGROUPS = W_B // RW_GROUP
FRONT_TM = 512
FRONT_SUB = 256
BACK_TM = 1024
ATTN_TQ = 512
ATTN_TK = 512
PAGES_PER_GROUP = 32
PAGES_PER_SUB = 4
VMEM_LIMIT = 56 * 1024 * 1024


def _cparams(sem):
    return pltpu.CompilerParams(dimension_semantics=sem, vmem_limit_bytes=VMEM_LIMIT)


def _lane_iota(n=LANES):
    return lax.broadcasted_iota(jnp.int32, (1, n), 1)


def _rms(x, g, n):
    ms = jnp.sum(x * x, axis=-1, keepdims=True) * (1.0 / n)
    return x * lax.rsqrt(ms + EPS) * g


def _rope128(n, c, s1, s2):
    return n * c + pltpu.roll(n, LANES - ROPE // 2, axis=1) * s1 + pltpu.roll(n, ROPE // 2, axis=1) * s2


def _split2(x):
    hi = x.astype(BF16)
    lo = (x - hi.astype(F32)).astype(BF16)
    return hi, lo


def _split3(x):
    hi = x.astype(BF16)
    r1 = x - hi.astype(F32)
    mid = r1.astype(BF16)
    lo = (r1 - mid.astype(F32)).astype(BF16)
    return hi, mid, lo


def _nt(x16, y16):
    return lax.dot_general(x16, y16, (((1,), (1,)), ((), ())), preferred_element_type=F32)


def _front_body(x_ref, xlast_ref, prev0_ref, tab_ref, norm_in_ref, w_in_ref, mu_ref, qan_ref, wuq_ref, kvan_ref,
                wuk_ref, wuv_ref, gq_ref, gk_ref, gkp_ref, wd0_ref, wdu_ref, a0_ref, wiu_ref, kk_ref_, ka_ref,
                qcat_ref, kcat_ref, v_ref, ckv_ref, kpe_ref, ga_ref, gb_ref,
                r_ref, lw_ref, km_ref, vb_ref, kkn_ref, kka_ref, xl_ref,
                cur_scr, *, tm, sub_rows, tiles_per_seq, seq_len, sample):
    i = pl.program_id(0)
    d_model = x_ref.shape[-1]
    xl_ref[...] = _rms(xlast_ref[...], norm_in_ref[...], d_model)
    subs = [slice(r0, r0 + sub_rows) for r0 in range(0, tm, sub_rows)]

    if not sample:
        @pl.when(i % tiles_per_seq == 0)
        def _():
            cur_scr[7:8, :] = prev0_ref[...]
    else:
        cur_scr[7:8, :] = jnp.zeros((1, N_SH), F32)
    xn16s = []
    for rs in subs:
        xn16s.append(_rms(x_ref[rs, :], norm_in_ref[...], d_model).astype(BF16))
        cur_scr[8 + rs.start:8 + rs.stop, :] = jnp.dot(xn16s[-1], w_in_ref[:, P_SH:P_TOT],
                                                        preferred_element_type=F32)
    for rs, xn16 in zip(subs, xn16s):
        _front_rows(rs, xn16, prev0_ref, tab_ref, w_in_ref, mu_ref, qan_ref, wuq_ref, kvan_ref,
                    wuk_ref, wuv_ref, gq_ref, gk_ref, gkp_ref, wd0_ref, wdu_ref, a0_ref, wiu_ref, kk_ref_, ka_ref,
                    qcat_ref, kcat_ref, v_ref, ckv_ref, kpe_ref, ga_ref, gb_ref,
                    r_ref, lw_ref, km_ref, vb_ref, kkn_ref, kka_ref, cur_scr, seq_len=seq_len, sample=sample)
    if not sample:
        cur_scr[7:8, :] = cur_scr[7 + tm:8 + tm, :]


def _front_rows(rs, xn16, prev0_ref, tab_ref, w_in_ref, mu_ref, qan_ref, wuq_ref, kvan_ref,
                wuk_ref, wuv_ref, gq_ref, gk_ref, gkp_ref, wd0_ref, wdu_ref, a0_ref, wiu_ref, kk_ref_, ka_ref,
                qcat_ref, kcat_ref, v_ref, ckv_ref, kpe_ref, ga_ref, gb_ref,
                r_ref, lw_ref, km_ref, vb_ref, kkn_ref, kka_ref, cur_scr, *, seq_len, sample):
    nrows = rs.stop - rs.start

    def proj(lo, hi):
        return jnp.dot(xn16, w_in_ref[:, lo:hi], preferred_element_type=F32)

    c_tab, s1_tab, s2_tab = tab_ref[0, rs, :], tab_ref[1, rs, :], tab_ref[2, rs, :]
    lane = _lane_iota()
    is_rope = lane < ROPE

    c_q = _rms(proj(P_Q, P_KV), qan_ref[...], Q_LORA)
    q = jnp.dot(c_q.astype(BF16), wuq_ref[...], preferred_element_type=F32)
    gq = gq_ref[...]
    for h in range(H_A):
        qb = q[:, h * LANES:(h + 1) * LANES]
        sq = qb * qb
        s_r = jnp.sum(jnp.where(is_rope, sq, 0.0), axis=-1, keepdims=True)
        s_n = jnp.sum(jnp.where(is_rope, 0.0, sq), axis=-1, keepdims=True)
        inv = jnp.where(is_rope, lax.rsqrt(s_r * (1.0 / ROPE) + EPS), lax.rsqrt(s_n * (1.0 / NOPE) + EPS))
        qn = qb * inv * gq
        qcat_ref[rs, h * LANES:(h + 1) * LANES] = _rope128(qn, c_tab, s1_tab, s2_tab).astype(BF16)

    p_kv = proj(P_KV, P_GA)
    c_kv = _rms(p_kv[:, :KV_LORA], kvan_ref[...], KV_LORA)
    ckv_ref[rs, :] = c_kv
    c_kv16 = c_kv.astype(BF16)
    is_one = (_lane_iota(HCAT) % LANES) == V_DIM
    v_ref[rs, :] = jnp.where(is_one, 1.0, jnp.dot(c_kv16, wuv_ref[...], preferred_element_type=F32)).astype(BF16)
    kraw = jnp.dot(c_kv16, wuk_ref[...], preferred_element_type=F32)
    kp = p_kv[:, KV_LORA:]
    kp_n = kp * lax.rsqrt(jnp.sum(kp * kp, axis=-1, keepdims=True) * (1.0 / ROPE) + EPS) * gkp_ref[...]
    kp_r = _rope128(kp_n, c_tab, s1_tab, s2_tab)
    kpe_ref[rs, :] = kp_r[:, :ROPE]
    gk = gk_ref[...]
    for h in range(H_A):
        kb = kraw[:, h * LANES:(h + 1) * LANES]
        s_n = jnp.sum(kb * kb, axis=-1, keepdims=True)
        kn = kb * lax.rsqrt(s_n * (1.0 / NOPE) + EPS) * gk
        kcat_ref[rs, h * LANES:(h + 1) * LANES] = (kn + kp_r).astype(BF16)

    ga_ref[rs, :] = proj(P_GA, P_GB)
    gb_ref[rs, :] = proj(P_GB, P_SH)

    cur = cur_scr[8 + rs.start:8 + rs.stop, :]
    prev = cur_scr[7 + rs.start:7 + rs.stop, :]
    if sample:
        row = lax.broadcasted_iota(jnp.int32, (nrows, 1), 0) + rs.start
        prev = jnp.where(row % seq_len == 0, prev0_ref[rs, :], prev)
    sh = cur + (prev - cur) * mu_ref[...]
    r = sh[:, 0:W_B]
    k = sh[:, W_B:2 * W_B]
    v = sh[:, 2 * W_B:3 * W_B]
    xw = sh[:, 3 * W_B:3 * W_B + LANES]
    xa = sh[:, 3 * W_B + LANES:3 * W_B + 2 * LANES]
    z = wd0_ref[...] + jnp.dot(jnp.tanh(xw).astype(BF16), wdu_ref[...], preferred_element_type=F32)
    nz = -z
    softplus = jnp.maximum(nz, 0.0) + jnp.log1p(jnp.exp(-jnp.abs(nz)))
    w_log = -softplus - 0.5
    lw_ref[rs, :] = -jnp.exp(w_log)
    a = jax.nn.sigmoid(a0_ref[...] + jnp.dot(xa.astype(BF16), wiu_ref[...], preferred_element_type=F32))
    kk = k * kk_ref_[...]
    lo_half = lane < N_B
    for c in range(W_B // LANES):
        blk = kk[:, c * LANES:(c + 1) * LANES]
        sq = blk * blk
        s0 = jnp.sum(jnp.where(lo_half, sq, 0.0), axis=-1, keepdims=True)
        s1 = jnp.sum(jnp.where(lo_half, 0.0, sq), axis=-1, keepdims=True)
        den = jnp.where(lo_half, jnp.maximum(jnp.sqrt(s0), 1e-12), jnp.maximum(jnp.sqrt(s1), 1e-12))
        kkn = blk / den
        kkn_ref[rs, c * LANES:(c + 1) * LANES] = kkn
        kka_ref[rs, c * LANES:(c + 1) * LANES] = kkn * a[:, c * LANES:(c + 1) * LANES]
    r_ref[rs, :] = r
    vb_ref[rs, :] = v
    km_ref[rs, :] = k * (1.0 + (a - 1.0) * ka_ref[...])


def _front_call(x3d, prev0, tabs, wts, *, sample):
    nseq, seq_len, d = x3d.shape
    x2d = x3d.reshape(nseq * seq_len, d)
    xlast = x3d[:, seq_len - 1, :]
    n = x2d.shape[0]
    if sample:
        tm = n
        tiles_per_seq = 1
    else:
        tm = FRONT_TM
        assert seq_len % tm == 0
        tiles_per_seq = seq_len // tm
    grid = (n // tm,)
    ttab = tabs.shape[1]
    tab_blocks = ttab // tm
    row = lambda i: (i, 0)
    const2 = lambda i: (0, 0)
    prev_spec = pl.BlockSpec((tm, N_SH), row) if sample else pl.BlockSpec((1, N_SH), const2)
    xl_shape = jax.ShapeDtypeStruct((nseq, d), F32)
    xl_spec = pl.BlockSpec((nseq, d), const2)
    scratch = [pltpu.VMEM((tm + 8, N_SH), F32)]
    in_specs = [pl.BlockSpec((tm, d), row), pl.BlockSpec((nseq, d), const2), prev_spec,
                pl.BlockSpec((3, tm, LANES), lambda i: (0, i % tab_blocks, 0))]
    in_specs += [pl.BlockSpec(w.shape, const2, pipeline_mode=pl.Buffered(1)) for w in wts]
    out_shapes = [
        jax.ShapeDtypeStruct((n, HCAT), BF16), jax.ShapeDtypeStruct((n, HCAT), BF16),
        jax.ShapeDtypeStruct((n, HCAT), BF16), jax.ShapeDtypeStruct((n, KV_LORA), F32),
        jax.ShapeDtypeStruct((n, ROPE), F32), jax.ShapeDtypeStruct((n, W_A), F32),
        jax.ShapeDtypeStruct((n, W_B), F32),
    ] + [jax.ShapeDtypeStruct((n, W_B), F32)] * 6 + [xl_shape]
    out_specs = [pl.BlockSpec((tm, s.shape[1]), row) for s in out_shapes[:-1]] + [xl_spec]
    sub_rows = min(FRONT_SUB, tm)
    assert tm % sub_rows == 0
    body = functools.partial(_front_body, tm=tm, sub_rows=sub_rows, tiles_per_seq=tiles_per_seq, seq_len=seq_len,
                             sample=sample)
    return pl.pallas_call(
        body, grid=grid, in_specs=in_specs, out_specs=out_specs, out_shape=out_shapes,
        scratch_shapes=scratch, compiler_params=_cparams(("arbitrary",)),
        name="front_sample" if sample else "front_prompt",
    )(x2d, xlast, prev0, tabs, *wts)


def _matmul_body(a_ref, b_ref, o_ref):
    o_ref[...] = jnp.dot(a_ref[...].astype(BF16), b_ref[...], preferred_element_type=F32)


def _matmul_call(a, b16):
    m, k = a.shape
    n = b16.shape[1]
    return pl.pallas_call(
        _matmul_body, grid=(1,),
        in_specs=[pl.BlockSpec((m, k), lambda i: (0, 0)), pl.BlockSpec((k, n), lambda i: (0, 0))],
        out_specs=pl.BlockSpec((m, n), lambda i: (0, 0)),
        out_shape=jax.ShapeDtypeStruct((m, n), F32),
        compiler_params=_cparams(("arbitrary",)), name="prev_proj",
    )(a, b16)


def _p_attn_body(q_ref, k_ref, v_ref, o_ref, *, tq, tk):
    qi = pl.program_id(2)
    lane = _lane_iota()
    qs = [q_ref[:, hh * LANES:(hh + 1) * LANES] for hh in range(2)]
    nsub = tq // tk

    def scores(j, hh):
        kb = k_ref[pl.ds(pl.multiple_of(j * tk, tk), tk), hh * LANES:(hh + 1) * LANES]
        return _nt(qs[hh], kb)

    def values(j, hh):
        return v_ref[pl.ds(pl.multiple_of(j * tk, tk), tk), hh * LANES:(hh + 1) * LANES]

    def update(carry, s, vb):
        m, acc = carry
        m_new = jnp.maximum(m, jnp.max(s, axis=-1, keepdims=True))
        e16 = jnp.exp2(s - m_new).astype(BF16)
        return m_new, acc * jnp.exp2(m - m_new) + jnp.dot(e16, vb, preferred_element_type=F32)

    def step(j, carry):
        ss = [scores(j, hh) for hh in range(2)]
        return tuple(update(carry[hh], ss[hh], values(j, hh)) for hh in range(2))

    init =(jnp.full((tq, 1), NEG, F32), jnp.zeros((tq, LANES), F32))
    carry = lax.fori_loop(0, qi * nsub, step, (init, init))
    rows = lax.broadcasted_iota(jnp.int32, (tq, tk), 0)
    cols = lax.broadcasted_iota(jnp.int32, (tq, tk), 1)
    for d in range(nsub):
        j = qi * nsub + d
        causal = cols + d * tk <= rows
        ss = [jnp.where(causal, scores(j, hh), NEG) for hh in range(2)]
        carry = tuple(update(carry[hh], ss[hh], values(j, hh)) for hh in range(2))
    outs = [carry[hh][1] / carry[hh][1][:, V_DIM:V_DIM + 1] for hh in range(2)]
    o_ref[...] = jnp.where(lane < V_DIM, outs[0], pltpu.roll(outs[1], V_DIM, axis=1))


def _p_attn_call(qcat, kcat, v16, *, nseq, seq_len):
    n = qcat.shape[0]
    tq = min(ATTN_TQ, seq_len)
    tk = min(ATTN_TK, tq)
    assert seq_len % tq == 0 and tq % tk == 0
    nq = seq_len // tq
    grid = (nseq, H_A // 2, nq)
    return pl.pallas_call(
        functools.partial(_p_attn_body, tq=tq, tk=tk), grid=grid,
        in_specs=[pl.BlockSpec((tq, 2 * LANES), lambda b, h, i: (b * nq + i, h)),
                  pl.BlockSpec((seq_len, 2 * LANES), lambda b, h, i: (b, h)),
                  pl.BlockSpec((seq_len, 2 * LANES), lambda b, h, i: (b, h))],
        out_specs=pl.BlockSpec((tq, LANES), lambda b, h, i: (b * nq + i, h)),
        out_shape=jax.ShapeDtypeStruct((n, W_A), F32),
        compiler_params=_cparams(("arbitrary", "arbitrary", "arbitrary")), name="p_attn",
    )(qcat, kcat, v16)


def _s_attn_body(pt_ref, qn_ref, qp_ref, cnew_ref, pnew_ref, wkt_ref, wuv_ref, ckv_hbm, kpe_hbm, o_ref,
                 *scr, pg, sub, layer, ngroups):
    bufs = tuple((scr[2 * g], scr[2 * g + 1]) for g in range(ngroups))
    sem = scr[2 * ngroups]
    m_scr, l_scr, acc_scr = scr[2 * ngroups + 1:]
    b = pl.program_id(0)
    last = pl.num_programs(0) - 1
    nrow = qn_ref.shape[0]

    def start_group(bb, sl):
        cbuf, pbuf = bufs[sl]
        for t in range(pg):
            idx = pt_ref[bb, sl * pg + t]
            pltpu.make_async_copy(ckv_hbm.at[layer, idx], cbuf.at[t], sem.at[0, sl]).start()
            pltpu.make_async_copy(kpe_hbm.at[layer, idx], pbuf.at[t], sem.at[1, sl]).start()

    def wait_group(sl):
        cbuf, pbuf = bufs[sl]
        pltpu.make_async_copy(cbuf, cbuf, sem.at[0, sl]).wait()
        pltpu.make_async_copy(pbuf, pbuf, sem.at[1, sl]).wait()

    @pl.when(b == 0)
    def _():
        for g in range(ngroups):
            start_group(b, g)

    m_scr[...] = jnp.full(m_scr.shape, NEG, F32)
    l_scr[...] = jnp.zeros(l_scr.shape, F32)
    acc_scr[...] = jnp.zeros(acc_scr.shape, F32)

    def attend(c16s, pt16s, masks):
        krts = [_nt(wkt_ref[...], c16) for c16 in c16s]
        s_list = []
        for krt, pt16, mask in zip(krts, pt16s, masks):
            ssq = jnp.concatenate(
                [jnp.sum(jnp.square(krt[h * NOPE:(h + 1) * NOPE, :]), axis=0, keepdims=True) for h in range(H_A)],
                axis=0)
            rinv = lax.rsqrt(ssq * (1.0 / NOPE) + EPS)
            sraw = jnp.dot(qn_ref[...], krt.astype(BF16), preferred_element_type=F32)
            spe = jnp.dot(qp_ref[...], pt16, preferred_element_type=F32)
            s = sraw * jnp.concatenate([rinv] * (nrow // H_A), axis=0) + spe
            if mask is not None:
                s = jnp.where(mask, s, NEG)
            s_list.append(s)
        m, l, acc = m_scr[...], l_scr[...], acc_scr[...]
        for s, c16 in zip(s_list, c16s):
            m_new = jnp.maximum(m, jnp.max(s, axis=-1, keepdims=True))
            alpha = jnp.exp(m - m_new)
            e = jnp.exp(s - m_new)
            l = l * alpha + jnp.sum(e, axis=-1, keepdims=True)
            acc = acc * alpha + jnp.dot(e.astype(BF16), c16, preferred_element_type=F32)
            m = m_new
        l_scr[...] = l
        acc_scr[...] = acc
        m_scr[...] = m

    def attend_buffer(sl, with_new):
        cbuf, pbuf = bufs[sl]
        page = cbuf.shape[1]
        c16s, pt16s = [], []
        for g in range(pg // sub):
            c16s.append(cbuf[g * sub:(g + 1) * sub].reshape(sub * page, KV_LORA).astype(BF16))
            pt16s.append(jnp.concatenate([pbuf[t].astype(BF16) for t in range(g * sub, (g + 1) * sub)], axis=1))
        masks = [None] * len(c16s)
        if with_new:
            nnew = cnew_ref.shape[0]
            rows = lax.broadcasted_iota(jnp.int32, (nrow, nnew), 0) // H_A
            cols = lax.broadcasted_iota(jnp.int32, (nrow, nnew), 1)
            c16s.append(cnew_ref[...].astype(BF16))
            pt16s.append(pnew_ref[...].astype(BF16))
            masks.append(cols <= rows)
        attend(c16s, pt16s, masks)

    nxt = jnp.minimum(b + 1, last)
    for g in range(ngroups):
        wait_group(g)
        attend_buffer(g, with_new=(g == ngroups - 1))
        start_group(nxt, g)

    o_lat = acc_scr[...] / l_scr[...]
    hi, lo = _split2(o_lat)
    o_ref[...] = (jnp.dot(hi, wuv_ref[...], preferred_element_type=F32)
                  + jnp.dot(lo, wuv_ref[...], preferred_element_type=F32))

    @pl.when(b == last)
    def _():
        for g in range(ngroups):
            wait_group(g)


def _s_attn_call(page_table, qn_blk, qp_blk, cnew, pnew_t, wkt16, wuv16, cache_ckv, cache_kpe_t, layer):
    bd, nrow, _ = qn_blk.shape
    n_pages = page_table.shape[1]
    page = cache_ckv.shape[2]
    pg = min(PAGES_PER_GROUP, n_pages)
    sub = min(PAGES_PER_SUB, pg)
    assert n_pages % pg == 0 and pg % sub == 0
    ngroups = n_pages // pg
    nnew = cnew.shape[1]
    grid = (bd,)
    seq3 = lambda b, pt: (b, 0, 0)
    const2 = lambda b, pt: (0, 0)

    in_specs = [pl.BlockSpec((None, nrow, W_A), seq3), pl.BlockSpec((None, nrow, ROPE), seq3),
                pl.BlockSpec((None, nnew, KV_LORA), seq3), pl.BlockSpec((None, ROPE, nnew), seq3),
                pl.BlockSpec(wkt16.shape, const2), pl.BlockSpec(wuv16.shape, const2),
                pl.BlockSpec(memory_space=pl.ANY), pl.BlockSpec(memory_space=pl.ANY)]
    gs = pltpu.PrefetchScalarGridSpec(
        num_scalar_prefetch=1, grid=grid, in_specs=in_specs,
        out_specs=pl.BlockSpec((None, nrow, W_A), seq3),
        scratch_shapes=[pltpu.VMEM((pg, page, KV_LORA), F32), pltpu.VMEM((pg, ROPE, page), F32)] * ngroups
        + [pltpu.SemaphoreType.DMA((2, ngroups)),
           pltpu.VMEM((nrow, 1), F32), pltpu.VMEM((nrow, 1), F32), pltpu.VMEM((nrow, KV_LORA), F32)])
    return pl.pallas_call(
        functools.partial(_s_attn_body, pg=pg, sub=sub, layer=layer, ngroups=ngroups), grid_spec=gs,
        out_shape=jax.ShapeDtypeStruct((bd, nrow, W_A), F32),
        compiler_params=_cparams(("arbitrary",)), name="s_attn",
    )(page_table, qn_blk, qp_blk, cnew, pnew_t, wkt16, wuv16, cache_ckv, cache_kpe_t)


def _rwkv_body(r_ref, lw_ref, km_ref, v_ref, kk_ref, kka_ref, gb_ref, s0_ref, rk_ref, lnw_ref, lnb_ref,
               o_ref, sout_ref, s_scr, *, c, nb):
    ci = pl.program_id(1)
    nc = pl.num_programs(1)
    g4 = RW_GROUP
    items = [(s, g) for s in range(nb) for g in range(RW_NGROUPS)]

    rr = lax.broadcasted_iota(jnp.int32, (g4, g4), 0) // N_B
    cc = lax.broadcasted_iota(jnp.int32, (g4, g4), 1) // N_B
    blockmask = rr == cc
    ones_blk = blockmask.astype(BF16)
    t_idx = lax.broadcasted_iota(jnp.int32, (c, g4), 0)
    i_idx = lax.broadcasted_iota(jnp.int32, (c, g4), 1) % N_B
    low_strict = i_idx < t_idx
    low_incl = i_idx <= t_idx
    eye = (i_idx == t_idx).astype(F32)
    lane_half = [_lane_iota() < N_B, _lane_iota() >= N_B]

    def bd_rows(y):
        y16 = y.astype(BF16)
        zero = jnp.zeros((c, LANES), BF16)
        blocks = []
        for h in range(g4 // N_B):
            col = y16[:, (h // 2) * LANES:(h // 2 + 1) * LANES]
            piece = jnp.where(lane_half[h % 2], col, zero)
            blocks.append(jnp.concatenate([piece, zero] if h < 2 else [zero, piece], axis=1))
            if c < N_B:
                blocks.append(jnp.zeros((N_B - c, g4), BF16))
        return jnp.concatenate(blocks, axis=0)

    def mmh(x, y):
        return jnp.dot(x.astype(BF16), bd_rows(y), preferred_element_type=F32)

    @pl.when(ci == 0)
    def _():
        s_scr[...] = jnp.zeros(s_scr.shape, F32)
        for s in range(nb):
            for h in range(H_B):
                g, jh = divmod(h, 4)
                s_scr[s, g, jh * N_B:(jh + 1) * N_B, jh * N_B:(jh + 1) * N_B] = s0_ref[s, h]

    tri = (lax.broadcasted_iota(jnp.int32, (c, c), 1) <= lax.broadcasted_iota(jnp.int32, (c, c), 0)).astype(BF16)
    g_seq = []
    for s in range(nb):
        g_seq.append(sum(jnp.dot(tri, part, preferred_element_type=F32) for part in _split3(lw_ref[s])))

    st = []
    for (s, g) in items:
        sl = slice(g * g4, (g + 1) * g4)
        lw = lw_ref[s, :, sl]
        gc = g_seq[s][:, sl]
        big = jnp.exp(gc)
        ginv = jnp.exp(-gc)
        at = -kk_ref[s, :, sl] * jnp.exp(gc - lw)
        bt = kka_ref[s, :, sl] * ginv
        kt = km_ref[s, :, sl] * ginv
        rt = r_ref[s, :, sl] * big
        s_old = s_scr[s, g]
        st.append(dict(sl=sl, s=s, g=g, bt=bt, kt=kt, g_last=big[c - 1:c, :], s_old=s_old,
                       lhs2=jnp.concatenate([at, rt], axis=0).astype(BF16), v=v_ref[s, :, sl]))

    for d in st:
        x1 = _nt(d["lhs2"], bd_rows(d["bt"]))
        x2 = _nt(d["lhs2"], bd_rows(d["kt"]))
        d["x3"] = _nt(d["lhs2"], d["s_old"].astype(BF16))
        d["a_ab"] = jnp.where(low_strict, x1[:c], 0.0)
        d["a_rb"] = jnp.where(low_incl, x1[c:], 0.0)
        d["a_ak"] = jnp.where(low_strict, x2[:c], 0.0)
        d["a_rk"] = jnp.where(low_incl, x2[c:], 0.0)
    for d in st:
        x4 = jnp.dot(jnp.concatenate([d["a_ak"], d["a_rk"]], axis=0).astype(BF16), bd_rows(d["v"]),
                     preferred_element_type=F32)
        d["b"] = x4[:c] + d["x3"][:c]
        d["ypart"] = x4[c:] + d["x3"][c:]

    if c <= 16:
        for d in st:
            d["tinv"] = eye + d["a_ab"]
            d["pw"] = d["a_ab"]
        n = 1
        while 2 * n < c:
            for d in st:
                d["pw"] = mmh(d["pw"], d["pw"])
            for d in st:
                d["tinv"] = mmh(d["tinv"], eye + d["pw"])
            n *= 2
        for d in st:
            d["u"] = mmh(d["tinv"], d["b"])
    else:
        same16 = (i_idx // 16) == (t_idx // 16)
        for d in st:
            dg = jnp.where(same16, d["a_ab"], 0.0)
            d["lo"] = d["a_ab"] - dg
            d["td"] = eye + dg
            d["pw"] = dg
        for _ in range(3):
            for d in st:
                d["pw"] = mmh(d["pw"], d["pw"])
            for d in st:
                d["td"] = mmh(d["td"], eye + d["pw"])
        for d in st:
            d["nn"] = mmh(d["td"], d["lo"])
            d["w"] = mmh(d["td"], d["b"])
        for d in st:
            d["n2"] = mmh(d["nn"], d["nn"])
        for d in st:
            d["w2"] = d["w"] + mmh(d["n2"], d["w"])
        for d in st:
            d["u"] = d["w2"] + mmh(d["nn"], d["w2"])

    for d in st:
        d["y"] = d["ypart"] + mmh(d["a_rb"], d["u"])
        lhs_t = jnp.concatenate([d["v"], d["u"]], axis=0).astype(BF16)
        rhs_t = jnp.concatenate([d["kt"] * d["g_last"], d["bt"] * d["g_last"]], axis=0).astype(BF16)
        upd = lax.dot_general(lhs_t, rhs_t, (((0,), (0,)), ((), ())), preferred_element_type=F32)
        s_scr[d["s"], d["g"]] = d["s_old"] * d["g_last"] + jnp.where(blockmask, upd, 0.0)

    def head_sums(xs):
        parts = []
        for x in xs:
            parts.extend(_split2(x))
        res = jnp.dot(jnp.concatenate(parts, axis=0), ones_blk, preferred_element_type=F32)
        return [res[(2 * k) * c:(2 * k + 1) * c] + res[(2 * k + 1) * c:(2 * k + 2) * c] for k in range(len(xs))]

    rkr = [r_ref[d["s"], :, d["sl"]] * km_ref[d["s"], :, d["sl"]] * rk_ref[:, d["sl"]] for d in st]
    sums = head_sums([d["y"] for d in st] + rkr)
    devs = [d["y"] - sums[k] * (1.0 / N_B) for k, d in enumerate(st)]
    var_sums = head_sums([dv * dv for dv in devs])
    for k, d in enumerate(st):
        sl = d["sl"]
        yn = devs[k] * lax.rsqrt(var_sums[k] * (1.0 / N_B) + LNX_EPS) * lnw_ref[:, sl] + lnb_ref[:, sl]
        bonus = sums[len(st) + k] * d["v"]
        gb = gb_ref[d["s"], :, sl]
        o_ref[d["s"], :, sl] = (yn + bonus) * (gb * jax.nn.sigmoid(gb))

    @pl.when(ci == nc - 1)
    def _():
        for s in range(nb):
            for h in range(H_B):
                g, jh = divmod(h, 4)
                sout_ref[s, h] = s_scr[s, g, jh * N_B:(jh + 1) * N_B, jh * N_B:(jh + 1) * N_B]


def _rwkv_call(r, lw, km, v, kk, kka, gb, s0, rk, lnw, lnb, *, c, nb):
    nseq, t, _ = r.shape
    nc = t // c
    assert nseq % nb == 0 and t % c == 0
    tok = pl.BlockSpec((nb, c, W_B), lambda b, i: (b, i, 0))
    vec = pl.BlockSpec((1, W_B), lambda b, i: (0, 0))
    st = pl.BlockSpec((nb, H_B, N_B, N_B), lambda b, i: (b, 0, 0, 0))
    return pl.pallas_call(
        functools.partial(_rwkv_body, c=c, nb=nb), grid=(nseq // nb, nc),
        in_specs=[tok] * 7 + [st, vec, vec, vec],
        out_specs=[tok, st],
        out_shape=[jax.ShapeDtypeStruct((nseq, t, W_B), F32), jax.ShapeDtypeStruct((nseq, H_B, N_B, N_B), F32)],
        scratch_shapes=[pltpu.VMEM((nb, RW_NGROUPS, RW_GROUP, RW_GROUP), F32)],
        compiler_params=_cparams(("arbitrary", "arbitrary")), name=f"rwkv_c{c}",
    )(r, lw, km, v, kk, kka, gb, s0, rk, lnw, lnb)


def _back_body(x_ref, oa_ref, ga_ref, ob_ref, p_ref, woa_ref, wob_ref, wple_ref, pn_ref, wg_ref, y_ref):
    ga = ga_ref[...]
    mixed_a = oa_ref[...] * (ga * jax.nn.sigmoid(ga))
    x1 = (x_ref[...] + jnp.dot(mixed_a.astype(BF16), woa_ref[...], preferred_element_type=F32)
          + jnp.dot(ob_ref[...].astype(BF16), wob_ref[...], preferred_element_type=F32))
    xg = _rms(x1, pn_ref[...], x1.shape[-1])
    gate = jax.nn.sigmoid(jnp.dot(xg.astype(BF16), wg_ref[...], preferred_element_type=F32))
    y_ref[...] = x1 + jnp.dot(p_ref[...].astype(BF16), wple_ref[...], preferred_element_type=F32) * gate


def _back_call(x2d, oa, ga, ob, p2d, wts):
    n, d = x2d.shape
    tm = min(BACK_TM, n)
    row = lambda i: (i, 0)
    const2 = lambda i: (0, 0)
    ins = [x2d, oa, ga, ob, p2d]
    return pl.pallas_call(
        _back_body, grid=(n // tm,),
        in_specs=[pl.BlockSpec((tm, a.shape[1]), row) for a in ins] + [pl.BlockSpec(w.shape, const2) for w in wts],
        out_specs=pl.BlockSpec((tm, d), row),
        out_shape=jax.ShapeDtypeStruct((n, d), F32),
        compiler_params=_cparams(("arbitrary",)), name="back",
    )(*ins, *wts)


def _pad_cols(a, width):
    return jnp.pad(a, ((0, 0), (0, width - a.shape[1])))


def _row(vec):
    return vec.reshape(1, -1).astype(F32)


def _rope_tabs(pos):
    inv = ROPE_BASE ** (-jnp.arange(0, ROPE, 2, dtype=F32) / ROPE)
    ang = pos.astype(F32)[:, None] * inv[None, :]
    cos, sin = jnp.cos(ang), jnp.sin(ang)
    t = pos.shape[0]
    z16 = jnp.zeros((t, ROPE // 2), F32)
    c = jnp.concatenate([cos, cos, jnp.ones((t, NOPE), F32), jnp.zeros((t, LANES - ROPE - NOPE), F32)], axis=1)
    s1 = _pad_cols(jnp.concatenate([-sin, z16], axis=1), LANES)
    s2 = _pad_cols(jnp.concatenate([z16, sin], axis=1), LANES)
    return jnp.stack([c, s1, s2])


def _layer_weights(i, norm_in, w_in, mu_shift, q_a_norm, w_uq, kv_a_norm, w_ukv, g_q_nope, g_q_pe, g_k_nope,
                   g_k_pe, w_decay0, w_decay_up, a0, w_iclr_up, k_k, k_a):
    w = w_in[i]
    c_q, c_kv, c_kpe = 0, Q_LORA, Q_LORA + KV_LORA
    c_ga = c_kpe + ROPE
    c_gb = c_ga + W_A
    c_sh = c_gb + W_B
    c_xw = c_sh + 3 * W_B
    c_xa = c_xw + DECAY_LORA
    w_perm = jnp.concatenate([
        w[:, c_q:c_kpe], _pad_cols(w[:, c_kpe:c_ga], LANES), w[:, c_ga:c_sh], w[:, c_sh:c_xw],
        _pad_cols(w[:, c_xw:c_xa], LANES), _pad_cols(w[:, c_xa:], LANES)], axis=1).astype(BF16)
    mu = mu_shift[i][None, :]
    mu_perm = jnp.concatenate([mu[:, :3 * W_B], _pad_cols(mu[:, 3 * W_B:3 * W_B + DECAY_LORA], LANES),
                               _pad_cols(mu[:, 3 * W_B + DECAY_LORA:], LANES)], axis=1)
    uq = w_uq[i]
    uq_cat = jnp.concatenate([uq[..., NOPE:], uq[..., :NOPE],
                              jnp.zeros((Q_LORA, H_A, LANES - NOPE - ROPE), F32)], axis=-1)
    uq_cat = uq_cat.reshape(Q_LORA, HCAT).astype(BF16)
    ukv = w_ukv[i]
    uk = ukv[..., :NOPE]
    uk_cat = jnp.concatenate([jnp.zeros((KV_LORA, H_A, ROPE), F32), uk,
                              jnp.zeros((KV_LORA, H_A, LANES - NOPE - ROPE), F32)], axis=-1)
    uk_cat = uk_cat.reshape(KV_LORA, HCAT).astype(BF16)
    uv = ukv[..., NOPE:].reshape(KV_LORA, W_A).astype(BF16)
    uv_cat = jnp.concatenate([ukv[..., NOPE:], jnp.zeros((KV_LORA, H_A, LANES - V_DIM), F32)], axis=-1)
    uv_cat = uv_cat.reshape(KV_LORA, HCAT).astype(BF16)
    ukt = jnp.transpose(uk.reshape(KV_LORA, H_A * NOPE)).astype(BF16)
    zpad = jnp.zeros((LANES - NOPE - ROPE,), F32)
    gq_prompt = jnp.concatenate([g_q_pe[i], g_q_nope[i], zpad]) * (SCALE * LOG2E)
    gq_sample = jnp.concatenate([g_q_pe[i], g_q_nope[i] * g_k_nope[i], zpad]) * SCALE
    gk = jnp.concatenate([jnp.zeros((ROPE,), F32), g_k_nope[i], zpad])
    gkp = jnp.concatenate([g_k_pe[i], jnp.zeros((LANES - ROPE,), F32)])
    wdu = jnp.pad(w_decay_up[i], ((0, LANES - DECAY_LORA), (0, 0))).astype(BF16)
    wiu = jnp.pad(w_iclr_up[i], ((0, LANES - ICLR_LORA), (0, 0))).astype(BF16)

    def front_wts(gq):
        return [_row(norm_in[i]), w_perm, mu_perm, _row(q_a_norm[i]), uq_cat, _row(kv_a_norm[i]), uk_cat, uv_cat,
                _row(gq), _row(gk), _row(gkp), _row(w_decay0[i]), wdu, _row(a0[i]), wiu, _row(k_k[i]), _row(k_a[i])]

    return front_wts(gq_prompt), front_wts(gq_sample), w_perm[:, P_SH:], ukt, uv


def kernel(x_prompt, x_sample, p_prompt, p_sample, cache_ckv, cache_kpe, state_wkv, state_shift, page_table,
           norm_in, w_in, mu_shift, q_a_norm, w_uq, kv_a_norm, w_ukv, g_q_nope, g_q_pe, g_k_nope, g_k_pe,
           w_decay0, w_decay_up, a0, w_iclr_up, k_k, k_a, r_k, lnx_w, lnx_b, w_out, w_ple, ple_norm, w_ple_gate):
    b, t, d = x_prompt.shape
    bd, tq, _ = x_sample.shape
    depth = w_in.shape[0]
    page = cache_ckv.shape[2]
    past_len = page_table.shape[1] * page
    tabs_p = _rope_tabs(jnp.arange(t))
    tabs_s = jnp.tile(_rope_tabs(past_len + jnp.arange(tq)), (1, bd, 1))
    cache_kpe_t = jnp.swapaxes(cache_kpe, 2, 3)
    y_p = x_prompt.reshape(b * t, d)
    y_s = x_sample.reshape(bd * tq, d)
    outs = [[] for _ in range(8)]
    for i in range(depth):
        wts_p, wts_s, w_shift16, ukt16, uv16 = _layer_weights(
            i, norm_in, w_in, mu_shift, q_a_norm, w_uq, kv_a_norm, w_ukv, g_q_nope, g_q_pe, g_k_nope, g_k_pe,
            w_decay0, w_decay_up, a0, w_iclr_up, k_k, k_a)
        rw_vecs = [_row(r_k[i]), _row(lnx_w[i]), _row(lnx_b[i])]
        wo = w_out[i].astype(BF16)
        back_wts = [wo[:W_A], wo[W_A:], w_ple[i].astype(BF16), _row(ple_norm[i]), w_ple_gate[i].astype(BF16)]

        (qcat, kcat, v16, ckv, kpe, ga, gb, r, lw, km, vb, kkn, kka, xl) = _front_call(
            y_p.reshape(b, t, d), jnp.zeros((1, N_SH), F32), tabs_p, wts_p, sample=False)
        o_a = _p_attn_call(qcat, kcat, v16, nseq=b, seq_len=t)
        seq3 = lambda a: a.reshape(b, t, W_B)
        o_b, s_p = _rwkv_call(seq3(r), seq3(lw), seq3(km), seq3(vb), seq3(kkn), seq3(kka), seq3(gb),
                              jnp.zeros((b, H_B, N_B, N_B), F32), *rw_vecs, c=min(RW_CHUNK, t), nb=8)
        y_p = _back_call(y_p, o_a, ga, o_b.reshape(b * t, W_B), p_prompt[i].reshape(b * t, -1), back_wts)
        outs[0].append(ckv.reshape(b, t, KV_LORA)); outs[2].append(kpe.reshape(b, t, ROPE))
        outs[4].append(s_p); outs[6].append(xl)

        prev_proj = _matmul_call(state_shift[i], w_shift16)
        prev0 = jnp.pad(prev_proj[:, None, :], ((0, 0), (0, tq - 1), (0, 0))).reshape(bd * tq, N_SH)
        (qcat, _, _, ckv, kpe, ga, gb, r, lw, km, vb, kkn, kka, xl) = _front_call(
            y_s.reshape(bd, tq, d), prev0, tabs_s, wts_s, sample=True)
        q4 = qcat.reshape(bd, tq, H_A, LANES)
        qp_blk = q4[..., :ROPE].reshape(bd, tq * H_A, ROPE)
        eye_h = jnp.eye(H_A, dtype=BF16)
        qn_blk = (q4[..., ROPE:ROPE + NOPE][:, :, :, None, :] * eye_h[None, None, :, :, None]
                  ).reshape(bd, tq * H_A, W_A)
        npad = LANES
        cnew = jnp.pad(ckv.reshape(bd, tq, KV_LORA), ((0, 0), (0, npad - tq), (0, 0)))
        pnew_t = jnp.swapaxes(jnp.pad(kpe.reshape(bd, tq, ROPE), ((0, 0), (0, npad - tq), (0, 0))), 1, 2)
        o_full = _s_attn_call(page_table, qn_blk, qp_blk, cnew, pnew_t, ukt16, uv16, cache_ckv, cache_kpe_t, i)
        o5 = o_full.reshape(bd, tq, H_A, H_A, V_DIM)
        o_a = jnp.einsum('bqhhd->bqhd', o5).reshape(bd * tq, W_A)
        cpad = 16

        def pad_tok(a):
            return jnp.pad(a.reshape(bd, tq, -1), ((0, 0), (0, cpad - tq), (0, 0)))

        o_b16, s_s = _rwkv_call(pad_tok(r), pad_tok(lw), pad_tok(km), pad_tok(vb), pad_tok(kkn), pad_tok(kka),
                                pad_tok(gb), state_wkv[i], *rw_vecs, c=cpad, nb=4)
        o_b = o_b16[:, :tq].reshape(bd * tq, W_B)
        y_s = _back_call(y_s, o_a, ga, o_b, p_sample[i].reshape(bd * tq, -1), back_wts)
        outs[1].append(ckv.reshape(bd, tq, KV_LORA)); outs[3].append(kpe.reshape(bd, tq, ROPE))
        outs[5].append(s_s); outs[7].append(xl)
    st = [jnp.stack(o) for o in outs]
    return (y_p.reshape(b, t, d), y_s.reshape(bd, tq, d), st[0], st[1], st[2], st[3], st[4], st[5], st[6], st[7])
```

```python
import functools

import jax
import jax.numpy as jnp
from jax import lax
from jax.experimental import pallas as pl
from jax.experimental.pallas import tpu as pltpu

F32 = jnp.float32
BF16 = jnp.bfloat16

LANES = 128
H_A = 8
NOPE = 64
ROPE = 32
V_DIM = 64
Q_LORA = 384
KV_LORA = 256
ROPE_BASE = 10000.0
SCALE = (NOPE + ROPE) ** -0.5
LOG2E = 1.4426950408889634
H_B = 8
N_B = 64
W_A = H_A * V_DIM
W_B = H_B * N_B
DECAY_LORA = 64
ICLR_LORA = 64
LNX_EPS = 64e-5
EPS = 1e-6
NEG = -1e30

P_Q = 0
P_KV = P_Q + Q_LORA
P_KPE = P_KV + KV_LORA
P_GA = P_KPE + LANES
P_GB = P_GA + W_A
P_SH = P_GB + W_B
N_SH = 3 * W_B + 2 * LANES
P_TOT = P_SH + N_SH

HCAT = H_A * LANES
RW_CHUNK = 64
RW_GROUP = 4 * N_B
RW_NGROUPS = W_B // RW_GROUP
FRONT_TM = 512
FRONT_SUB = 256
BACK_TM = 1024
ATTN_TQ = 512
ATTN_TK = 512
PAGES_PER_GROUP = 32
PAGES_PER_SUB = 4
VMEM_LIMIT = 56 * 1024 * 1024


def _cparams(sem):
    return pltpu.CompilerParams(dimension_semantics=sem, vmem_limit_bytes=VMEM_LIMIT)


def _lane_iota(n=LANES):
    return lax.broadcasted_iota(jnp.int32, (1, n), 1)


def _rms(x, g, n):
    ms = jnp.sum(x * x, axis=-1, keepdims=True) * (1.0 / n)
    return x * lax.rsqrt(ms + EPS) * g


def _rope128(n, c, s1, s2):
    return n * c + pltpu.roll(n, LANES - ROPE // 2, axis=1) * s1 + pltpu.roll(n, ROPE // 2, axis=1) * s2


def _split2(x):
    hi = x.astype(BF16)
    lo = (x - hi.astype(F32)).astype(BF16)
    return hi, lo


def _split3(x):
    hi = x.astype(BF16)
    r1 = x - hi.astype(F32)
    mid = r1.astype(BF16)
    lo = (r1 - mid.astype(F32)).astype(BF16)
    return hi, mid, lo


def _nt(x16, y16):
    return lax.dot_general(x16, y16, (((1,), (1,)), ((), ())), preferred_element_type=F32)


def _front_body(x_ref, xlast_ref, prev0_ref, tab_ref, norm_in_ref, w_in_ref, mu_ref, qan_ref, wuq_ref, kvan_ref,
                wuk_ref, wuv_ref, gq_ref, gk_ref, gkp_ref, wd0_ref, wdu_ref, a0_ref, wiu_ref, kk_ref_, ka_ref,
                qcat_ref, kcat_ref, v_ref, ckv_ref, kpe_ref, ga_ref, gb_ref,
                r_ref, lw_ref, km_ref, vb_ref, kkn_ref, kka_ref, xl_ref,
                cur_scr, *, tm, sub_rows, tiles_per_seq, seq_len, sample):
    i = pl.program_id(0)
    d_model = x_ref.shape[-1]
    xl_ref[...] = _rms(xlast_ref[...], norm_in_ref[...], d_model)
    subs = [slice(r0, r0 + sub_rows) for r0 in range(0, tm, sub_rows)]

    if not sample:
        @pl.when(i % tiles_per_seq == 0)
        def _():
            cur_scr[7:8, :] = prev0_ref[...]
    else:
        cur_scr[7:8, :] = jnp.zeros((1, N_SH), F32)
    xn16s = []
    for rs in subs:
        xn16s.append(_rms(x_ref[rs, :], norm_in_ref[...], d_model).astype(BF16))
        cur_scr[8 + rs.start:8 + rs.stop, :] = jnp.dot(xn16s[-1], w_in_ref[:, P_SH:P_TOT],
                                                        preferred_element_type=F32)
    for rs, xn16 in zip(subs, xn16s):
        _front_rows(rs, xn16, prev0_ref, tab_ref, w_in_ref, mu_ref, qan_ref, wuq_ref, kvan_ref,
                    wuk_ref, wuv_ref, gq_ref, gk_ref, gkp_ref, wd0_ref, wdu_ref, a0_ref, wiu_ref, kk_ref_, ka_ref,
                    qcat_ref, kcat_ref, v_ref, ckv_ref, kpe_ref, ga_ref, gb_ref,
                    r_ref, lw_ref, km_ref, vb_ref, kkn_ref, kka_ref, cur_scr, seq_len=seq_len, sample=sample)
    if not sample:
        cur_scr[7:8, :] = cur_scr[7 + tm:8 + tm, :]


def _front_rows(rs, xn16, prev0_ref, tab_ref, w_in_ref, mu_ref, qan_ref, wuq_ref, kvan_ref,
                wuk_ref, wuv_ref, gq_ref, gk_ref, gkp_ref, wd0_ref, wdu_ref, a0_ref, wiu_ref, kk_ref_, ka_ref,
                qcat_ref, kcat_ref, v_ref, ckv_ref, kpe_ref, ga_ref, gb_ref,
                r_ref, lw_ref, km_ref, vb_ref, kkn_ref, kka_ref, cur_scr, *, seq_len, sample):
    nrows = rs.stop - rs.start

    def proj(lo, hi):
        return jnp.dot(xn16, w_in_ref[:, lo:hi], preferred_element_type=F32)

    c_tab, s1_tab, s2_tab = tab_ref[0, rs, :], tab_ref[1, rs, :], tab_ref[2, rs, :]
    lane = _lane_iota()
    is_rope = lane < ROPE

    c_q = _rms(proj(P_Q, P_KV), qan_ref[...], Q_LORA)
    q = jnp.dot(c_q.astype(BF16), wuq_ref[...], preferred_element_type=F32)
    gq = gq_ref[...]
    for h in range(H_A):
        qb = q[:, h * LANES:(h + 1) * LANES]
        sq = qb * qb
        s_r = jnp.sum(jnp.where(is_rope, sq, 0.0), axis=-1, keepdims=True)
        s_n = jnp.sum(jnp.where(is_rope, 0.0, sq), axis=-1, keepdims=True)
        inv = jnp.where(is_rope, lax.rsqrt(s_r * (1.0 / ROPE) + EPS), lax.rsqrt(s_n * (1.0 / NOPE) + EPS))
        qn = qb * inv * gq
        qcat_ref[rs, h * LANES:(h + 1) * LANES] = _rope128(qn, c_tab, s1_tab, s2_tab).astype(BF16)

    p_kv = proj(P_KV, P_GA)
    c_kv = _rms(p_kv[:, :KV_LORA], kvan_ref[...], KV_LORA)
    ckv_ref[rs, :] = c_kv
    c_kv16 = c_kv.astype(BF16)
    is_one = (_lane_iota(HCAT) % LANES) == V_DIM
    v_ref[rs, :] = jnp.where(is_one, 1.0, jnp.dot(c_kv16, wuv_ref[...], preferred_element_type=F32)).astype(BF16)
    kraw = jnp.dot(c_kv16, wuk_ref[...], preferred_element_type=F32)
    kp = p_kv[:, KV_LORA:]
    kp_n = kp * lax.rsqrt(jnp.sum(kp * kp, axis=-1, keepdims=True) * (1.0 / ROPE) + EPS) * gkp_ref[...]
    kp_r = _rope128(kp_n, c_tab, s1_tab, s2_tab)
    kpe_ref[rs, :] = kp_r[:, :ROPE]
    gk = gk_ref[...]
    for h in range(H_A):
        kb = kraw[:, h * LANES:(h + 1) * LANES]
        s_n = jnp.sum(kb * kb, axis=-1, keepdims=True)
        kn = kb * lax.rsqrt(s_n * (1.0 / NOPE) + EPS) * gk
        kcat_ref[rs, h * LANES:(h + 1) * LANES] = (kn + kp_r).astype(BF16)

    ga_ref[rs, :] = proj(P_GA, P_GB)
    gb_ref[rs, :] = proj(P_GB, P_SH)

    cur = cur_scr[8 + rs.start:8 + rs.stop, :]
    prev = cur_scr[7 + rs.start:7 + rs.stop, :]
    if sample:
        row = lax.broadcasted_iota(jnp.int32, (nrows, 1), 0) + rs.start
        prev = jnp.where(row % seq_len == 0, prev0_ref[rs, :], prev)
    sh = cur + (prev - cur) * mu_ref[...]
    r = sh[:, 0:W_B]
    k = sh[:, W_B:2 * W_B]
    v = sh[:, 2 * W_B:3 * W_B]
    xw = sh[:, 3 * W_B:3 * W_B + LANES]
    xa = sh[:, 3 * W_B + LANES:3 * W_B + 2 * LANES]
    z = wd0_ref[...] + jnp.dot(jnp.tanh(xw).astype(BF16), wdu_ref[...], preferred_element_type=F32)
    nz = -z
    softplus = jnp.maximum(nz, 0.0) + jnp.log1p(jnp.exp(-jnp.abs(nz)))
    w_log = -softplus - 0.5
    lw_ref[rs, :] = -jnp.exp(w_log)
    a = jax.nn.sigmoid(a0_ref[...] + jnp.dot(xa.astype(BF16), wiu_ref[...], preferred_element_type=F32))
    kk = k * kk_ref_[...]
    lo_half = lane < N_B
    for c in range(W_B // LANES):
        blk = kk[:, c * LANES:(c + 1) * LANES]
        sq = blk * blk
        s0 = jnp.sum(jnp.where(lo_half, sq, 0.0), axis=-1, keepdims=True)
        s1 = jnp.sum(jnp.where(lo_half, 0.0, sq), axis=-1, keepdims=True)
        den = jnp.where(lo_half, jnp.maximum(jnp.sqrt(s0), 1e-12), jnp.maximum(jnp.sqrt(s1), 1e-12))
        kkn = blk / den
        kkn_ref[rs, c * LANES:(c + 1) * LANES] = kkn
        kka_ref[rs, c * LANES:(c + 1) * LANES] = kkn * a[:, c * LANES:(c + 1) * LANES]
    r_ref[rs, :] = r
    vb_ref[rs, :] = v
    km_ref[rs, :] = k * (1.0 + (a - 1.0) * ka_ref[...])


def _front_call(x3d, prev0, tabs, wts, *, sample):
    nseq, seq_len, d = x3d.shape
    x2d = x3d.reshape(nseq * seq_len, d)
    xlast = x3d[:, seq_len - 1, :]
    n = x2d.shape[0]
    if sample:
        tm = n
        tiles_per_seq = 1
    else:
        tm = FRONT_TM
        assert seq_len % tm == 0
        tiles_per_seq = seq_len // tm
    grid = (n // tm,)
    ttab = tabs.shape[1]
    tab_blocks = ttab // tm
    row = lambda i: (i, 0)
    const2 = lambda i: (0, 0)
    prev_spec = pl.BlockSpec((tm, N_SH), row) if sample else pl.BlockSpec((1, N_SH), const2)
    xl_shape = jax.ShapeDtypeStruct((nseq, d), F32)
    xl_spec = pl.BlockSpec((nseq, d), const2)
    scratch = [pltpu.VMEM((tm + 8, N_SH), F32)]
    in_specs = [pl.BlockSpec((tm, d), row), pl.BlockSpec((nseq, d), const2), prev_spec,
                pl.BlockSpec((3, tm, LANES), lambda i: (0, i % tab_blocks, 0))]
    in_specs += [pl.BlockSpec(w.shape, const2, pipeline_mode=pl.Buffered(1)) for w in wts]
    out_shapes = [
        jax.ShapeDtypeStruct((n, HCAT), BF16), jax.ShapeDtypeStruct((n, HCAT), BF16),
        jax.ShapeDtypeStruct((n, HCAT), BF16), jax.ShapeDtypeStruct((n, KV_LORA), F32),
        jax.ShapeDtypeStruct((n, ROPE), F32), jax.ShapeDtypeStruct((n, W_A), F32),
        jax.ShapeDtypeStruct((n, W_B), F32),
    ] + [jax.ShapeDtypeStruct((n, W_B), F32)] * 6 + [xl_shape]
    out_specs = [pl.BlockSpec((tm, s.shape[1]), row) for s in out_shapes[:-1]] + [xl_spec]
    sub_rows = min(FRONT_SUB, tm)
    assert tm % sub_rows == 0
    body = functools.partial(_front_body, tm=tm, sub_rows=sub_rows, tiles_per_seq=tiles_per_seq, seq_len=seq_len,
                             sample=sample)
    return pl.pallas_call(
        body, grid=grid, in_specs=in_specs, out_specs=out_specs, out_shape=out_shapes,
        scratch_shapes=scratch, compiler_params=_cparams(("arbitrary",)),
        name="front_sample" if sample else "front_prompt",
    )(x2d, xlast, prev0, tabs, *wts)


def _matmul_body(a_ref, b_ref, o_ref):
    o_ref[...] = jnp.dot(a_ref[...].astype(BF16), b_ref[...], preferred_element_type=F32)


def _matmul_call(a, b16):
    m, k = a.shape
    n = b16.shape[1]
    return pl.pallas_call(
        _matmul_body, grid=(1,),
        in_specs=[pl.BlockSpec((m, k), lambda i: (0, 0)), pl.BlockSpec((k, n), lambda i: (0, 0))],
        out_specs=pl.BlockSpec((m, n), lambda i: (0, 0)),
        out_shape=jax.ShapeDtypeStruct((m, n), F32),
        compiler_params=_cparams(("arbitrary",)), name="prev_proj",
    )(a, b16)


def _p_attn_body(q_ref, k_ref, v_ref, o_ref, bias_scr, *, tq, tk):
    qi = pl.program_id(2)
    lane = _lane_iota()

    @pl.when((pl.program_id(0) == 0) & (pl.program_id(1) == 0) & (qi == 0))
    def _():
        rows = lax.broadcasted_iota(jnp.int32, (tq, tk), 0)
        cols = lax.broadcasted_iota(jnp.int32, (tq, tk), 1)
        for d in range(tq // tk):
            bias_scr[d] = jnp.where(cols + d * tk <= rows, 0.0, NEG)

    qs = [q_ref[:, hh * LANES:(hh + 1) * LANES] for hh in range(2)]
    nsub = tq // tk

    def scores(j, hh):
        kb = k_ref[pl.ds(pl.multiple_of(j * tk, tk), tk), hh * LANES:(hh + 1) * LANES]
        return _nt(qs[hh], kb)

    def values(j, hh):
        return v_ref[pl.ds(pl.multiple_of(j * tk, tk), tk), hh * LANES:(hh + 1) * LANES]

    def update(carry, s, vb):
        m, acc = carry
        m_new = jnp.maximum(m, jnp.max(s, axis=-1, keepdims=True))
        e16 = jnp.exp2(s - m_new).astype(BF16)
        return m_new, acc * jnp.exp2(m - m_new) + jnp.dot(e16, vb, preferred_element_type=F32)

    def step(j, carry):
        ss = [scores(j, hh) for hh in range(2)]
        return tuple(update(carry[hh], ss[hh], values(j, hh)) for hh in range(2))

    init =(jnp.full((tq, 1), NEG, F32), jnp.zeros((tq, LANES), F32))
    carry = lax.fori_loop(0, qi * nsub, step, (init, init))
    for d in range(nsub):
        j = qi * nsub + d
        ss = [scores(j, hh) + bias_scr[d] for hh in range(2)]
        carry = tuple(update(carry[hh], ss[hh], values(j, hh)) for hh in range(2))
    outs = [carry[hh][1] / carry[hh][1][:, V_DIM:V_DIM + 1] for hh in range(2)]
    o_ref[...] = jnp.where(lane < V_DIM, outs[0], pltpu.roll(outs[1], V_DIM, axis=1))


def _p_attn_call(qcat, kcat, v16, *, nseq, seq_len):
    n = qcat.shape[0]
    tq = min(ATTN_TQ, seq_len)
    tk = min(ATTN_TK, tq)
    assert seq_len % tq == 0 and tq % tk == 0
    nq = seq_len // tq
    grid = (nseq, H_A // 2, nq)
    return pl.pallas_call(
        functools.partial(_p_attn_body, tq=tq, tk=tk), grid=grid,
        in_specs=[pl.BlockSpec((tq, 2 * LANES), lambda b, h, i: (b * nq + i, h)),
                  pl.BlockSpec((seq_len, 2 * LANES), lambda b, h, i: (b, h)),
                  pl.BlockSpec((seq_len, 2 * LANES), lambda b, h, i: (b, h))],
        out_specs=pl.BlockSpec((tq, LANES), lambda b, h, i: (b * nq + i, h)),
        out_shape=jax.ShapeDtypeStruct((n, W_A), F32),
        scratch_shapes=[pltpu.VMEM((tq // tk, tq, tk), F32)],
        compiler_params=_cparams(("arbitrary", "arbitrary", "arbitrary")), name="p_attn",
    )(qcat, kcat, v16)


def _s_attn_body(pt_ref, qn_ref, qp_ref, cnew_ref, pnew_ref, wkt_ref, wuv_ref, ckv_hbm, kpe_hbm, o_ref,
                 *scr, pg, sub, layer, ngroups):
    bufs = tuple((scr[2 * g], scr[2 * g + 1]) for g in range(ngroups))
    sem = scr[2 * ngroups]
    m_scr, l_scr, acc_scr = scr[2 * ngroups + 1:]
    b = pl.program_id(0)
    last = pl.num_programs(0) - 1
    nrow = qn_ref.shape[0]

    def start_group(bb, sl):
        cbuf, pbuf = bufs[sl]
        for t in range(pg):
            idx = pt_ref[bb, sl * pg + t]
            pltpu.make_async_copy(ckv_hbm.at[layer, idx], cbuf.at[t], sem.at[0, sl]).start()
            pltpu.make_async_copy(kpe_hbm.at[layer, idx], pbuf.at[t], sem.at[1, sl]).start()

    def wait_group(sl):
        cbuf, pbuf = bufs[sl]
        pltpu.make_async_copy(cbuf, cbuf, sem.at[0, sl]).wait()
        pltpu.make_async_copy(pbuf, pbuf, sem.at[1, sl]).wait()

    @pl.when(b == 0)
    def _():
        for g in range(ngroups):
            start_group(b, g)

    m_scr[...] = jnp.full(m_scr.shape, NEG, F32)
    l_scr[...] = jnp.zeros(l_scr.shape, F32)
    acc_scr[...] = jnp.zeros(acc_scr.shape, F32)

    def attend(c16s, pt16s, masks):
        krts = [_nt(wkt_ref[...], c16) for c16 in c16s]
        s_list = []
        for krt, pt16, mask in zip(krts, pt16s, masks):
            ssq = jnp.concatenate(
                [jnp.sum(jnp.square(krt[h * NOPE:(h + 1) * NOPE, :]), axis=0, keepdims=True) for h in range(H_A)],
                axis=0)
            rinv = lax.rsqrt(ssq * (1.0 / NOPE) + EPS)
            sraw = jnp.dot(qn_ref[...], krt.astype(BF16), preferred_element_type=F32)
            spe = jnp.dot(qp_ref[...], pt16, preferred_element_type=F32)
            s = sraw * jnp.concatenate([rinv] * (nrow // H_A), axis=0) + spe
            if mask is not None:
                s = jnp.where(mask, s, NEG)
            s_list.append(s)
        m, l, acc = m_scr[...], l_scr[...], acc_scr[...]
        for s, c16 in zip(s_list, c16s):
            m_new = jnp.maximum(m, jnp.max(s, axis=-1, keepdims=True))
            alpha = jnp.exp(m - m_new)
            e = jnp.exp(s - m_new)
            l = l * alpha + jnp.sum(e, axis=-1, keepdims=True)
            acc = acc * alpha + jnp.dot(e.astype(BF16), c16, preferred_element_type=F32)
            m = m_new
        l_scr[...] = l
        acc_scr[...] = acc
        m_scr[...] = m

    def attend_buffer(sl, with_new):
        cbuf, pbuf = bufs[sl]
        page = cbuf.shape[1]
        c16s, pt16s = [], []
        for g in range(pg // sub):
            c16s.append(cbuf[g * sub:(g + 1) * sub].reshape(sub * page, KV_LORA).astype(BF16))
            pt16s.append(jnp.concatenate([pbuf[t].astype(BF16) for t in range(g * sub, (g + 1) * sub)], axis=1))
        masks = [None] * len(c16s)
        if with_new:
            nnew = cnew_ref.shape[0]
            rows = lax.broadcasted_iota(jnp.int32, (nrow, nnew), 0) // H_A
            cols = lax.broadcasted_iota(jnp.int32, (nrow, nnew), 1)
            c16s.append(cnew_ref[...].astype(BF16))
            pt16s.append(pnew_ref[...].astype(BF16))
            masks.append(cols <= rows)
        attend(c16s, pt16s, masks)

    nxt = jnp.minimum(b + 1, last)
    for g in range(ngroups):
        wait_group(g)
        attend_buffer(g, with_new=(g == ngroups - 1))
        start_group(nxt, g)

    o_lat = acc_scr[...] / l_scr[...]
    hi, lo = _split2(o_lat)
    o_ref[...] = (jnp.dot(hi, wuv_ref[...], preferred_element_type=F32)
                  + jnp.dot(lo, wuv_ref[...], preferred_element_type=F32))

    @pl.when(b == last)
    def _():
        for g in range(ngroups):
            wait_group(g)


def _s_attn_call(page_table, qn_blk, qp_blk, cnew, pnew_t, wkt16, wuv16, cache_ckv, cache_kpe_t, layer):
    bd, nrow, _ = qn_blk.shape
    n_pages = page_table.shape[1]
    page = cache_ckv.shape[2]
    pg = min(PAGES_PER_GROUP, n_pages)
    sub = min(PAGES_PER_SUB, pg)
    assert n_pages % pg == 0 and pg % sub == 0
    ngroups = n_pages // pg
    nnew = cnew.shape[1]
    grid = (bd,)
    seq3 = lambda b, pt: (b, 0, 0)
    const2 = lambda b, pt: (0, 0)

    in_specs = [pl.BlockSpec((None, nrow, W_A), seq3), pl.BlockSpec((None, nrow, ROPE), seq3),
                pl.BlockSpec((None, nnew, KV_LORA), seq3), pl.BlockSpec((None, ROPE, nnew), seq3),
                pl.BlockSpec(wkt16.shape, const2), pl.BlockSpec(wuv16.shape, const2),
                pl.BlockSpec(memory_space=pl.ANY), pl.BlockSpec(memory_space=pl.ANY)]
    gs = pltpu.PrefetchScalarGridSpec(
        num_scalar_prefetch=1, grid=grid, in_specs=in_specs,
        out_specs=pl.BlockSpec((None, nrow, W_A), seq3),
        scratch_shapes=[pltpu.VMEM((pg, page, KV_LORA), F32), pltpu.VMEM((pg, ROPE, page), F32)] * ngroups
        + [pltpu.SemaphoreType.DMA((2, ngroups)),
           pltpu.VMEM((nrow, 1), F32), pltpu.VMEM((nrow, 1), F32), pltpu.VMEM((nrow, KV_LORA), F32)])
    return pl.pallas_call(
        functools.partial(_s_attn_body, pg=pg, sub=sub, layer=layer, ngroups=ngroups), grid_spec=gs,
        out_shape=jax.ShapeDtypeStruct((bd, nrow, W_A), F32),
        compiler_params=_cparams(("arbitrary",)), name="s_attn",
    )(page_table, qn_blk, qp_blk, cnew, pnew_t, wkt16, wuv16, cache_ckv, cache_kpe_t)


def _rwkv_body(r_ref, lw_ref, km_ref, v_ref, kk_ref, kka_ref, gb_ref, s0_ref, rk_ref, lnw_ref, lnb_ref,
               o_ref, sout_ref, s_scr, *, c, nb):
    ci = pl.program_id(1)
    nc = pl.num_programs(1)
    g4 = RW_GROUP
    items = [(s, g) for s in range(nb) for g in range(RW_NGROUPS)]

    rr = lax.broadcasted_iota(jnp.int32, (g4, g4), 0) // N_B
    cc = lax.broadcasted_iota(jnp.int32, (g4, g4), 1) // N_B
    blockmask = rr == cc
    ones_blk = blockmask.astype(BF16)
    t_idx = lax.broadcasted_iota(jnp.int32, (c, g4), 0)
    i_idx = lax.broadcasted_iota(jnp.int32, (c, g4), 1) % N_B
    low_strict = i_idx < t_idx
    low_incl = i_idx <= t_idx
    eye = (i_idx == t_idx).astype(F32)
    lane_half = [_lane_iota() < N_B, _lane_iota() >= N_B]

    def bd_rows(y):
        y16 = y.astype(BF16)
        zero = jnp.zeros((c, LANES), BF16)
        blocks = []
        for h in range(g4 // N_B):
            col = y16[:, (h // 2) * LANES:(h // 2 + 1) * LANES]
            piece = jnp.where(lane_half[h % 2], col, zero)
            blocks.append(jnp.concatenate([piece, zero] if h < 2 else [zero, piece], axis=1))
            if c < N_B:
                blocks.append(jnp.zeros((N_B - c, g4), BF16))
        return jnp.concatenate(blocks, axis=0)

    def mmh(x, y):
        return jnp.dot(x.astype(BF16), bd_rows(y), preferred_element_type=F32)

    @pl.when(ci == 0)
    def _():
        s_scr[...] = jnp.zeros(s_scr.shape, F32)
        for s in range(nb):
            for h in range(H_B):
                g, jh = divmod(h, 4)
                s_scr[s, g, jh * N_B:(jh + 1) * N_B, jh * N_B:(jh + 1) * N_B] = s0_ref[s, h]

    tri = (lax.broadcasted_iota(jnp.int32, (c, c), 1) <= lax.broadcasted_iota(jnp.int32, (c, c), 0)).astype(BF16)
    g_seq = []
    for s in range(nb):
        g_seq.append(sum(jnp.dot(tri, part, preferred_element_type=F32) for part in _split3(lw_ref[s])))

    st = []
    for (s, g) in items:
        sl = slice(g * g4, (g + 1) * g4)
        lw = lw_ref[s, :, sl]
        gc = g_seq[s][:, sl]
        big = jnp.exp(gc)
        ginv = jnp.exp(-gc)
        at = -kk_ref[s, :, sl] * jnp.exp(gc - lw)
        bt = kka_ref[s, :, sl] * ginv
        kt = km_ref[s, :, sl] * ginv
        rt = r_ref[s, :, sl] * big
        s_old = s_scr[s, g]
        st.append(dict(sl=sl, s=s, g=g, bt=bt, kt=kt, g_last=big[c - 1:c, :], s_old=s_old,
                       lhs2=jnp.concatenate([at, rt], axis=0).astype(BF16), v=v_ref[s, :, sl]))

    for d in st:
        x1 = _nt(d["lhs2"], bd_rows(d["bt"]))
        x2 = _nt(d["lhs2"], bd_rows(d["kt"]))
        d["x3"] = _nt(d["lhs2"], d["s_old"].astype(BF16))
        d["a_ab"] = jnp.where(low_strict, x1[:c], 0.0)
        d["a_rb"] = jnp.where(low_incl, x1[c:], 0.0)
        d["a_ak"] = jnp.where(low_strict, x2[:c], 0.0)
        d["a_rk"] = jnp.where(low_incl, x2[c:], 0.0)
    for d in st:
        x4 = jnp.dot(jnp.concatenate([d["a_ak"], d["a_rk"]], axis=0).astype(BF16), bd_rows(d["v"]),
                     preferred_element_type=F32)
        d["b"] = x4[:c] + d["x3"][:c]
        d["ypart"] = x4[c:] + d["x3"][c:]

    if c <= 16:
        for d in st:
            d["tinv"] = eye + d["a_ab"]
            d["pw"] = d["a_ab"]
        n = 1
        while 2 * n < c:
            for d in st:
                d["pw"] = mmh(d["pw"], d["pw"])
            for d in st:
                d["tinv"] = mmh(d["tinv"], eye + d["pw"])
            n *= 2
        for d in st:
            d["u"] = mmh(d["tinv"], d["b"])
    else:
        same16 = (i_idx // 16) == (t_idx // 16)
        for d in st:
            dg = jnp.where(same16, d["a_ab"], 0.0)
            d["lo"] = d["a_ab"] - dg
            d["td"] = eye + dg
            d["pw"] = dg
        for _ in range(3):
            for d in st:
                d["pw"] = mmh(d["pw"], d["pw"])
            for d in st:
                d["td"] = mmh(d["td"], eye + d["pw"])
        for d in st:
            d["nn"] = mmh(d["td"], d["lo"])
            d["w"] = mmh(d["td"], d["b"])
        for d in st:
            d["n2"] = mmh(d["nn"], d["nn"])
        for d in st:
            d["w2"] = d["w"] + mmh(d["n2"], d["w"])
        for d in st:
            d["u"] = d["w2"] + mmh(d["nn"], d["w2"])

    for d in st:
        d["y"] = d["ypart"] + mmh(d["a_rb"], d["u"])
        lhs_t = jnp.concatenate([d["v"], d["u"]], axis=0).astype(BF16)
        rhs_t = jnp.concatenate([d["kt"] * d["g_last"], d["bt"] * d["g_last"]], axis=0).astype(BF16)
        upd = lax.dot_general(lhs_t, rhs_t, (((0,), (0,)), ((), ())), preferred_element_type=F32)
        s_scr[d["s"], d["g"]] = d["s_old"] * d["g_last"] + jnp.where(blockmask, upd, 0.0)

    def head_sums(xs):
        parts = []
        for x in xs:
            parts.extend(_split2(x))
        res = jnp.dot(jnp.concatenate(parts, axis=0), ones_blk, preferred_element_type=F32)
        return [res[(2 * k) * c:(2 * k + 1) * c] + res[(2 * k + 1) * c:(2 * k + 2) * c] for k in range(len(xs))]

    rkr = [r_ref[d["s"], :, d["sl"]] * km_ref[d["s"], :, d["sl"]] * rk_ref[:, d["sl"]] for d in st]
    sums = head_sums([d["y"] for d in st] + rkr)
    devs = [d["y"] - sums[k] * (1.0 / N_B) for k, d in enumerate(st)]
    var_sums = head_sums([dv * dv for dv in devs])
    for k, d in enumerate(st):
        sl = d["sl"]
        yn = devs[k] * lax.rsqrt(var_sums[k] * (1.0 / N_B) + LNX_EPS) * lnw_ref[:, sl] + lnb_ref[:, sl]
        bonus = sums[len(st) + k] * d["v"]
        gb = gb_ref[d["s"], :, sl]
        o_ref[d["s"], :, sl] = (yn + bonus) * (gb * jax.nn.sigmoid(gb))

    @pl.when(ci == nc - 1)
    def _():
        for s in range(nb):
            for h in range(H_B):
                g, jh = divmod(h, 4)
                sout_ref[s, h] = s_scr[s, g, jh * N_B:(jh + 1) * N_B, jh * N_B:(jh + 1) * N_B]


def _rwkv_call(r, lw, km, v, kk, kka, gb, s0, rk, lnw, lnb, *, c, nb):
    nseq, t, _ = r.shape
    nc = t // c
    assert nseq % nb == 0 and t % c == 0
    tok = pl.BlockSpec((nb, c, W_B), lambda b, i: (b, i, 0))
    vec = pl.BlockSpec((1, W_B), lambda b, i: (0, 0))
    st = pl.BlockSpec((nb, H_B, N_B, N_B), lambda b, i: (b, 0, 0, 0))
    return pl.pallas_call(
        functools.partial(_rwkv_body, c=c, nb=nb), grid=(nseq // nb, nc),
        in_specs=[tok] * 7 + [st, vec, vec, vec],
        out_specs=[tok, st],
        out_shape=[jax.ShapeDtypeStruct((nseq, t, W_B), F32), jax.ShapeDtypeStruct((nseq, H_B, N_B, N_B), F32)],
        scratch_shapes=[pltpu.VMEM((nb, RW_NGROUPS, RW_GROUP, RW_GROUP), F32)],
        compiler_params=_cparams(("arbitrary", "arbitrary")), name=f"rwkv_c{c}",
    )(r, lw, km, v, kk, kka, gb, s0, rk, lnw, lnb)


def _back_body(x_ref, oa_ref, ga_ref, ob_ref, p_ref, woa_ref, wob_ref, wple_ref, pn_ref, wg_ref, y_ref):
    ga = ga_ref[...]
    mixed_a = oa_ref[...] * (ga * jax.nn.sigmoid(ga))
    x1 = (x_ref[...] + jnp.dot(mixed_a.astype(BF16), woa_ref[...], preferred_element_type=F32)
          + jnp.dot(ob_ref[...].astype(BF16), wob_ref[...], preferred_element_type=F32))
    xg = _rms(x1, pn_ref[...], x1.shape[-1])
    gate = jax.nn.sigmoid(jnp.dot(xg.astype(BF16), wg_ref[...], preferred_element_type=F32))
    y_ref[...] = x1 + jnp.dot(p_ref[...].astype(BF16), wple_ref[...], preferred_element_type=F32) * gate


def _back_call(x2d, oa, ga, ob, p2d, wts):
    n, d = x2d.shape
    tm = min(BACK_TM, n)
    row = lambda i: (i, 0)
    const2 = lambda i: (0, 0)
    ins = [x2d, oa, ga, ob, p2d]
    return pl.pallas_call(
        _back_body, grid=(n // tm,),
        in_specs=[pl.BlockSpec((tm, a.shape[1]), row) for a in ins] + [pl.BlockSpec(w.shape, const2) for w in wts],
        out_specs=pl.BlockSpec((tm, d), row),
        out_shape=jax.ShapeDtypeStruct((n, d), F32),
        compiler_params=_cparams(("arbitrary",)), name="back",
    )(*ins, *wts)


def _pad_cols(a, width):
    return jnp.pad(a, ((0, 0), (0, width - a.shape[1])))


def _row(vec):
    return vec.reshape(1, -1).astype(F32)


def _rope_tabs(pos):
    inv = ROPE_BASE ** (-jnp.arange(0, ROPE, 2, dtype=F32) / ROPE)
    ang = pos.astype(F32)[:, None] * inv[None, :]
    cos, sin = jnp.cos(ang), jnp.sin(ang)
    t = pos.shape[0]
    z16 = jnp.zeros((t, ROPE // 2), F32)
    c = jnp.concatenate([cos, cos, jnp.ones((t, NOPE), F32), jnp.zeros((t, LANES - ROPE - NOPE), F32)], axis=1)
    s1 = _pad_cols(jnp.concatenate([-sin, z16], axis=1), LANES)
    s2 = _pad_cols(jnp.concatenate([z16, sin], axis=1), LANES)
    return jnp.stack([c, s1, s2])


def _layer_weights(i, norm_in, w_in, mu_shift, q_a_norm, w_uq, kv_a_norm, w_ukv, g_q_nope, g_q_pe, g_k_nope,
                   g_k_pe, w_decay0, w_decay_up, a0, w_iclr_up, k_k, k_a):
    w = w_in[i]
    c_q, c_kv, c_kpe = 0, Q_LORA, Q_LORA + KV_LORA
    c_ga = c_kpe + ROPE
    c_gb = c_ga + W_A
    c_sh = c_gb + W_B
    c_xw = c_sh + 3 * W_B
    c_xa = c_xw + DECAY_LORA
    w_perm = jnp.concatenate([
        w[:, c_q:c_kpe], _pad_cols(w[:, c_kpe:c_ga], LANES), w[:, c_ga:c_sh], w[:, c_sh:c_xw],
        _pad_cols(w[:, c_xw:c_xa], LANES), _pad_cols(w[:, c_xa:], LANES)], axis=1).astype(BF16)
    mu = mu_shift[i][None, :]
    mu_perm = jnp.concatenate([mu[:, :3 * W_B], _pad_cols(mu[:, 3 * W_B:3 * W_B + DECAY_LORA], LANES),
                               _pad_cols(mu[:, 3 * W_B + DECAY_LORA:], LANES)], axis=1)
    uq = w_uq[i]
    uq_cat = jnp.concatenate([uq[..., NOPE:], uq[..., :NOPE],
                              jnp.zeros((Q_LORA, H_A, LANES - NOPE - ROPE), F32)], axis=-1)
    uq_cat = uq_cat.reshape(Q_LORA, HCAT).astype(BF16)
    ukv = w_ukv[i]
    uk = ukv[..., :NOPE]
    uk_cat = jnp.concatenate([jnp.zeros((KV_LORA, H_A, ROPE), F32), uk,
                              jnp.zeros((KV_LORA, H_A, LANES - NOPE - ROPE), F32)], axis=-1)
    uk_cat = uk_cat.reshape(KV_LORA, HCAT).astype(BF16)
    uv = ukv[..., NOPE:].reshape(KV_LORA, W_A).astype(BF16)
    uv_cat = jnp.concatenate([ukv[..., NOPE:], jnp.zeros((KV_LORA, H_A, LANES - V_DIM), F32)], axis=-1)
    uv_cat = uv_cat.reshape(KV_LORA, HCAT).astype(BF16)
    ukt = jnp.transpose(uk.reshape(KV_LORA, H_A * NOPE)).astype(BF16)
    zpad = jnp.zeros((LANES - NOPE - ROPE,), F32)
    gq_prompt = jnp.concatenate([g_q_pe[i], g_q_nope[i], zpad]) * (SCALE * LOG2E)
    gq_sample = jnp.concatenate([g_q_pe[i], g_q_nope[i] * g_k_nope[i], zpad]) * SCALE
    gk = jnp.concatenate([jnp.zeros((ROPE,), F32), g_k_nope[i], zpad])
    gkp = jnp.concatenate([g_k_pe[i], jnp.zeros((LANES - ROPE,), F32)])
    wdu = jnp.pad(w_decay_up[i], ((0, LANES - DECAY_LORA), (0, 0))).astype(BF16)
    wiu = jnp.pad(w_iclr_up[i], ((0, LANES - ICLR_LORA), (0, 0))).astype(BF16)

    def front_wts(gq):
        return [_row(norm_in[i]), w_perm, mu_perm, _row(q_a_norm[i]), uq_cat, _row(kv_a_norm[i]), uk_cat, uv_cat,
                _row(gq), _row(gk), _row(gkp), _row(w_decay0[i]), wdu, _row(a0[i]), wiu, _row(k_k[i]), _row(k_a[i])]

    return front_wts(gq_prompt), front_wts(gq_sample), w_perm[:, P_SH:], ukt, uv


def kernel(x_prompt, x_sample, p_prompt, p_sample, cache_ckv, cache_kpe, state_wkv, state_shift, page_table,
           norm_in, w_in, mu_shift, q_a_norm, w_uq, kv_a_norm, w_ukv, g_q_nope, g_q_pe, g_k_nope, g_k_pe,
           w_decay0, w_decay_up, a0, w_iclr_up, k_k, k_a, r_k, lnx_w, lnx_b, w_out, w_ple, ple_norm, w_ple_gate):
    b, t, d = x_prompt.shape
    bd, tq, _ = x_sample.shape
    depth = w_in.shape[0]
    page = cache_ckv.shape[2]
    past_len = page_table.shape[1] * page
    tabs_p = _rope_tabs(jnp.arange(t))
    tabs_s = jnp.tile(_rope_tabs(past_len + jnp.arange(tq)), (1, bd, 1))
    cache_kpe_t = jnp.swapaxes(cache_kpe, 2, 3)
    y_p = x_prompt.reshape(b * t, d)
    y_s = x_sample.reshape(bd * tq, d)
    outs = [[] for _ in range(8)]
    for i in range(depth):
        wts_p, wts_s, w_shift16, ukt16, uv16 = _layer_weights(
            i, norm_in, w_in, mu_shift, q_a_norm, w_uq, kv_a_norm, w_ukv, g_q_nope, g_q_pe, g_k_nope, g_k_pe,
            w_decay0, w_decay_up, a0, w_iclr_up, k_k, k_a)
        rw_vecs = [_row(r_k[i]), _row(lnx_w[i]), _row(lnx_b[i])]
        wo = w_out[i].astype(BF16)
        back_wts = [wo[:W_A], wo[W_A:], w_ple[i].astype(BF16), _row(ple_norm[i]), w_ple_gate[i].astype(BF16)]

        (qcat, kcat, v16, ckv, kpe, ga, gb, r, lw, km, vb, kkn, kka, xl) = _front_call(
            y_p.reshape(b, t, d), jnp.zeros((1, N_SH), F32), tabs_p, wts_p, sample=False)
        o_a = _p_attn_call(qcat, kcat, v16, nseq=b, seq_len=t)
        seq3 = lambda a: a.reshape(b, t, W_B)
        o_b, s_p = _rwkv_call(seq3(r), seq3(lw), seq3(km), seq3(vb), seq3(kkn), seq3(kka), seq3(gb),
                              jnp.zeros((b, H_B, N_B, N_B), F32), *rw_vecs, c=min(RW_CHUNK, t), nb=8)
        y_p = _back_call(y_p, o_a, ga, o_b.reshape(b * t, W_B), p_prompt[i].reshape(b * t, -1), back_wts)
        outs[0].append(ckv.reshape(b, t, KV_LORA)); outs[2].append(kpe.reshape(b, t, ROPE))
        outs[4].append(s_p); outs[6].append(xl)

        prev_proj = _matmul_call(state_shift[i], w_shift16)
        prev0 = jnp.pad(prev_proj[:, None, :], ((0, 0), (0, tq - 1), (0, 0))).reshape(bd * tq, N_SH)
        (qcat, _, _, ckv, kpe, ga, gb, r, lw, km, vb, kkn, kka, xl) = _front_call(
            y_s.reshape(bd, tq, d), prev0, tabs_s, wts_s, sample=True)
        q4 = qcat.reshape(bd, tq, H_A, LANES)
        qp_blk = q4[..., :ROPE].reshape(bd, tq * H_A, ROPE)
        eye_h = jnp.eye(H_A, dtype=BF16)
        qn_blk = (q4[..., ROPE:ROPE + NOPE][:, :, :, None, :] * eye_h[None, None, :, :, None]
                  ).reshape(bd, tq * H_A, W_A)
        npad = LANES
        cnew = jnp.pad(ckv.reshape(bd, tq, KV_LORA), ((0, 0), (0, npad - tq), (0, 0)))
        pnew_t = jnp.swapaxes(jnp.pad(kpe.reshape(bd, tq, ROPE), ((0, 0), (0, npad - tq), (0, 0))), 1, 2)
        o_full = _s_attn_call(page_table, qn_blk, qp_blk, cnew, pnew_t, ukt16, uv16, cache_ckv, cache_kpe_t, i)
        o5 = o_full.reshape(bd, tq, H_A, H_A, V_DIM)
        o_a = jnp.einsum('bqhhd->bqhd', o5).reshape(bd * tq, W_A)
        cpad = 16

        def pad_tok(a):
            return jnp.pad(a.reshape(bd, tq, -1), ((0, 0), (0, cpad - tq), (0, 0)))

        o_b16, s_s = _rwkv_call(pad_tok(r), pad_tok(lw), pad_tok(km), pad_tok(vb), pad_tok(kkn), pad_tok(kka),
                                pad_tok(gb), state_wkv[i], *rw_vecs, c=cpad, nb=8)
        o_b = o_b16[:, :tq].reshape(bd * tq, W_B)
        y_s = _back_call(y_s, o_a, ga, o_b, p_sample[i].reshape(bd * tq, -1), back_wts)
        outs[1].append(ckv.reshape(bd, tq, KV_LORA)); outs[3].append(kpe.reshape(bd, tq, ROPE))
        outs[5].append(s_s); outs[7].append(xl)
    st = [jnp.stack(o) for o in outs]
    return (y_p.reshape(b, t, d), y_s.reshape(bd, tq, d), st[0], st[1], st[2], st[3], st[4], st[5], st[6], st[7])
```
